```python
import math
import jax
import jax.numpy as jnp
from jax import lax
import numpy as np

D_MODEL = 2048
BATCH = 8
SEQ = 2048
DEPTH = 1

MIX_WIDTH = D_MODEL
ATTN_WIDTH = MIX_WIDTH // 2
ATTN_HEAD_DIM = 128
N_ATTN_HEADS = ATTN_WIDTH // ATTN_HEAD_DIM
RET_WIDTH = MIX_WIDTH - ATTN_WIDTH
RET_HEAD_DIM = 256
N_RET_HEADS = RET_WIDTH // RET_HEAD_DIM
RET_CHUNK = 128
DILATED_PATTERNS = ((128, 1), (512, 4), (2048, 16))
FFN_HIDDEN = ((8 * D_MODEL // 3 + 255) // 256) * 256
IN_PROJ_WIDTH = 3 * ATTN_WIDTH + 4 * RET_WIDTH
NORM_EPS = 1e-6

kernel_name = "hybrid_dilated_attn_retention_block"


def _rmsnorm(x, w):
    x32 = x.astype(jnp.float32)
    y = x32 * lax.rsqrt(jnp.mean(x32 * x32, axis=-1, keepdims=True) + NORM_EPS)
    return (y * w.astype(jnp.float32)).astype(x.dtype)


def _alibi_slopes(n_heads):
    return jnp.exp2(-8.0 * jnp.arange(1, n_heads + 1, dtype=jnp.float32) / n_heads)


def _dilated_branch(q, k, v, slopes, window, dilation):
    B, H, S, hd = q.shape
    blk = window // dilation
    span = dilation * blk
    sp = -(-S // span) * span
    L = sp // dilation
    nb = L // blk

    def to_sub(t):
        t = jnp.pad(t, ((0, 0), (0, 0), (0, sp - S), (0, 0)))
        t = t.reshape(B, H, L, dilation, hd).transpose(0, 1, 3, 2, 4)
        return t.reshape(B, H, dilation, nb, blk, hd)

    def two_block(t):
        prev = jnp.concatenate([jnp.zeros_like(t[:, :, :, :1]), t[:, :, :, :-1]], axis=3)
        return jnp.concatenate([prev, t], axis=4)

    qb = to_sub(q)
    kk = two_block(to_sub(k))
    vv = two_block(to_sub(v))
    s = jnp.einsum('bhrnqd,bhrnkd->bhrnqk', qb, kk).astype(jnp.float32) * (1.0 / math.sqrt(hd))
    qi = jnp.arange(blk)[:, None]
    kj = jnp.arange(2 * blk)[None, :]
    diff = qi - kj + blk
    key_idx = jnp.arange(nb)[:, None, None] * blk + kj[None] - blk
    valid = (diff >= 0) & (diff <= blk) & (key_idx >= 0)
    bias = -slopes[:, None, None] * (diff * dilation).astype(jnp.float32)
    s = s + bias[None, :, None, None]
    s = jnp.where(valid, s, -jnp.inf)
    lse = jax.nn.logsumexp(s, axis=-1)
    p = jnp.exp(s - lse[..., None])
    o = jnp.einsum('bhrnqk,bhrnkd->bhrnqd', p.astype(v.dtype), vv)

    def from_sub(t):
        tail = t.shape[5:]
        t = t.reshape((B, H, dilation, L) + tail)
        t = jnp.moveaxis(t, 2, 3)
        return t.reshape((B, H, sp) + tail)[:, :, :S]

    return from_sub(o), from_sub(lse)


def _dilated_attention(q, k, v):
    slopes = _alibi_slopes(q.shape[1])
    outs, lses = [], []
    for window, dilation in DILATED_PATTERNS:
        o, l = _dilated_branch(q, k, v, slopes, window, dilation)
        outs.append(o.astype(jnp.float32))
        lses.append(l)
    wts = jax.nn.softmax(jnp.stack(lses), axis=0)
    return jnp.sum(wts[..., None] * jnp.stack(outs), axis=0)


def _retention_chunkwise(q, k, v):
    B, H, S, dh = q.shape
    C = RET_CHUNK
    nc = S // C
    log_gamma = jnp.log(1.0 - jnp.exp2(-5.0 - jnp.arange(H, dtype=jnp.float32)))
    k = k * (1.0 / math.sqrt(dh))
    qc = q.reshape(B, H, nc, C, dh)
    kc = k.reshape(B, H, nc, C, dh)
    vc = v.reshape(B, H, nc, C, dh)
    idx = jnp.arange(C, dtype=jnp.float32)
    dif = idx[:, None] - idx[None, :]
    decay = jnp.where(dif >= 0, jnp.exp(log_gamma[:, None, None] * jnp.maximum(dif, 0.0)), 0.0)
    scores = jnp.einsum('bhnid,bhnjd->bhnij', qc, kc) * decay[None, :, None]
    inner = jnp.einsum('bhnij,bhnjd->bhnid', scores, vc)
    zeta = jnp.exp(log_gamma[:, None] * (C - 1.0 - idx))
    kv = jnp.einsum('bhnjd,bhnje->bhnde', kc * zeta[None, :, None, :, None], vc)
    gamma_chunk = jnp.exp(log_gamma * C)[None, :, None, None]

    def step(state, kv_n):
        return state * gamma_chunk + kv_n, state

    _, r_prev = lax.scan(step, jnp.zeros((B, H, dh, dh), jnp.float32), jnp.moveaxis(kv, 2, 0))
    r_prev = jnp.moveaxis(r_prev, 0, 2)
    xi = jnp.exp(log_gamma[:, None] * (idx + 1.0))
    cross = jnp.einsum('bhnid,bhnde->bhnie', qc, r_prev) * xi[None, :, None, :, None]
    return (inner + cross).reshape(B, H, S, dh)


def _heads(t, n_heads, head_dim):
    B, S, _ = t.shape
    return t.reshape(B, S, n_heads, head_dim).transpose(0, 2, 1, 3)


def _merge(t):
    B, H, S, hd = t.shape
    return t.transpose(0, 2, 1, 3).reshape(B, S, H * hd)


def _hybrid_mixer(h, w_in, w_out):
    proj = jnp.einsum('bsd,de->bse', h, w_in)
    cuts = np.cumsum([ATTN_WIDTH] * 3 + [RET_WIDTH] * 3)
    qa, ka, va, qr, kr, vr, gr = jnp.split(proj, cuts, axis=-1)
    attn = _dilated_attention(_heads(qa, N_ATTN_HEADS, ATTN_HEAD_DIM),
                              _heads(ka, N_ATTN_HEADS, ATTN_HEAD_DIM),
                              _heads(va, N_ATTN_HEADS, ATTN_HEAD_DIM))
    f32 = jnp.float32
    ret = _retention_chunkwise(_heads(qr, N_RET_HEADS, RET_HEAD_DIM).astype(f32),
                               _heads(kr, N_RET_HEADS, RET_HEAD_DIM).astype(f32),
                               _heads(vr, N_RET_HEADS, RET_HEAD_DIM).astype(f32))
    ret = ret * lax.rsqrt(jnp.mean(ret * ret, axis=-1, keepdims=True) + NORM_EPS)
    ret = jax.nn.silu(gr.astype(f32)) * _merge(ret)
    mixed = jnp.concatenate([_merge(attn), ret], axis=-1).astype(h.dtype)
    return jnp.einsum('bse,ed->bsd', mixed, w_out)


def _swiglu(h, w_gate, w_up, w_down):
    g = jnp.einsum('bsd,df->bsf', h, w_gate)
    u = jnp.einsum('bsd,df->bsf', h, w_up)
    return jnp.einsum('bsf,fd->bsd', jax.nn.silu(g) * u, w_down)


def setup_inputs(seed: int = 0) -> dict:
    key = jax.random.key(seed)
    ks = jax.random.split(key, 10)
    f32 = jnp.float32

    def normal(k, shape, fan_in):
        return jax.random.normal(k, shape, f32) * (fan_in ** -0.5)

    return {
        "x": jax.random.normal(ks[0], (BATCH, SEQ, D_MODEL), f32),
        "norm_mix_w": 1.0 + 0.02 * jax.random.normal(ks[1], (DEPTH, D_MODEL), f32),
        "w_in": normal(ks[2], (DEPTH, D_MODEL, IN_PROJ_WIDTH), D_MODEL),
        "w_out": normal(ks[3], (DEPTH, MIX_WIDTH, D_MODEL), MIX_WIDTH),
        "norm_ffn_w": 1.0 + 0.02 * jax.random.normal(ks[4], (DEPTH, D_MODEL), f32),
        "w_gate": normal(ks[5], (DEPTH, D_MODEL, FFN_HIDDEN), D_MODEL),
        "w_up": normal(ks[6], (DEPTH, D_MODEL, FFN_HIDDEN), D_MODEL),
        "w_down": normal(ks[7], (DEPTH, FFN_HIDDEN, D_MODEL), FFN_HIDDEN),
        "norm_final_w": 1.0 + 0.02 * jax.random.normal(ks[8], (D_MODEL,), f32),
    }


def reference(x, norm_mix_w, w_in, w_out, norm_ffn_w, w_gate, w_up, w_down, norm_final_w):
    for layer in range(DEPTH):
        h = _rmsnorm(x, norm_mix_w[layer])
        x = x + _hybrid_mixer(h, w_in[layer], w_out[layer])
        h = _rmsnorm(x, norm_ffn_w[layer])
        x = x + _swiglu(h, w_gate[layer], w_up[layer], w_down[layer])
    return _rmsnorm(x, norm_final_w)
```

```python
import functools
import math

import jax
import jax.numpy as jnp
from jax import lax
from jax.experimental import pallas as pl
from jax.experimental.pallas import tpu as pltpu

F32 = jnp.float32
BF16 = jnp.bfloat16

LANES = 128
ATTN_HEAD_DIM = 128
N_ATTN_HEADS = 8
RET_HEAD_DIM = 256
N_RET_HEADS = 4
RET_CHUNK = 128
ATTN_BLK = 128
DILATIONS = (1, 4, 16)
NORM_EPS = 1e-6
MASK_VALUE = -1e30
NORM_ROWS = 128
VMEM_LIMIT = 48 * 1024 * 1024


def _rms_rows(x, w):
    ms = jnp.mean(x * x, axis=-1, keepdims=True)
    return x * lax.rsqrt(ms + NORM_EPS) * w


def _norm_into(x_ref, w_ref, h_ref, rows):
    def body(i, c):
        r = pl.multiple_of(i * NORM_ROWS, NORM_ROWS)
        h_ref[pl.ds(r, NORM_ROWS), :] = _rms_rows(
            x_ref[pl.ds(r, NORM_ROWS), :], w_ref[...]).astype(h_ref.dtype)
        return c
    lax.fori_loop(0, rows // NORM_ROWS, body, 0)


def _inproj_kernel(x_ref, nw_ref, w_ref, o_ref, h_ref, *, tm, tn):
    @pl.when(pl.program_id(1) == 0)
    def _():
        _norm_into(x_ref, nw_ref, h_ref, tm)

    acc = jnp.dot(h_ref[...], w_ref[...], preferred_element_type=F32)
    for c in range(tn // LANES):
        o_ref[c] = acc[:, c * LANES:(c + 1) * LANES].astype(o_ref.dtype)


def _in_proj(x2d, norm_w, w_bf16, *, tm=1024, tn=512):
    m, d = x2d.shape
    n = w_bf16.shape[1]
    return pl.pallas_call(
        functools.partial(_inproj_kernel, tm=tm, tn=tn),
        grid=(m // tm, n // tn),
        in_specs=[
            pl.BlockSpec((tm, d), lambda i, j: (i, 0)),
            pl.BlockSpec((1, d), lambda i, j: (0, 0)),
            pl.BlockSpec((d, tn), lambda i, j: (0, j)),
        ],
        out_specs=pl.BlockSpec((tn // LANES, tm, LANES), lambda i, j: (j, i, 0)),
        out_shape=jax.ShapeDtypeStruct((n // LANES, m, LANES), BF16),
        scratch_shapes=[pltpu.VMEM((tm, d), BF16)],
        compiler_params=pltpu.CompilerParams(
            dimension_semantics=("parallel", "arbitrary"),
            vmem_limit_bytes=VMEM_LIMIT),
        name="in_proj",
    )(x2d, norm_w.reshape(1, d), w_bf16)


def _rows(ref, start, n, stride):
    if stride == 1:
        return ref[pl.ds(start, n), :]
    return ref[pl.ds(start, n, stride=stride), :]


def _attn_kernel(slopes_ref, q_ref, k_ref, v_ref, o_ref, qf, kf, vf, bm, oacc, lacc, *, seq):
    blk = ATTN_BLK
    slope = slopes_ref[pl.program_id(1)]
    scale = 1.0 / math.sqrt(ATTN_HEAD_DIM)

    qf[...] = q_ref[...].astype(F32)
    kf[...] = k_ref[...].astype(F32)
    vf[...] = v_ref[...].astype(F32)

    qi = lax.broadcasted_iota(jnp.int32, (blk, 2 * blk), 0)
    kj = lax.broadcasted_iota(jnp.int32, (blk, 2 * blk), 1)
    diff = qi - kj + blk
    valid = (diff >= 0) & (diff <= blk)
    dist = diff.astype(F32)
    for pi, dil in enumerate(DILATIONS):
        bm[pi] = jnp.where(valid, -slope * (dist * float(dil)), MASK_VALUE)

    def block(pi, stride, q0, k0, nkeys):
        qb = _rows(qf, q0, blk, stride).astype(BF16)
        kk = _rows(kf, k0, nkeys, stride).astype(BF16)
        vv = _rows(vf, k0, nkeys, stride).astype(BF16)
        bias = bm[pi] if nkeys == 2 * blk else bm[pi, :, blk:]
        s = lax.dot_general(qb, kk, (((1,), (1,)), ((), ())),
                            preferred_element_type=F32) * scale + bias
        mx = jnp.max(s, axis=-1, keepdims=True)
        p = jnp.exp(s - mx)
        l = jnp.sum(p, axis=-1, keepdims=True)
        acc = jnp.dot(p.astype(BF16), vv, preferred_element_type=F32)
        o = acc / l
        lse = jnp.broadcast_to(mx + jnp.log(l), (blk, LANES))
        if stride == 1:
            oacc[pi, pl.ds(q0, blk), :] = o
            lacc[pi, pl.ds(q0, blk), :] = lse
        else:
            oacc[pi, pl.ds(q0, blk, stride=stride), :] = o
            lacc[pi, pl.ds(q0, blk, stride=stride), :] = lse

    block(0, 1, 0, 0, blk)

    def p0_body(n, c):
        q0 = pl.multiple_of(n * blk, blk)
        block(0, 1, q0, q0 - blk, 2 * blk)
        return c
    lax.fori_loop(1, seq // blk, p0_body, 0)

    def p1_body(r, c):
        block(1, 4, r, r, blk)

        def inner(n, c2):
            block(1, 4, r + 4 * blk * n, r + 4 * blk * (n - 1), 2 * blk)
            return c2
        lax.fori_loop(1, seq // (4 * blk), inner, 0)
        return c
    lax.fori_loop(0, 4, p1_body, 0)

    def p2_body(r, c):
        block(2, 16, r, r, blk)
        return c
    lax.fori_loop(0, 16, p2_body, 0)

    def comb(i, c):
        r = pl.multiple_of(i * blk, blk)
        ls = [lacc[pi, pl.ds(r, blk), :] for pi in range(3)]
        mx = jnp.maximum(jnp.maximum(ls[0], ls[1]), ls[2])
        ws = [jnp.exp(l - mx) for l in ls]
        den = ws[0] + ws[1] + ws[2]
        num = (ws[0] * oacc[0, pl.ds(r, blk), :] + ws[1] * oacc[1, pl.ds(r, blk), :]
               + ws[2] * oacc[2, pl.ds(r, blk), :])
        o_ref[pl.ds(r, blk), :] = (num / den).astype(o_ref.dtype)
        return c
    lax.fori_loop(0, seq // blk, comb, 0)


def _attention(proj_hm, slopes, *, batch, seq):
    h = N_ATTN_HEADS
    blk3 = (None, seq, LANES)
    return pl.pallas_call(
        functools.partial(_attn_kernel, seq=seq),
        grid=(batch, h),
        in_specs=[
            pl.BlockSpec(memory_space=pltpu.SMEM),
            pl.BlockSpec(blk3, lambda b, hh: (hh, b, 0)),
            pl.BlockSpec(blk3, lambda b, hh: (h + hh, b, 0)),
            pl.BlockSpec(blk3, lambda b, hh: (2 * h + hh, b, 0)),
        ],
        out_specs=pl.BlockSpec(blk3, lambda b, hh: (hh, b, 0)),
        out_shape=jax.ShapeDtypeStruct((h, batch * seq, LANES), BF16),
        scratch_shapes=[
            pltpu.VMEM((seq, LANES), F32),
            pltpu.VMEM((seq, LANES), F32),
            pltpu.VMEM((seq, LANES), F32),
            pltpu.VMEM((3, ATTN_BLK, 2 * ATTN_BLK), F32),
            pltpu.VMEM((3, seq, LANES), F32),
            pltpu.VMEM((3, seq, LANES), F32),
        ],
        compiler_params=pltpu.CompilerParams(
            dimension_semantics=("parallel", "parallel"),
            vmem_limit_bytes=VMEM_LIMIT),
        name="dilated_attention",
    )(slopes, proj_hm, proj_hm, proj_hm)


def _ret_kernel(lg_ref, q_ref, k_ref, v_ref, g_ref, o_ref, decay, zeta, xi, state, *, seq):
    c = RET_CHUNK
    dh = RET_HEAD_DIM
    lg = lg_ref[pl.program_id(1)]

    ii = lax.broadcasted_iota(jnp.int32, (c, c), 0)
    jj = lax.broadcasted_iota(jnp.int32, (c, c), 1)
    dif = (ii - jj).astype(F32)
    decay[...] = jnp.where(dif >= 0, jnp.exp(lg * jnp.maximum(dif, 0.0)), 0.0)
    idx = lax.broadcasted_iota(jnp.int32, (c, dh), 0).astype(F32)
    zeta[...] = jnp.exp(lg * (c - 1.0 - idx))
    xi[...] = jnp.exp(lg * (idx + 1.0))
    gamma_chunk = jnp.exp(jnp.full((dh, dh), lg * c, F32))
    state[...] = jnp.zeros_like(state)

    def wide(ref, r):
        return jnp.concatenate([ref[0, pl.ds(r, c), :], ref[1, pl.ds(r, c), :]], axis=-1)

    def body(n, carry):
        r = pl.multiple_of(n * c, c)
        qn = wide(q_ref, r)
        kn = wide(k_ref, r).astype(F32) * (1.0 / math.sqrt(dh))
        vn = wide(v_ref, r)
        kb = kn.astype(BF16)
        scores = lax.dot_general(qn, kb, (((1,), (1,)), ((), ())),
                                 preferred_element_type=F32) * decay[...]
        inner = jnp.dot(scores.astype(BF16), vn, preferred_element_type=F32)
        st = state[...]
        cross = jnp.dot(qn, st.astype(BF16), preferred_element_type=F32) * xi[...]
        kz_t = (kn * zeta[...]).T.astype(BF16)
        kv = jnp.dot(kz_t, vn, preferred_element_type=F32)
        state[...] = st * gamma_chunk + kv
        ret = inner + cross
        ret = ret * lax.rsqrt(jnp.mean(ret * ret, axis=-1, keepdims=True) + NORM_EPS)
        gate = wide(g_ref, r).astype(F32)
        out = (gate * jax.nn.sigmoid(gate) * ret).astype(o_ref.dtype)
        o_ref[0, pl.ds(r, c), :] = out[:, :LANES]
        o_ref[1, pl.ds(r, c), :] = out[:, LANES:]
        return carry
    lax.fori_loop(0, seq // c, body, 0)


def _retention(proj_hm, log_gamma, *, batch, seq):
    h = N_RET_HEADS
    base = 3 * N_ATTN_HEADS * ATTN_HEAD_DIM // (2 * LANES)
    blk = (2, seq, LANES)
    return pl.pallas_call(
        functools.partial(_ret_kernel, seq=seq),
        grid=(batch, h),
        in_specs=[
            pl.BlockSpec(memory_space=pltpu.SMEM),
            pl.BlockSpec(blk, lambda b, hh: (base + hh, b, 0)),
            pl.BlockSpec(blk, lambda b, hh: (base + h + hh, b, 0)),
            pl.BlockSpec(blk, lambda b, hh: (base + 2 * h + hh, b, 0)),
            pl.BlockSpec(blk, lambda b, hh: (base + 3 * h + hh, b, 0)),
        ],
        out_specs=pl.BlockSpec(blk, lambda b, hh: (hh, b, 0)),
        out_shape=jax.ShapeDtypeStruct((2 * h, batch * seq, LANES), BF16),
        scratch_shapes=[
            pltpu.VMEM((RET_CHUNK, RET_CHUNK), F32),
            pltpu.VMEM((RET_CHUNK, RET_HEAD_DIM), F32),
            pltpu.VMEM((RET_CHUNK, RET_HEAD_DIM), F32),
            pltpu.VMEM((RET_HEAD_DIM, RET_HEAD_DIM), F32),
        ],
        compiler_params=pltpu.CompilerParams(
            dimension_semantics=("parallel", "parallel"),
            vmem_limit_bytes=VMEM_LIMIT),
        name="retention",
    )(log_gamma, proj_hm, proj_hm, proj_hm, proj_hm)


def _outproj_kernel(x_ref, a_ref, r_ref, wa_ref, wr_ref, o_ref):
    na = a_ref.shape[0]
    nr = r_ref.shape[0]
    a = jnp.concatenate([a_ref[i] for i in range(na)], axis=-1)
    r = jnp.concatenate([r_ref[i] for i in range(nr)], axis=-1)
    y = jnp.dot(a, wa_ref[...], preferred_element_type=F32)
    y = y + jnp.dot(r, wr_ref[...], preferred_element_type=F32)
    o_ref[...] = x_ref[...] + y


def _out_proj(x2d, attn_hm, ret_hm, w_out_bf16, *, tm=512):
    m, d = x2d.shape
    na, nr = attn_hm.shape[0], ret_hm.shape[0]
    wa = w_out_bf16[:na * LANES]
    wr = w_out_bf16[na * LANES:]
    return pl.pallas_call(
        _outproj_kernel,
        grid=(m // tm,),
        in_specs=[
            pl.BlockSpec((tm, d), lambda i: (i, 0)),
            pl.BlockSpec((na, tm, LANES), lambda i: (0, i, 0)),
            pl.BlockSpec((nr, tm, LANES), lambda i: (0, i, 0)),
            pl.BlockSpec((na * LANES, d), lambda i: (0, 0)),
            pl.BlockSpec((nr * LANES, d), lambda i: (0, 0)),
        ],
        out_specs=pl.BlockSpec((tm, d), lambda i: (i, 0)),
        out_shape=jax.ShapeDtypeStruct((m, d), F32),
        compiler_params=pltpu.CompilerParams(
            dimension_semantics=("parallel",),
            vmem_limit_bytes=VMEM_LIMIT),
        name="out_proj",
    )(x2d, attn_hm, ret_hm, wa, wr)


def _ffn_kernel(x_ref, nw_ref, wg_ref, wu_ref, wd_ref, fw_ref, o_ref, h_ref, acc_ref,
                *, tm, final_norm):
    f = pl.program_id(1)

    @pl.when(f == 0)
    def _():
        _norm_into(x_ref, nw_ref, h_ref, tm)

    h = h_ref[...]
    g = jnp.dot(h, wg_ref[...], preferred_element_type=F32)
    u = jnp.dot(h, wu_ref[...], preferred_element_type=F32)
    a = (g * jax.nn.sigmoid(g) * u).astype(BF16)
    y = jnp.dot(a, wd_ref[...], preferred_element_type=F32)

    @pl.when(f == 0)
    def _():
        acc_ref[...] = y

    @pl.when(f > 0)
    def _():
        acc_ref[...] += y

    @pl.when(f == pl.num_programs(1) - 1)
    def _():
        def body(i, c):
            r = pl.multiple_of(i * NORM_ROWS, NORM_ROWS)
            x2 = x_ref[pl.ds(r, NORM_ROWS), :] + acc_ref[pl.ds(r, NORM_ROWS), :]
            if final_norm:
                x2 = _rms_rows(x2, fw_ref[...])
            o_ref[pl.ds(r, NORM_ROWS), :] = x2
            return c
        lax.fori_loop(0, tm // NORM_ROWS, body, 0)


def _ffn(x2d, norm_w, wg, wu, wd, final_w, *, final_norm, tm=512, tf=512):
    m, d = x2d.shape
    hid = wg.shape[1]
    return pl.pallas_call(
        functools.partial(_ffn_kernel, tm=tm, final_norm=final_norm),
        grid=(m // tm, hid // tf),
        in_specs=[
            pl.BlockSpec((tm, d), lambda i, f: (i, 0)),
            pl.BlockSpec((1, d), lambda i, f: (0, 0)),
            pl.BlockSpec((d, tf), lambda i, f: (0, f)),
            pl.BlockSpec((d, tf), lambda i, f: (0, f)),
            pl.BlockSpec((tf, d), lambda i, f: (f, 0)),
            pl.BlockSpec((1, d), lambda i, f: (0, 0)),
        ],
        out_specs=pl.BlockSpec((tm, d), lambda i, f: (i, 0)),
        out_shape=jax.ShapeDtypeStruct((m, d), F32),
        scratch_shapes=[pltpu.VMEM((tm, d), BF16), pltpu.VMEM((tm, d), F32)],
        compiler_params=pltpu.CompilerParams(
            dimension_semantics=("parallel", "arbitrary"),
            vmem_limit_bytes=VMEM_LIMIT),
        name="ffn",
    )(x2d, norm_w.reshape(1, d), wg, wu, wd, final_w.reshape(1, d))


def kernel(x, norm_mix_w, w_in, w_out, norm_ffn_w, w_gate, w_up, w_down, norm_final_w):
    batch, seq, d = x.shape
    depth = w_in.shape[0]
    assert seq == DILATIONS[-1] * ATTN_BLK and seq % RET_CHUNK == 0
    slopes = jnp.exp2(-8.0 * jnp.arange(1, N_ATTN_HEADS + 1, dtype=F32) / N_ATTN_HEADS)
    log_gamma = jnp.log(1.0 - jnp.exp2(-5.0 - jnp.arange(N_RET_HEADS, dtype=F32)))

    xs = x.reshape(batch * seq, d)
    for layer in range(depth):
        proj = _in_proj(xs, norm_mix_w[layer], w_in[layer].astype(BF16))
        attn = _attention(proj, slopes, batch=batch, seq=seq)
        ret = _retention(proj, log_gamma, batch=batch, seq=seq)
        xs = _out_proj(xs, attn, ret, w_out[layer].astype(BF16))
        xs = _ffn(xs, norm_ffn_w[layer], w_gate[layer].astype(BF16),
                  w_up[layer].astype(BF16), w_down[layer].astype(BF16), norm_final_w,
                  final_norm=(layer == depth - 1))
    return xs.reshape(batch, seq, d)
```

```python
import functools
import math

import jax
import jax.numpy as jnp
from jax import lax
from jax.experimental import pallas as pl
from jax.experimental.pallas import tpu as pltpu

F32 = jnp.float32
BF16 = jnp.bfloat16

LANES = 128
ATTN_HEAD_DIM = 128
N_ATTN_HEADS = 8
RET_HEAD_DIM = 256
N_RET_HEADS = 4
RET_CHUNK = 128
ATTN_BLK = 128
DILATIONS = (1, 4, 16)
ATTN_LOOKAHEAD = 6
NORM_EPS = 1e-6
MASK_VALUE = -1e30
LOG2E = math.log2(math.e)
NORM_ROWS = 128
VMEM_LIMIT = 48 * 1024 * 1024


def _rms_rows(x, w):
    ms = jnp.mean(x * x, axis=-1, keepdims=True)
    return x * lax.rsqrt(ms + NORM_EPS) * w


def _norm_into(x_ref, w_ref, h_ref, rows):
    def body(i, c):
        r = pl.multiple_of(i * NORM_ROWS, NORM_ROWS)
        h_ref[pl.ds(r, NORM_ROWS), :] = _rms_rows(
            x_ref[pl.ds(r, NORM_ROWS), :], w_ref[...]).astype(h_ref.dtype)
        return c
    lax.fori_loop(0, rows // NORM_ROWS, body, 0)


def _inproj_kernel(x_ref, nw_ref, w_ref, o_ref, h_ref, *, tm, tn):
    @pl.when(pl.program_id(1) == 0)
    def _():
        _norm_into(x_ref, nw_ref, h_ref, tm)

    acc = jnp.dot(h_ref[...], w_ref[...], preferred_element_type=F32)
    for c in range(tn // LANES):
        o_ref[c] = acc[:, c * LANES:(c + 1) * LANES].astype(o_ref.dtype)


def _in_proj(x2d, norm_w, w_bf16, *, tm=1024, tn=512):
    m, d = x2d.shape
    n = w_bf16.shape[1]
    return pl.pallas_call(
        functools.partial(_inproj_kernel, tm=tm, tn=tn),
        grid=(m // tm, n // tn),
        in_specs=[
            pl.BlockSpec((tm, d), lambda i, j: (i, 0)),
            pl.BlockSpec((1, d), lambda i, j: (0, 0)),
            pl.BlockSpec((d, tn), lambda i, j: (0, j)),
        ],
        out_specs=pl.BlockSpec((tn // LANES, tm, LANES), lambda i, j: (j, i, 0)),
        out_shape=jax.ShapeDtypeStruct((n // LANES, m, LANES), BF16),
        scratch_shapes=[pltpu.VMEM((tm, d), BF16)],
        compiler_params=pltpu.CompilerParams(
            dimension_semantics=("parallel", "arbitrary"),
            vmem_limit_bytes=VMEM_LIMIT),
        name="in_proj",
    )(x2d, norm_w.reshape(1, d), w_bf16)


def _attn_kernel(slopes_ref, q1, k1, v1, q4, k4, v4, q16, k16, v16, o_ref, bm, oacc, lacc,
                 *, seq):
    blk = ATTN_BLK
    nblk = seq // blk
    per4 = nblk // 4
    slope = slopes_ref[pl.program_id(1)]
    qk_scale = LOG2E / math.sqrt(ATTN_HEAD_DIM)

    qi = lax.broadcasted_iota(jnp.int32, (blk, 2 * blk), 0)
    kj = lax.broadcasted_iota(jnp.int32, (blk, 2 * blk), 1)
    diff = qi - kj + blk
    valid = (diff >= 0) & (diff <= blk)
    dist = diff.astype(F32)
    for pi, dil in enumerate(DILATIONS):
        bm[pi] = jnp.where(valid, (-LOG2E * slope) * (dist * float(dil)), MASK_VALUE)

    def scores(pi, q_at, k_at):
        kk = k_at()
        bias = bm[pi] if kk.shape[0] == 2 * blk else bm[pi, :, blk:]
        return lax.dot_general(q_at(), kk, (((1,), (1,)), ((), ())),
                               preferred_element_type=F32) * qk_scale + bias

    def finish(s, pi, v_at, dst):
        mx = jnp.max(s, axis=-1, keepdims=True)
        p = jnp.exp2(s - mx)
        l = jnp.sum(p, axis=-1, keepdims=True)
        acc = jnp.dot(p.astype(BF16), v_at(), preferred_element_type=F32)
        oacc[pi, dst, :] = acc * (1.0 / l)
        lacc[pi, dst, :] = jnp.broadcast_to(mx + jnp.log2(l), (blk, LANES))

    work = []
    for i in range(nblk):
        rows = slice(i * blk, (i + 1) * blk)
        kr = slice(max(i - 1, 0) * blk, (i + 1) * blk)
        work.append((0, lambda rows=rows: q1[rows, :], lambda kr=kr: k1[kr, :],
                     lambda kr=kr: v1[kr, :], rows))
        r4, n = divmod(i, per4)
        ln = slice(r4 * LANES, (r4 + 1) * LANES)
        qr = slice(n * blk, (n + 1) * blk)
        kr = slice(max(n - 1, 0) * blk, (n + 1) * blk)
        work.append((1, lambda qr=qr, ln=ln: q4[qr, ln], lambda kr=kr, ln=ln: k4[kr, ln],
                     lambda kr=kr, ln=ln: v4[kr, ln], rows))
        ln = slice(i * LANES, (i + 1) * LANES)
        dst = pl.ds((i % 4) * (seq // 4) + i // 4, blk, stride=4)
        work.append((2, lambda ln=ln: q16[:, ln], lambda ln=ln: k16[:, ln],
                     lambda ln=ln: v16[:, ln], dst))

    pending = {}
    for t in range(len(work) + ATTN_LOOKAHEAD):
        if t < len(work):
            pi, q_at, k_at, _, _ = work[t]
            pending[t] = scores(pi, q_at, k_at)
        if t >= ATTN_LOOKAHEAD:
            pi, _, _, v_at, dst = work[t - ATTN_LOOKAHEAD]
            finish(pending.pop(t - ATTN_LOOKAHEAD), pi, v_at, dst)

    for c in range(nblk):
        r4, n = divmod(c, per4)
        rows = slice(c * blk, (c + 1) * blk)
        nat = pl.ds(r4 + 4 * blk * n, blk, stride=4)
        ls = [lacc[0, nat, :], lacc[1, rows, :], lacc[2, rows, :]]
        mx = jnp.maximum(jnp.maximum(ls[0], ls[1]), ls[2])
        ws = [jnp.exp2(l - mx) for l in ls]
        num = ws[0] * oacc[0, nat, :] + ws[1] * oacc[1, rows, :] + ws[2] * oacc[2, rows, :]
        out = num * (1.0 / (ws[0] + ws[1] + ws[2]))
        o_ref[n * blk:(n + 1) * blk, r4 * LANES:(r4 + 1) * LANES] = out.astype(o_ref.dtype)


def _attention(proj_hm, slopes, *, batch, seq):
    h = N_ATTN_HEADS
    nslab = proj_hm.shape[0]
    views = [proj_hm.reshape(nslab, batch, seq // d, d * LANES) for d in DILATIONS]
    specs = []
    for d in DILATIONS:
        for t in range(3):
            specs.append(pl.BlockSpec((None, None, seq // d, d * LANES),
                                      lambda b, hh, t=t: (t * h + hh, b, 0, 0)))
    out = pl.pallas_call(
        functools.partial(_attn_kernel, seq=seq),
        grid=(batch, h),
        in_specs=[pl.BlockSpec(memory_space=pltpu.SMEM)] + specs,
        out_specs=pl.BlockSpec((None, None, seq // 4, 4 * LANES), lambda b, hh: (hh, b, 0, 0)),
        out_shape=jax.ShapeDtypeStruct((h, batch, seq // 4, 4 * LANES), BF16),
        scratch_shapes=[
            pltpu.VMEM((3, ATTN_BLK, 2 * ATTN_BLK), F32),
            pltpu.VMEM((3, seq, LANES), F32),
            pltpu.VMEM((3, seq, LANES), F32),
        ],
        compiler_params=pltpu.CompilerParams(
            dimension_semantics=("parallel", "parallel"),
            vmem_limit_bytes=VMEM_LIMIT),
        name="dilated_attention",
    )(slopes, *[v for v in views for _ in range(3)])
    return out.reshape(h, batch * seq, LANES)


def _ret_kernel(lg_ref, q_ref, k_ref, v_ref, g_ref, o_ref, decay, zeta, xi, state, *, seq):
    c = RET_CHUNK
    dh = RET_HEAD_DIM
    lg = lg_ref[pl.program_id(1)]

    ii = lax.broadcasted_iota(jnp.int32, (c, c), 0)
    jj = lax.broadcasted_iota(jnp.int32, (c, c), 1)
    dif = (ii - jj).astype(F32)
    decay[...] = jnp.where(dif >= 0, jnp.exp(lg * jnp.maximum(dif, 0.0)), 0.0)
    idx = lax.broadcasted_iota(jnp.int32, (c, dh), 0).astype(F32)
    zeta[...] = jnp.exp(lg * (c - 1.0 - idx))
    xi[...] = jnp.exp(lg * (idx + 1.0))
    gamma_chunk = jnp.exp(jnp.full((dh, dh), lg * c, F32))
    state[...] = jnp.zeros_like(state)

    def wide(ref, r):
        return jnp.concatenate([ref[0, pl.ds(r, c), :], ref[1, pl.ds(r, c), :]], axis=-1)

    def body(n, carry):
        r = pl.multiple_of(n * c, c)
        qn = wide(q_ref, r)
        kn = wide(k_ref, r).astype(F32) * (1.0 / math.sqrt(dh))
        vn = wide(v_ref, r)
        kb = kn.astype(BF16)
        scores = lax.dot_general(qn, kb, (((1,), (1,)), ((), ())),
                                 preferred_element_type=F32) * decay[...]
        inner = jnp.dot(scores.astype(BF16), vn, preferred_element_type=F32)
        st = state[...]
        cross = jnp.dot(qn, st.astype(BF16), preferred_element_type=F32) * xi[...]
        kz_t = (kn * zeta[...]).T.astype(BF16)
        kv = jnp.dot(kz_t, vn, preferred_element_type=F32)
        state[...] = st * gamma_chunk + kv
        ret = inner + cross
        ret = ret * lax.rsqrt(jnp.mean(ret * ret, axis=-1, keepdims=True) + NORM_EPS)
        gate = wide(g_ref, r).astype(F32)
        out = (gate * jax.nn.sigmoid(gate) * ret).astype(o_ref.dtype)
        o_ref[0, pl.ds(r, c), :] = out[:, :LANES]
        o_ref[1, pl.ds(r, c), :] = out[:, LANES:]
        return carry
    lax.fori_loop(0, seq // c, body, 0)


def _retention(proj_hm, log_gamma, *, batch, seq):
    h = N_RET_HEADS
    base = 3 * N_ATTN_HEADS * ATTN_HEAD_DIM // (2 * LANES)
    blk = (2, seq, LANES)
    return pl.pallas_call(
        functools.partial(_ret_kernel, seq=seq),
        grid=(batch, h),
        in_specs=[
            pl.BlockSpec(memory_space=pltpu.SMEM),
            pl.BlockSpec(blk, lambda b, hh: (base + hh, b, 0)),
            pl.BlockSpec(blk, lambda b, hh: (base + h + hh, b, 0)),
            pl.BlockSpec(blk, lambda b, hh: (base + 2 * h + hh, b, 0)),
            pl.BlockSpec(blk, lambda b, hh: (base + 3 * h + hh, b, 0)),
        ],
        out_specs=pl.BlockSpec(blk, lambda b, hh: (hh, b, 0)),
        out_shape=jax.ShapeDtypeStruct((2 * h, batch * seq, LANES), BF16),
        scratch_shapes=[
            pltpu.VMEM((RET_CHUNK, RET_CHUNK), F32),
            pltpu.VMEM((RET_CHUNK, RET_HEAD_DIM), F32),
            pltpu.VMEM((RET_CHUNK, RET_HEAD_DIM), F32),
            pltpu.VMEM((RET_HEAD_DIM, RET_HEAD_DIM), F32),
        ],
        compiler_params=pltpu.CompilerParams(
            dimension_semantics=("parallel", "parallel"),
            vmem_limit_bytes=VMEM_LIMIT),
        name="retention",
    )(log_gamma, proj_hm, proj_hm, proj_hm, proj_hm)


def _outproj_kernel(x_ref, a_ref, r_ref, wa_ref, wr_ref, o_ref):
    na = a_ref.shape[0]
    nr = r_ref.shape[0]
    a = jnp.concatenate([a_ref[i] for i in range(na)], axis=-1)
    r = jnp.concatenate([r_ref[i] for i in range(nr)], axis=-1)
    y = jnp.dot(a, wa_ref[...], preferred_element_type=F32)
    y = y + jnp.dot(r, wr_ref[...], preferred_element_type=F32)
    o_ref[...] = x_ref[...] + y


def _out_proj(x2d, attn_hm, ret_hm, w_out_bf16, *, tm=512):
    m, d = x2d.shape
    na, nr = attn_hm.shape[0], ret_hm.shape[0]
    wa = w_out_bf16[:na * LANES]
    wr = w_out_bf16[na * LANES:]
    return pl.pallas_call(
        _outproj_kernel,
        grid=(m // tm,),
        in_specs=[
            pl.BlockSpec((tm, d), lambda i: (i, 0)),
            pl.BlockSpec((na, tm, LANES), lambda i: (0, i, 0)),
            pl.BlockSpec((nr, tm, LANES), lambda i: (0, i, 0)),
            pl.BlockSpec((na * LANES, d), lambda i: (0, 0)),
            pl.BlockSpec((nr * LANES, d), lambda i: (0, 0)),
        ],
        out_specs=pl.BlockSpec((tm, d), lambda i: (i, 0)),
        out_shape=jax.ShapeDtypeStruct((m, d), F32),
        compiler_params=pltpu.CompilerParams(
            dimension_semantics=("parallel",),
            vmem_limit_bytes=VMEM_LIMIT),
        name="out_proj",
    )(x2d, attn_hm, ret_hm, wa, wr)


def _ffn_kernel(x_ref, nw_ref, wg_ref, wu_ref, wd_ref, fw_ref, o_ref, h_ref, acc_ref,
                *, tm, final_norm):
    f = pl.program_id(1)

    @pl.when(f == 0)
    def _():
        _norm_into(x_ref, nw_ref, h_ref, tm)

    h = h_ref[...]
    g = jnp.dot(h, wg_ref[...], preferred_element_type=F32)
    u = jnp.dot(h, wu_ref[...], preferred_element_type=F32)
    a = (g * jax.nn.sigmoid(g) * u).astype(BF16)
    y = jnp.dot(a, wd_ref[...], preferred_element_type=F32)

    @pl.when(f == 0)
    def _():
        acc_ref[...] = y

    @pl.when(f > 0)
    def _():
        acc_ref[...] += y

    @pl.when(f == pl.num_programs(1) - 1)
    def _():
        def body(i, c):
            r = pl.multiple_of(i * NORM_ROWS, NORM_ROWS)
            x2 = x_ref[pl.ds(r, NORM_ROWS), :] + acc_ref[pl.ds(r, NORM_ROWS), :]
            if final_norm:
                x2 = _rms_rows(x2, fw_ref[...])
            o_ref[pl.ds(r, NORM_ROWS), :] = x2
            return c
        lax.fori_loop(0, tm // NORM_ROWS, body, 0)


def _ffn(x2d, norm_w, wg, wu, wd, final_w, *, final_norm, tm=512, tf=512):
    m, d = x2d.shape
    hid = wg.shape[1]
    return pl.pallas_call(
        functools.partial(_ffn_kernel, tm=tm, final_norm=final_norm),
        grid=(m // tm, hid // tf),
        in_specs=[
            pl.BlockSpec((tm, d), lambda i, f: (i, 0)),
            pl.BlockSpec((1, d), lambda i, f: (0, 0)),
            pl.BlockSpec((d, tf), lambda i, f: (0, f)),
            pl.BlockSpec((d, tf), lambda i, f: (0, f)),
            pl.BlockSpec((tf, d), lambda i, f: (f, 0)),
            pl.BlockSpec((1, d), lambda i, f: (0, 0)),
        ],
        out_specs=pl.BlockSpec((tm, d), lambda i, f: (i, 0)),
        out_shape=jax.ShapeDtypeStruct((m, d), F32),
        scratch_shapes=[pltpu.VMEM((tm, d), BF16), pltpu.VMEM((tm, d), F32)],
        compiler_params=pltpu.CompilerParams(
            dimension_semantics=("parallel", "arbitrary"),
            vmem_limit_bytes=VMEM_LIMIT),
        name="ffn",
    )(x2d, norm_w.reshape(1, d), wg, wu, wd, final_w.reshape(1, d))


def kernel(x, norm_mix_w, w_in, w_out, norm_ffn_w, w_gate, w_up, w_down, norm_final_w):
    batch, seq, d = x.shape
    depth = w_in.shape[0]
    assert seq == DILATIONS[-1] * ATTN_BLK and seq % RET_CHUNK == 0
    slopes = jnp.exp2(-8.0 * jnp.arange(1, N_ATTN_HEADS + 1, dtype=F32) / N_ATTN_HEADS)
    log_gamma = jnp.log(1.0 - jnp.exp2(-5.0 - jnp.arange(N_RET_HEADS, dtype=F32)))

    xs = x.reshape(batch * seq, d)
    for layer in range(depth):
        proj = _in_proj(xs, norm_mix_w[layer], w_in[layer].astype(BF16))
        attn = _attention(proj, slopes, batch=batch, seq=seq)
        ret = _retention(proj, log_gamma, batch=batch, seq=seq)
        xs = _out_proj(xs, attn, ret, w_out[layer].astype(BF16))
        xs = _ffn(xs, norm_ffn_w[layer], w_gate[layer].astype(BF16),
                  w_up[layer].astype(BF16), w_down[layer].astype(BF16), norm_final_w,
                  final_norm=(layer == depth - 1))
    return xs.reshape(batch, seq, d)
```

```python
import functools
import math

import jax
import jax.numpy as jnp
from jax import lax
from jax.experimental import pallas as pl
from jax.experimental.pallas import tpu as pltpu

F32 = jnp.float32
BF16 = jnp.bfloat16

LANES = 128
ATTN_HEAD_DIM = 128
N_ATTN_HEADS = 8
RET_HEAD_DIM = 256
N_RET_HEADS = 4
RET_CHUNK = 128
ATTN_BLK = 128
DILATIONS = (1, 4, 16)
ATTN_LOOKAHEAD = 6
RET_LOOKAHEAD = 2
NORM_EPS = 1e-6
MASK_VALUE = -1e30
LOG2E = math.log2(math.e)
QK_SCALE = LOG2E / math.sqrt(ATTN_HEAD_DIM)
NORM_ROWS = 128
VMEM_LIMIT = 48 * 1024 * 1024


def _rms_rows(x, w):
    ms = jnp.mean(x * x, axis=-1, keepdims=True)
    return x * lax.rsqrt(ms + NORM_EPS) * w


def _norm_into(x_ref, w_ref, h_ref, rows):
    def body(i, c):
        r = pl.multiple_of(i * NORM_ROWS, NORM_ROWS)
        h_ref[pl.ds(r, NORM_ROWS), :] = _rms_rows(
            x_ref[pl.ds(r, NORM_ROWS), :], w_ref[...]).astype(h_ref.dtype)
        return c
    lax.fori_loop(0, rows // NORM_ROWS, body, 0)


def _inproj_kernel(x_ref, nw_ref, w_ref, o_ref, h_ref, *, tm, tn):
    @pl.when(pl.program_id(1) == 0)
    def _():
        _norm_into(x_ref, nw_ref, h_ref, tm)

    acc = jnp.dot(h_ref[...], w_ref[...], preferred_element_type=F32)
    for c in range(tn // LANES):
        o_ref[c] = acc[:, c * LANES:(c + 1) * LANES].astype(o_ref.dtype)


def _in_proj(x2d, norm_w, w_bf16, *, tm=1024, tn=512):
    m, d = x2d.shape
    n = w_bf16.shape[1]
    return pl.pallas_call(
        functools.partial(_inproj_kernel, tm=tm, tn=tn),
        grid=(m // tm, n // tn),
        in_specs=[
            pl.BlockSpec((tm, d), lambda i, j: (i, 0)),
            pl.BlockSpec((1, d), lambda i, j: (0, 0)),
            pl.BlockSpec((d, tn), lambda i, j: (0, j)),
        ],
        out_specs=pl.BlockSpec((tn // LANES, tm, LANES), lambda i, j: (j, i, 0)),
        out_shape=jax.ShapeDtypeStruct((n // LANES, m, LANES), BF16),
        scratch_shapes=[pltpu.VMEM((tm, d), BF16)],
        compiler_params=pltpu.CompilerParams(
            dimension_semantics=("parallel", "arbitrary"),
            vmem_limit_bytes=VMEM_LIMIT),
        name="in_proj",
    )(x2d, norm_w.reshape(1, d), w_bf16)


def _inproj_attn_kernel(x_ref, nw_ref, w_ref, o1_ref, o4_ref, o16_ref, h_ref, nat, mod4,
                        *, tm, tn, q_tiles):
    j = pl.program_id(1)

    @pl.when(j == 0)
    def _():
        _norm_into(x_ref, nw_ref, h_ref, tm)

    acc = jnp.dot(h_ref[...], w_ref[...], preferred_element_type=F32)
    acc = acc * jnp.where(j < q_tiles, QK_SCALE, 1.0)
    for c in range(tn // LANES):
        slab = acc[:, c * LANES:(c + 1) * LANES]
        o1_ref[c] = slab.astype(o1_ref.dtype)
        nat[c] = slab
    n4, n16 = tm // 4, tm // 16
    for c in range(tn // LANES):
        for r in range(4):
            rows = nat[c, pl.ds(r, n4, stride=4), :]
            mod4[c, r * n4:(r + 1) * n4, :] = rows
            o4_ref[c, :, r * LANES:(r + 1) * LANES] = rows.astype(o4_ref.dtype)
    for c in range(tn // LANES):
        for r in range(16):
            a, r4 = divmod(r, 4)
            rows = mod4[c, pl.ds(r4 * n4 + a, n16, stride=4), :]
            o16_ref[c, :, r * LANES:(r + 1) * LANES] = rows.astype(o16_ref.dtype)


def _in_proj_attn(x2d, norm_w, w_bf16, *, batch, seq, tm=1024, tn=512):
    m, d = x2d.shape
    n = w_bf16.shape[1]
    ns = tn // LANES
    per_b = seq // tm
    q_tiles = N_ATTN_HEADS * ATTN_HEAD_DIM // tn
    return pl.pallas_call(
        functools.partial(_inproj_attn_kernel, tm=tm, tn=tn, q_tiles=q_tiles),
        grid=(m // tm, n // tn),
        in_specs=[
            pl.BlockSpec((tm, d), lambda i, j: (i, 0)),
            pl.BlockSpec((1, d), lambda i, j: (0, 0)),
            pl.BlockSpec((d, tn), lambda i, j: (0, j)),
        ],
        out_specs=[
            pl.BlockSpec((ns, tm, LANES), lambda i, j: (j, i, 0)),
            pl.BlockSpec((ns, None, tm // 4, 4 * LANES),
                         lambda i, j: (j, i // per_b, i % per_b, 0)),
            pl.BlockSpec((ns, None, tm // 16, 16 * LANES),
                         lambda i, j: (j, i // per_b, i % per_b, 0)),
        ],
        out_shape=[
            jax.ShapeDtypeStruct((n // LANES, m, LANES), BF16),
            jax.ShapeDtypeStruct((n // LANES, batch, seq // 4, 4 * LANES), BF16),
            jax.ShapeDtypeStruct((n // LANES, batch, seq // 16, 16 * LANES), BF16),
        ],
        scratch_shapes=[pltpu.VMEM((tm, d), BF16),
                        pltpu.VMEM((ns, tm, LANES), F32),
                        pltpu.VMEM((ns, tm, LANES), F32)],
        compiler_params=pltpu.CompilerParams(
            dimension_semantics=("parallel", "arbitrary"),
            vmem_limit_bytes=VMEM_LIMIT),
        name="in_proj_attn",
    )(x2d, norm_w.reshape(1, d), w_bf16)


def _attn_kernel(slopes_ref, q1, k1, v1, q4, k4, v4, q16, k16, v16, o_ref,
                 bm, macc, lacc, oacc, onat, *, seq):
    blk = ATTN_BLK
    nblk = seq // blk
    per4 = nblk // 4
    slope = slopes_ref[pl.program_id(1)]

    qi = lax.broadcasted_iota(jnp.int32, (blk, 2 * blk), 0)
    kj = lax.broadcasted_iota(jnp.int32, (blk, 2 * blk), 1)
    diff = qi - kj + blk
    valid = (diff >= 0) & (diff <= blk)
    dist = diff.astype(F32)
    for pi, dil in enumerate(DILATIONS):
        bm[pi] = jnp.where(valid, (-LOG2E * slope) * (dist * float(dil)), MASK_VALUE)

    def scores(pi, q_at, k_at):
        kk = k_at()
        bias = bm[pi] if kk.shape[0] == 2 * blk else bm[pi, :, blk:]
        return lax.dot_general(q_at(), kk, (((1,), (1,)), ((), ())),
                               preferred_element_type=F32) + bias

    def finish(s, pi, v_at, dst):
        mx = jnp.max(s, axis=-1, keepdims=True)
        p = jnp.exp2(s - mx).astype(BF16)
        vv = v_at()
        acc = jnp.dot(p, jnp.concatenate([vv, jnp.ones_like(vv)], axis=-1),
                      preferred_element_type=F32)
        oacc[pi, dst, :] = acc[:, :LANES]
        lacc[pi, dst, :] = acc[:, LANES:]
        macc[pi, dst, :] = jnp.broadcast_to(mx, (blk, LANES))

    work = []
    for i in range(nblk):
        rows = slice(i * blk, (i + 1) * blk)
        kr = slice(max(i - 1, 0) * blk, (i + 1) * blk)
        work.append((0, lambda rows=rows: q1[rows, :], lambda kr=kr: k1[kr, :],
                     lambda kr=kr: v1[kr, :], rows))
        r4, n = divmod(i, per4)
        ln = slice(r4 * LANES, (r4 + 1) * LANES)
        qr = slice(n * blk, (n + 1) * blk)
        kr = slice(max(n - 1, 0) * blk, (n + 1) * blk)
        work.append((1, lambda qr=qr, ln=ln: q4[qr, ln], lambda kr=kr, ln=ln: k4[kr, ln],
                     lambda kr=kr, ln=ln: v4[kr, ln], rows))
        ln = slice(i * LANES, (i + 1) * LANES)
        dst = pl.ds((i % 4) * (seq // 4) + i // 4, blk, stride=4)
        work.append((2, lambda ln=ln: q16[:, ln], lambda ln=ln: k16[:, ln],
                     lambda ln=ln: v16[:, ln], dst))

    pending = {}
    for t in range(len(work) + ATTN_LOOKAHEAD):
        if t < len(work):
            pi, q_at, k_at, _, _ = work[t]
            pending[t] = scores(pi, q_at, k_at)
        if t >= ATTN_LOOKAHEAD:
            pi, _, _, v_at, dst = work[t - ATTN_LOOKAHEAD]
            finish(pending.pop(t - ATTN_LOOKAHEAD), pi, v_at, dst)

    for c in range(nblk):
        r4, n = divmod(c, per4)
        rows = slice(c * blk, (c + 1) * blk)
        nat = pl.ds(r4 + 4 * blk * n, blk, stride=4)
        at = (nat, rows, rows)
        ms = [macc[pi, at[pi], :] for pi in range(3)]
        mx = jnp.maximum(jnp.maximum(ms[0], ms[1]), ms[2])
        es = [jnp.exp2(m - mx) for m in ms]
        den = es[0] * lacc[0, nat, :] + es[1] * lacc[1, rows, :] + es[2] * lacc[2, rows, :]
        num = es[0] * oacc[0, nat, :] + es[1] * oacc[1, rows, :] + es[2] * oacc[2, rows, :]
        onat[nat, :] = num * (1.0 / den)
    o_ref[...] = onat[...].astype(o_ref.dtype)


def _attention(nat, mod4, mod16, slopes, *, batch, seq):
    h = N_ATTN_HEADS
    specs = [pl.BlockSpec((None, seq, LANES), lambda b, hh, t=t: (t * h + hh, b, 0))
             for t in range(3)]
    for d in DILATIONS[1:]:
        specs += [pl.BlockSpec((None, None, seq // d, d * LANES),
                               lambda b, hh, t=t: (t * h + hh, b, 0, 0)) for t in range(3)]
    return pl.pallas_call(
        functools.partial(_attn_kernel, seq=seq),
        grid=(batch, h),
        in_specs=[pl.BlockSpec(memory_space=pltpu.SMEM)] + specs,
        out_specs=pl.BlockSpec((None, seq, LANES), lambda b, hh: (hh, b, 0)),
        out_shape=jax.ShapeDtypeStruct((h, batch * seq, LANES), BF16),
        scratch_shapes=[
            pltpu.VMEM((3, ATTN_BLK, 2 * ATTN_BLK), F32),
            pltpu.VMEM((3, seq, LANES), F32),
            pltpu.VMEM((3, seq, LANES), F32),
            pltpu.VMEM((3, seq, LANES), F32),
            pltpu.VMEM((seq, LANES), F32),
        ],
        compiler_params=pltpu.CompilerParams(
            dimension_semantics=("parallel", "parallel"),
            vmem_limit_bytes=VMEM_LIMIT),
        name="dilated_attention",
    )(slopes, nat, nat, nat, mod4, mod4, mod4, mod16, mod16, mod16)


def _ret_kernel(lg_ref, q_ref, k_ref, v_ref, g_ref, o_ref, decay, zeta, xi, states, *, seq):
    c = RET_CHUNK
    dh = RET_HEAD_DIM
    nc = seq // c
    lg = lg_ref[pl.program_id(1)]
    k_scale = 1.0 / math.sqrt(dh)
    assert math.log2(k_scale).is_integer()

    ii = lax.broadcasted_iota(jnp.int32, (c, c), 0)
    jj = lax.broadcasted_iota(jnp.int32, (c, c), 1)
    dif = (ii - jj).astype(F32)
    decay[...] = jnp.where(dif >= 0, jnp.exp(lg * jnp.maximum(dif, 0.0)), 0.0) * k_scale
    idx = lax.broadcasted_iota(jnp.int32, (c, dh), 0).astype(F32)
    zeta[...] = jnp.exp(lg * (c - 1.0 - idx)) * k_scale
    xi[...] = jnp.exp(lg * (idx + 1.0))
    gamma_chunk = jnp.exp(jnp.full((dh, dh), lg * c, F32))

    def wide(ref, n):
        rows = slice(n * c, (n + 1) * c)
        return jnp.concatenate([ref[0, rows, :], ref[1, rows, :]], axis=-1)

    st = jnp.zeros((dh, dh), F32)
    for n in range(nc):
        states[n] = st.astype(BF16)
        if n + 1 < nc:
            kz_t = (wide(k_ref, n).astype(F32) * zeta[...]).T.astype(BF16)
            st = st * gamma_chunk + jnp.dot(kz_t, wide(v_ref, n), preferred_element_type=F32)

    def front(n):
        qn = wide(q_ref, n)
        sc = lax.dot_general(qn, wide(k_ref, n), (((1,), (1,)), ((), ())),
                             preferred_element_type=F32) * decay[...]
        cross = jnp.dot(qn, states[n], preferred_element_type=F32) * xi[...]
        return sc.astype(BF16), cross

    def back(n, sc, cross):
        ret = jnp.dot(sc, wide(v_ref, n), preferred_element_type=F32) + cross
        ret = ret * lax.rsqrt(jnp.mean(ret * ret, axis=-1, keepdims=True) + NORM_EPS)
        gate = wide(g_ref, n).astype(F32)
        out = (gate * jax.nn.sigmoid(gate) * ret).astype(o_ref.dtype)
        o_ref[0, n * c:(n + 1) * c, :] = out[:, :LANES]
        o_ref[1, n * c:(n + 1) * c, :] = out[:, LANES:]

    pending = {}
    for t in range(nc + RET_LOOKAHEAD):
        if t < nc:
            pending[t] = front(t)
        if t >= RET_LOOKAHEAD:
            back(t - RET_LOOKAHEAD, *pending.pop(t - RET_LOOKAHEAD))


def _retention(proj_hm, log_gamma, *, batch, seq):
    h = N_RET_HEADS
    blk = (2, seq, LANES)
    return pl.pallas_call(
        functools.partial(_ret_kernel, seq=seq),
        grid=(batch, h),
        in_specs=[
            pl.BlockSpec(memory_space=pltpu.SMEM),
            pl.BlockSpec(blk, lambda b, hh: (hh, b, 0)),
            pl.BlockSpec(blk, lambda b, hh: (h + hh, b, 0)),
            pl.BlockSpec(blk, lambda b, hh: (2 * h + hh, b, 0)),
            pl.BlockSpec(blk, lambda b, hh: (3 * h + hh, b, 0)),
        ],
        out_specs=pl.BlockSpec(blk, lambda b, hh: (hh, b, 0)),
        out_shape=jax.ShapeDtypeStruct((2 * h, batch * seq, LANES), BF16),
        scratch_shapes=[
            pltpu.VMEM((RET_CHUNK, RET_CHUNK), F32),
            pltpu.VMEM((RET_CHUNK, RET_HEAD_DIM), F32),
            pltpu.VMEM((RET_CHUNK, RET_HEAD_DIM), F32),
            pltpu.VMEM((seq // RET_CHUNK, RET_HEAD_DIM, RET_HEAD_DIM), BF16),
        ],
        compiler_params=pltpu.CompilerParams(
            dimension_semantics=("parallel", "parallel"),
            vmem_limit_bytes=VMEM_LIMIT),
        name="retention",
    )(log_gamma, proj_hm, proj_hm, proj_hm, proj_hm)


def _outproj_kernel(x_ref, a_ref, r_ref, wa_ref, wr_ref, o_ref):
    na = a_ref.shape[0]
    nr = r_ref.shape[0]
    a = jnp.concatenate([a_ref[i] for i in range(na)], axis=-1)
    r = jnp.concatenate([r_ref[i] for i in range(nr)], axis=-1)
    y = jnp.dot(a, wa_ref[...], preferred_element_type=F32)
    y = y + jnp.dot(r, wr_ref[...], preferred_element_type=F32)
    o_ref[...] = x_ref[...] + y


def _out_proj(x2d, attn_hm, ret_hm, w_out_bf16, *, tm=512):
    m, d = x2d.shape
    na, nr = attn_hm.shape[0], ret_hm.shape[0]
    wa = w_out_bf16[:na * LANES]
    wr = w_out_bf16[na * LANES:]
    return pl.pallas_call(
        _outproj_kernel,
        grid=(m // tm,),
        in_specs=[
            pl.BlockSpec((tm, d), lambda i: (i, 0)),
            pl.BlockSpec((na, tm, LANES), lambda i: (0, i, 0)),
            pl.BlockSpec((nr, tm, LANES), lambda i: (0, i, 0)),
            pl.BlockSpec((na * LANES, d), lambda i: (0, 0)),
            pl.BlockSpec((nr * LANES, d), lambda i: (0, 0)),
        ],
        out_specs=pl.BlockSpec((tm, d), lambda i: (i, 0)),
        out_shape=jax.ShapeDtypeStruct((m, d), F32),
        compiler_params=pltpu.CompilerParams(
            dimension_semantics=("parallel",),
            vmem_limit_bytes=VMEM_LIMIT),
        name="out_proj",
    )(x2d, attn_hm, ret_hm, wa, wr)


def _ffn_kernel(x_ref, nw_ref, wg_ref, wu_ref, wd_ref, fw_ref, o_ref, h_ref, acc_ref,
                *, tm, final_norm):
    f = pl.program_id(1)

    @pl.when(f == 0)
    def _():
        _norm_into(x_ref, nw_ref, h_ref, tm)

    h = h_ref[...]
    g = jnp.dot(h, wg_ref[...], preferred_element_type=F32)
    u = jnp.dot(h, wu_ref[...], preferred_element_type=F32)
    a = (g * jax.nn.sigmoid(g) * u).astype(BF16)
    y = jnp.dot(a, wd_ref[...], preferred_element_type=F32)

    @pl.when(f == 0)
    def _():
        acc_ref[...] = y

    @pl.when(f > 0)
    def _():
        acc_ref[...] += y

    @pl.when(f == pl.num_programs(1) - 1)
    def _():
        def body(i, c):
            r = pl.multiple_of(i * NORM_ROWS, NORM_ROWS)
            x2 = x_ref[pl.ds(r, NORM_ROWS), :] + acc_ref[pl.ds(r, NORM_ROWS), :]
            if final_norm:
                x2 = _rms_rows(x2, fw_ref[...])
            o_ref[pl.ds(r, NORM_ROWS), :] = x2
            return c
        lax.fori_loop(0, tm // NORM_ROWS, body, 0)


def _ffn(x2d, norm_w, wg, wu, wd, final_w, *, final_norm, tm=512, tf=512):
    m, d = x2d.shape
    hid = wg.shape[1]
    return pl.pallas_call(
        functools.partial(_ffn_kernel, tm=tm, final_norm=final_norm),
        grid=(m // tm, hid // tf),
        in_specs=[
            pl.BlockSpec((tm, d), lambda i, f: (i, 0)),
            pl.BlockSpec((1, d), lambda i, f: (0, 0)),
            pl.BlockSpec((d, tf), lambda i, f: (0, f)),
            pl.BlockSpec((d, tf), lambda i, f: (0, f)),
            pl.BlockSpec((tf, d), lambda i, f: (f, 0)),
            pl.BlockSpec((1, d), lambda i, f: (0, 0)),
        ],
        out_specs=pl.BlockSpec((tm, d), lambda i, f: (i, 0)),
        out_shape=jax.ShapeDtypeStruct((m, d), F32),
        scratch_shapes=[pltpu.VMEM((tm, d), BF16), pltpu.VMEM((tm, d), F32)],
        compiler_params=pltpu.CompilerParams(
            dimension_semantics=("parallel", "arbitrary"),
            vmem_limit_bytes=VMEM_LIMIT),
        name="ffn",
    )(x2d, norm_w.reshape(1, d), wg, wu, wd, final_w.reshape(1, d))


def kernel(x, norm_mix_w, w_in, w_out, norm_ffn_w, w_gate, w_up, w_down, norm_final_w):
    batch, seq, d = x.shape
    depth = w_in.shape[0]
    assert seq == DILATIONS[-1] * ATTN_BLK and seq % RET_CHUNK == 0
    slopes = jnp.exp2(-8.0 * jnp.arange(1, N_ATTN_HEADS + 1, dtype=F32) / N_ATTN_HEADS)
    log_gamma = jnp.log(1.0 - jnp.exp2(-5.0 - jnp.arange(N_RET_HEADS, dtype=F32)))

    xs = x.reshape(batch * seq, d)
    for layer in range(depth):
        w_in_l = w_in[layer].astype(BF16)
        n_attn = 3 * N_ATTN_HEADS * ATTN_HEAD_DIM
        nat, mod4, mod16 = _in_proj_attn(xs, norm_mix_w[layer], w_in_l[:, :n_attn],
                                         batch=batch, seq=seq)
        proj_ret = _in_proj(xs, norm_mix_w[layer], w_in_l[:, n_attn:])
        attn = _attention(nat, mod4, mod16, slopes, batch=batch, seq=seq)
        ret = _retention(proj_ret, log_gamma, batch=batch, seq=seq)
        xs = _out_proj(xs, attn, ret, w_out[layer].astype(BF16))
        xs = _ffn(xs, norm_ffn_w[layer], w_gate[layer].astype(BF16),
                  w_up[layer].astype(BF16), w_down[layer].astype(BF16), norm_final_w,
                  final_norm=(layer == depth - 1))
    return xs.reshape(batch, seq, d)
```

```python
import functools
import math

import jax
import jax.numpy as jnp
from jax import lax
from jax.experimental import pallas as pl
from jax.experimental.pallas import tpu as pltpu

F32 = jnp.float32
BF16 = jnp.bfloat16

LANES = 128
MXU_COLS = 256
ATTN_HEAD_DIM = 128
N_ATTN_HEADS = 8
RET_HEAD_DIM = 256
N_RET_HEADS = 4
RET_CHUNK = 128
ATTN_BLK = 128
DILATIONS = (1, 4, 16)
ATTN_LOOKAHEAD = 6
INPROJ_ROW_PARTS = 4
RET_LOOKAHEAD = 2
NORM_EPS = 1e-6
MASK_VALUE = -1e30
LOG2E = math.log2(math.e)
QK_SCALE = LOG2E / math.sqrt(ATTN_HEAD_DIM)
NORM_ROWS = 128
VMEM_LIMIT = 48 * 1024 * 1024


def _rms_rows(x, w):
    ms = jnp.mean(x * x, axis=-1, keepdims=True)
    return x * lax.rsqrt(ms + NORM_EPS) * w


def _norm_into(x_ref, w_ref, h_ref, rows, zero_ref=None):
    def body(i, c):
        r = pl.multiple_of(i * NORM_ROWS, NORM_ROWS)
        h_ref[pl.ds(r, NORM_ROWS), :] = _rms_rows(
            x_ref[pl.ds(r, NORM_ROWS), :], w_ref[...]).astype(h_ref.dtype)
        if zero_ref is not None:
            zero_ref[pl.ds(r, NORM_ROWS), :] = jnp.zeros((NORM_ROWS, zero_ref.shape[1]),
                                                         zero_ref.dtype)
        return c
    lax.fori_loop(0, rows // NORM_ROWS, body, 0)


def _dot_parts(h_ref, w_ref, tm, tn):
    rp = tm // INPROJ_ROW_PARTS
    slabs = []
    for p in range(INPROJ_ROW_PARTS):
        for q in range(tn // MXU_COLS):
            acc = jnp.dot(h_ref[p * rp:(p + 1) * rp, :],
                          w_ref[:, q * MXU_COLS:(q + 1) * MXU_COLS],
                          preferred_element_type=F32)
            for s in range(MXU_COLS // LANES):
                slabs.append((p, q * (MXU_COLS // LANES) + s,
                              acc[:, s * LANES:(s + 1) * LANES]))
    return rp, slabs


def _inproj_kernel(x_ref, nw_ref, w_ref, o_ref, h_ref, *, tm, tn):
    @pl.when(pl.program_id(1) == 0)
    def _():
        _norm_into(x_ref, nw_ref, h_ref, tm)

    rp, parts = _dot_parts(h_ref, w_ref, tm, tn)
    for p, c, slab in parts:
        o_ref[c, p * rp:(p + 1) * rp, :] = slab.astype(o_ref.dtype)


def _in_proj(x2d, norm_w, w_bf16, *, col0, tm=1024, tn=512):
    m, d = x2d.shape
    n = w_bf16.shape[1] - col0
    j0 = col0 // tn
    return pl.pallas_call(
        functools.partial(_inproj_kernel, tm=tm, tn=tn),
        grid=(m // tm, n // tn),
        in_specs=[
            pl.BlockSpec((tm, d), lambda i, j: (i, 0)),
            pl.BlockSpec((1, d), lambda i, j: (0, 0)),
            pl.BlockSpec((d, tn), lambda i, j: (0, j0 + j)),
        ],
        out_specs=pl.BlockSpec((tn // LANES, tm, LANES), lambda i, j: (j, i, 0)),
        out_shape=jax.ShapeDtypeStruct((n // LANES, m, LANES), BF16),
        scratch_shapes=[pltpu.VMEM((tm, d), BF16)],
        compiler_params=pltpu.CompilerParams(
            dimension_semantics=("parallel", "arbitrary"),
            vmem_limit_bytes=VMEM_LIMIT),
        name="in_proj",
    )(x2d, norm_w.reshape(1, d), w_bf16)


def _inproj_attn_kernel(x_ref, nw_ref, w_ref, o1_ref, o4_ref, o16_ref, h_ref, nat, mod4,
                        *, tm, tn, q_tiles):
    j = pl.program_id(1)

    @pl.when(j == 0)
    def _():
        _norm_into(x_ref, nw_ref, h_ref, tm)

    scale = jnp.where(j < q_tiles, QK_SCALE, 1.0)
    rp, parts = _dot_parts(h_ref, w_ref, tm, tn)
    n4, n16 = rp // 4, rp // 16
    for p, c, slab in parts:
        slab = slab * scale
        o1_ref[c, p * rp:(p + 1) * rp, :] = slab.astype(o1_ref.dtype)
        nat[p, c] = slab
        for r in range(4):
            rows = nat[p, c, pl.ds(r, n4, stride=4), :]
            mod4[p, c, r * n4:(r + 1) * n4, :] = rows
            o4_ref[c, p * n4:(p + 1) * n4, r * LANES:(r + 1) * LANES] = rows.astype(o4_ref.dtype)
        for r in range(16):
            a, r4 = divmod(r, 4)
            rows = mod4[p, c, pl.ds(r4 * n4 + a, n16, stride=4), :]
            o16_ref[c, p * n16:(p + 1) * n16, r * LANES:(r + 1) * LANES] = rows.astype(
                o16_ref.dtype)


def _in_proj_attn(x2d, norm_w, w_bf16, *, batch, seq, tm=1024, tn=512):
    m, d = x2d.shape
    n = 3 * N_ATTN_HEADS * ATTN_HEAD_DIM
    ns = tn // LANES
    per_b = seq // tm
    q_tiles = N_ATTN_HEADS * ATTN_HEAD_DIM // tn
    return pl.pallas_call(
        functools.partial(_inproj_attn_kernel, tm=tm, tn=tn, q_tiles=q_tiles),
        grid=(m // tm, n // tn),
        in_specs=[
            pl.BlockSpec((tm, d), lambda i, j: (i, 0)),
            pl.BlockSpec((1, d), lambda i, j: (0, 0)),
            pl.BlockSpec((d, tn), lambda i, j: (0, j)),
        ],
        out_specs=[
            pl.BlockSpec((ns, tm, LANES), lambda i, j: (j, i, 0)),
            pl.BlockSpec((ns, None, tm // 4, 4 * LANES),
                         lambda i, j: (j, i // per_b, i % per_b, 0)),
            pl.BlockSpec((ns, None, tm // 16, 16 * LANES),
                         lambda i, j: (j, i // per_b, i % per_b, 0)),
        ],
        out_shape=[
            jax.ShapeDtypeStruct((n // LANES, m, LANES), BF16),
            jax.ShapeDtypeStruct((n // LANES, batch, seq // 4, 4 * LANES), BF16),
            jax.ShapeDtypeStruct((n // LANES, batch, seq // 16, 16 * LANES), BF16),
        ],
        scratch_shapes=[pltpu.VMEM((tm, d), BF16),
                        pltpu.VMEM((INPROJ_ROW_PARTS, ns, tm // INPROJ_ROW_PARTS, LANES), F32),
                        pltpu.VMEM((INPROJ_ROW_PARTS, ns, tm // INPROJ_ROW_PARTS, LANES), F32)],
        compiler_params=pltpu.CompilerParams(
            dimension_semantics=("parallel", "arbitrary"),
            vmem_limit_bytes=VMEM_LIMIT),
        name="in_proj_attn",
    )(x2d, norm_w.reshape(1, d), w_bf16)


def _attn_kernel(slopes_ref, q1, k1, v1, q4, k4, v4, q16, k16, v16, o_ref,
                 bm, macc, lacc, oacc, onat, *, seq):
    blk = ATTN_BLK
    nblk = seq // blk
    per4 = nblk // 4
    slope = slopes_ref[pl.program_id(1)]

    qi = lax.broadcasted_iota(jnp.int32, (blk, 2 * blk), 0)
    kj = lax.broadcasted_iota(jnp.int32, (blk, 2 * blk), 1)
    diff = qi - kj + blk
    valid = (diff >= 0) & (diff <= blk)
    dist = diff.astype(F32)
    for pi, dil in enumerate(DILATIONS):
        bm[pi] = jnp.where(valid, (-LOG2E * slope) * (dist * float(dil)), MASK_VALUE)

    def scores(pi, q_at, k_at):
        kk = k_at()
        bias = bm[pi] if kk.shape[0] == 2 * blk else bm[pi, :, blk:]
        return lax.dot_general(q_at(), kk, (((1,), (1,)), ((), ())),
                               preferred_element_type=F32) + bias

    def finish(s, pi, v_at, dst):
        mx = jnp.max(s, axis=-1, keepdims=True)
        p = jnp.exp2(s - mx).astype(BF16)
        vv = v_at()
        acc = jnp.dot(p, jnp.concatenate([vv, jnp.ones_like(vv)], axis=-1),
                      preferred_element_type=F32)
        oacc[pi, dst, :] = acc[:, :LANES]
        lacc[pi, dst, :] = acc[:, LANES:]
        macc[pi, dst, :] = jnp.broadcast_to(mx, (blk, LANES))

    work = []
    for i in range(nblk):
        rows = slice(i * blk, (i + 1) * blk)
        kr = slice(max(i - 1, 0) * blk, (i + 1) * blk)
        work.append((0, lambda rows=rows: q1[rows, :], lambda kr=kr: k1[kr, :],
                     lambda kr=kr: v1[kr, :], rows))
        r4, n = divmod(i, per4)
        ln = slice(r4 * LANES, (r4 + 1) * LANES)
        qr = slice(n * blk, (n + 1) * blk)
        kr = slice(max(n - 1, 0) * blk, (n + 1) * blk)
        work.append((1, lambda qr=qr, ln=ln: q4[qr, ln], lambda kr=kr, ln=ln: k4[kr, ln],
                     lambda kr=kr, ln=ln: v4[kr, ln], rows))
        ln = slice(i * LANES, (i + 1) * LANES)
        dst = pl.ds((i % 4) * (seq // 4) + i // 4, blk, stride=4)
        work.append((2, lambda ln=ln: q16[:, ln], lambda ln=ln: k16[:, ln],
                     lambda ln=ln: v16[:, ln], dst))

    pending = {}
    for t in range(len(work) + ATTN_LOOKAHEAD):
        if t < len(work):
            pi, q_at, k_at, _, _ = work[t]
            pending[t] = scores(pi, q_at, k_at)
        if t >= ATTN_LOOKAHEAD:
            pi, _, _, v_at, dst = work[t - ATTN_LOOKAHEAD]
            finish(pending.pop(t - ATTN_LOOKAHEAD), pi, v_at, dst)

    for c in range(nblk):
        r4, n = divmod(c, per4)
        rows = slice(c * blk, (c + 1) * blk)
        nat = pl.ds(r4 + 4 * blk * n, blk, stride=4)
        at = (nat, rows, rows)
        ms = [macc[pi, at[pi], :] for pi in range(3)]
        mx = jnp.maximum(jnp.maximum(ms[0], ms[1]), ms[2])
        es = [jnp.exp2(m - mx) for m in ms]
        den = es[0] * lacc[0, nat, :] + es[1] * lacc[1, rows, :] + es[2] * lacc[2, rows, :]
        num = es[0] * oacc[0, nat, :] + es[1] * oacc[1, rows, :] + es[2] * oacc[2, rows, :]
        onat[nat, :] = num * (1.0 / den)
    o_ref[...] = onat[...].astype(o_ref.dtype)


def _attention(nat, mod4, mod16, slopes, *, batch, seq):
    h = N_ATTN_HEADS
    specs = [pl.BlockSpec((None, seq, LANES), lambda b, hh, t=t: (t * h + hh, b, 0))
             for t in range(3)]
    for d in DILATIONS[1:]:
        specs += [pl.BlockSpec((None, None, seq // d, d * LANES),
                               lambda b, hh, t=t: (t * h + hh, b, 0, 0)) for t in range(3)]
    return pl.pallas_call(
        functools.partial(_attn_kernel, seq=seq),
        grid=(batch, h),
        in_specs=[pl.BlockSpec(memory_space=pltpu.SMEM)] + specs,
        out_specs=pl.BlockSpec((None, seq, LANES), lambda b, hh: (hh, b, 0)),
        out_shape=jax.ShapeDtypeStruct((h, batch * seq, LANES), BF16),
        scratch_shapes=[
            pltpu.VMEM((3, ATTN_BLK, 2 * ATTN_BLK), F32),
            pltpu.VMEM((3, seq, LANES), F32),
            pltpu.VMEM((3, seq, LANES), F32),
            pltpu.VMEM((3, seq, LANES), F32),
            pltpu.VMEM((seq, LANES), F32),
        ],
        compiler_params=pltpu.CompilerParams(
            dimension_semantics=("parallel", "parallel"),
            vmem_limit_bytes=VMEM_LIMIT),
        name="dilated_attention",
    )(slopes, nat, nat, nat, mod4, mod4, mod4, mod16, mod16, mod16)


def _ret_kernel(lg_ref, q_ref, k_ref, v_ref, g_ref, o_ref, decay, zeta, xi, states, *, seq):
    c = RET_CHUNK
    dh = RET_HEAD_DIM
    nc = seq // c
    lg = lg_ref[pl.program_id(1)]
    k_scale = 1.0 / math.sqrt(dh)
    assert math.log2(k_scale).is_integer()

    ii = lax.broadcasted_iota(jnp.int32, (c, c), 0)
    jj = lax.broadcasted_iota(jnp.int32, (c, c), 1)
    dif = (ii - jj).astype(F32)
    decay[...] = jnp.where(dif >= 0, jnp.exp(lg * jnp.maximum(dif, 0.0)), 0.0) * k_scale
    idx = lax.broadcasted_iota(jnp.int32, (c, dh), 0).astype(F32)
    zeta[...] = jnp.exp(lg * (c - 1.0 - idx)) * k_scale
    xi[...] = jnp.exp(lg * (idx + 1.0))
    gamma_chunk = jnp.exp(jnp.full((dh, dh), lg * c, F32))

    def wide(ref, n):
        rows = slice(n * c, (n + 1) * c)
        return jnp.concatenate([ref[0, rows, :], ref[1, rows, :]], axis=-1)

    st = jnp.zeros((dh, dh), F32)
    for n in range(nc):
        states[n] = st.astype(BF16)
        if n + 1 < nc:
            kz_t = (wide(k_ref, n).astype(F32) * zeta[...]).T.astype(BF16)
            st = st * gamma_chunk + jnp.dot(kz_t, wide(v_ref, n), preferred_element_type=F32)

    def front(n):
        qn = wide(q_ref, n)
        sc = lax.dot_general(qn, wide(k_ref, n), (((1,), (1,)), ((), ())),
                             preferred_element_type=F32) * decay[...]
        cross = jnp.dot(qn, states[n], preferred_element_type=F32) * xi[...]
        return sc.astype(BF16), cross

    def back(n, sc, cross):
        ret = jnp.dot(sc, wide(v_ref, n), preferred_element_type=F32) + cross
        ret = ret * lax.rsqrt(jnp.mean(ret * ret, axis=-1, keepdims=True) + NORM_EPS)
        gate = wide(g_ref, n).astype(F32)
        out = (gate * jax.nn.sigmoid(gate) * ret).astype(o_ref.dtype)
        o_ref[0, n * c:(n + 1) * c, :] = out[:, :LANES]
        o_ref[1, n * c:(n + 1) * c, :] = out[:, LANES:]

    pending = {}
    for t in range(nc + RET_LOOKAHEAD):
        if t < nc:
            pending[t] = front(t)
        if t >= RET_LOOKAHEAD:
            back(t - RET_LOOKAHEAD, *pending.pop(t - RET_LOOKAHEAD))


def _retention(proj_hm, log_gamma, *, batch, seq):
    h = N_RET_HEADS
    blk = (2, seq, LANES)
    return pl.pallas_call(
        functools.partial(_ret_kernel, seq=seq),
        grid=(batch, h),
        in_specs=[
            pl.BlockSpec(memory_space=pltpu.SMEM),
            pl.BlockSpec(blk, lambda b, hh: (hh, b, 0)),
            pl.BlockSpec(blk, lambda b, hh: (h + hh, b, 0)),
            pl.BlockSpec(blk, lambda b, hh: (2 * h + hh, b, 0)),
            pl.BlockSpec(blk, lambda b, hh: (3 * h + hh, b, 0)),
        ],
        out_specs=pl.BlockSpec(blk, lambda b, hh: (hh, b, 0)),
        out_shape=jax.ShapeDtypeStruct((2 * h, batch * seq, LANES), BF16),
        scratch_shapes=[
            pltpu.VMEM((RET_CHUNK, RET_CHUNK), F32),
            pltpu.VMEM((RET_CHUNK, RET_HEAD_DIM), F32),
            pltpu.VMEM((RET_CHUNK, RET_HEAD_DIM), F32),
            pltpu.VMEM((seq // RET_CHUNK, RET_HEAD_DIM, RET_HEAD_DIM), BF16),
        ],
        compiler_params=pltpu.CompilerParams(
            dimension_semantics=("parallel", "parallel"),
            vmem_limit_bytes=VMEM_LIMIT),
        name="retention",
    )(log_gamma, proj_hm, proj_hm, proj_hm, proj_hm)


def _outproj_kernel(x_ref, a_ref, r_ref, wa_ref, wr_ref, o_ref):
    na = a_ref.shape[0]
    nr = r_ref.shape[0]
    a = jnp.concatenate([a_ref[i] for i in range(na)], axis=-1)
    r = jnp.concatenate([r_ref[i] for i in range(nr)], axis=-1)
    y = jnp.dot(a, wa_ref[...], preferred_element_type=F32)
    y = y + jnp.dot(r, wr_ref[...], preferred_element_type=F32)
    o_ref[...] = x_ref[...] + y


def _out_proj(x2d, attn_hm, ret_hm, w_out_bf16, *, tm=512):
    m, d = x2d.shape
    na, nr = attn_hm.shape[0], ret_hm.shape[0]
    assert na == nr
    return pl.pallas_call(
        _outproj_kernel,
        grid=(m // tm,),
        in_specs=[
            pl.BlockSpec((tm, d), lambda i: (i, 0)),
            pl.BlockSpec((na, tm, LANES), lambda i: (0, i, 0)),
            pl.BlockSpec((nr, tm, LANES), lambda i: (0, i, 0)),
            pl.BlockSpec((na * LANES, d), lambda i: (0, 0)),
            pl.BlockSpec((nr * LANES, d), lambda i: (1, 0)),
        ],
        out_specs=pl.BlockSpec((tm, d), lambda i: (i, 0)),
        out_shape=jax.ShapeDtypeStruct((m, d), F32),
        compiler_params=pltpu.CompilerParams(
            dimension_semantics=("parallel",),
            vmem_limit_bytes=VMEM_LIMIT),
        name="out_proj",
    )(x2d, attn_hm, ret_hm, w_out_bf16, w_out_bf16)


def _ffn_kernel(x_ref, nw_ref, wg_ref, wu_ref, wd_ref, fw_ref, o_ref, h_ref, acc_ref,
                *, tm, final_norm):
    f = pl.program_id(1)

    @pl.when(f == 0)
    def _():
        _norm_into(x_ref, nw_ref, h_ref, tm, zero_ref=acc_ref)

    h = h_ref[...]
    g = jnp.dot(h, wg_ref[...], preferred_element_type=F32)
    u = jnp.dot(h, wu_ref[...], preferred_element_type=F32)
    a = (g * jax.nn.sigmoid(g) * u).astype(BF16)
    y = jnp.dot(a, wd_ref[...], preferred_element_type=F32)
    acc_ref[...] += y

    @pl.when(f == pl.num_programs(1) - 1)
    def _():
        def body(i, c):
            r = pl.multiple_of(i * NORM_ROWS, NORM_ROWS)
            x2 = x_ref[pl.ds(r, NORM_ROWS), :] + acc_ref[pl.ds(r, NORM_ROWS), :]
            if final_norm:
                x2 = _rms_rows(x2, fw_ref[...])
            o_ref[pl.ds(r, NORM_ROWS), :] = x2
            return c
        lax.fori_loop(0, tm // NORM_ROWS, body, 0)


def _ffn(x2d, norm_w, wg, wu, wd, final_w, *, final_norm, tm=512, tf=512):
    m, d = x2d.shape
    hid = wg.shape[1]
    return pl.pallas_call(
        functools.partial(_ffn_kernel, tm=tm, final_norm=final_norm),
        grid=(m // tm, hid // tf),
        in_specs=[
            pl.BlockSpec((tm, d), lambda i, f: (i, 0)),
            pl.BlockSpec((1, d), lambda i, f: (0, 0)),
            pl.BlockSpec((d, tf), lambda i, f: (0, f)),
            pl.BlockSpec((d, tf), lambda i, f: (0, f)),
            pl.BlockSpec((tf, d), lambda i, f: (f, 0)),
            pl.BlockSpec((1, d), lambda i, f: (0, 0)),
        ],
        out_specs=pl.BlockSpec((tm, d), lambda i, f: (i, 0)),
        out_shape=jax.ShapeDtypeStruct((m, d), F32),
        scratch_shapes=[pltpu.VMEM((tm, d), BF16), pltpu.VMEM((tm, d), F32)],
        compiler_params=pltpu.CompilerParams(
            dimension_semantics=("parallel", "arbitrary"),
            vmem_limit_bytes=VMEM_LIMIT),
        name="ffn",
    )(x2d, norm_w.reshape(1, d), wg, wu, wd, final_w.reshape(1, d))


def kernel(x, norm_mix_w, w_in, w_out, norm_ffn_w, w_gate, w_up, w_down, norm_final_w):
    batch, seq, d = x.shape
    depth = w_in.shape[0]
    assert seq == DILATIONS[-1] * ATTN_BLK and seq % RET_CHUNK == 0
    slopes = jnp.exp2(-8.0 * jnp.arange(1, N_ATTN_HEADS + 1, dtype=F32) / N_ATTN_HEADS)
    log_gamma = jnp.log(1.0 - jnp.exp2(-5.0 - jnp.arange(N_RET_HEADS, dtype=F32)))

    xs = x.reshape(batch * seq, d)
    for layer in range(depth):
        w_in_l = w_in[layer].astype(BF16)
        n_attn = 3 * N_ATTN_HEADS * ATTN_HEAD_DIM
        nat, mod4, mod16 = _in_proj_attn(xs, norm_mix_w[layer], w_in_l, batch=batch, seq=seq)
        proj_ret = _in_proj(xs, norm_mix_w[layer], w_in_l, col0=n_attn)
        attn = _attention(nat, mod4, mod16, slopes, batch=batch, seq=seq)
        ret = _retention(proj_ret, log_gamma, batch=batch, seq=seq)
        xs = _out_proj(xs, attn, ret, w_out[layer].astype(BF16))
        xs = _ffn(xs, norm_ffn_w[layer], w_gate[layer].astype(BF16),
                  w_up[layer].astype(BF16), w_down[layer].astype(BF16), norm_final_w,
                  final_norm=(layer == depth - 1))
    return xs.reshape(batch, seq, d)
```

```python
import functools
import math

import jax
import jax.numpy as jnp
from jax import lax
from jax.experimental import pallas as pl
from jax.experimental.pallas import tpu as pltpu

F32 = jnp.float32
BF16 = jnp.bfloat16

LANES = 128
ATTN_HEAD_DIM = 128
N_ATTN_HEADS = 8
RET_HEAD_DIM = 256
N_RET_HEADS = 4
RET_CHUNK = 128
ATTN_BLK = 128
DILATIONS = (1, 4, 16)
ATTN_LOOKAHEAD = 6
RET_LOOKAHEAD = 2
NORM_EPS = 1e-6
MASK_VALUE = -1e30
LOG2E = math.log2(math.e)
QK_SCALE = LOG2E / math.sqrt(ATTN_HEAD_DIM)
NORM_ROWS = 128
VMEM_LIMIT = 48 * 1024 * 1024


def _rms_rows(x, w):
    ms = jnp.mean(x * x, axis=-1, keepdims=True)
    return x * lax.rsqrt(ms + NORM_EPS) * w


def _norm_into(x_ref, w_ref, h_ref, rows, zero_ref=None):
    def body(i, c):
        r = pl.multiple_of(i * NORM_ROWS, NORM_ROWS)
        h_ref[pl.ds(r, NORM_ROWS), :] = _rms_rows(
            x_ref[pl.ds(r, NORM_ROWS), :], w_ref[...]).astype(h_ref.dtype)
        if zero_ref is not None:
            zero_ref[pl.ds(r, NORM_ROWS), :] = jnp.zeros((NORM_ROWS, zero_ref.shape[1]),
                                                         zero_ref.dtype)
        return c
    lax.fori_loop(0, rows // NORM_ROWS, body, 0)


def _inproj_kernel(x_ref, nw_ref, w_ref, o1_ref, o4_ref, o16_ref, ret_ref, h_even, h_odd,
                   nat, mod4, *, tm, tn, attn_tiles, q_tiles):
    i = pl.program_id(0)
    j = pl.program_id(1)

    def norm_chunk(h_ref):
        c = jnp.minimum(j, tm // NORM_ROWS - 1)
        r = pl.multiple_of(c * NORM_ROWS, NORM_ROWS)
        h_ref[pl.ds(r, NORM_ROWS), :] = _rms_rows(x_ref[...], nw_ref[...]).astype(BF16)

    @pl.when(i == 0)
    def _():
        norm_chunk(h_even)

    for parity, (h_wr, h_rd) in enumerate(((h_even, h_odd), (h_odd, h_even))):
        mine = (i > 0) & (i % 2 == parity)

        @pl.when(mine & (j < attn_tiles))
        def _(h_wr=h_wr, h_rd=h_rd):
            norm_chunk(h_wr)
            _inproj_attn_step(h_rd, w_ref, o1_ref, o4_ref, o16_ref, nat, mod4,
                              jnp.where(j < q_tiles, QK_SCALE, 1.0), tm, tn)

        @pl.when(mine & (j >= attn_tiles))
        def _(h_wr=h_wr, h_rd=h_rd):
            norm_chunk(h_wr)
            acc = jnp.dot(h_rd[...], w_ref[...], preferred_element_type=F32)
            for c in range(tn // LANES):
                ret_ref[c] = acc[:, c * LANES:(c + 1) * LANES].astype(ret_ref.dtype)


def _inproj_attn_step(h_ref, w_ref, o1_ref, o4_ref, o16_ref, nat, mod4, scale, tm, tn):
    acc = jnp.dot(h_ref[...], w_ref[...], preferred_element_type=F32) * scale
    n4, n16 = tm // 4, tm // 16
    for c in range(tn // LANES):
        slab = acc[:, c * LANES:(c + 1) * LANES]
        o1_ref[c] = slab.astype(o1_ref.dtype)
        nat[c] = slab
        for r in range(4):
            rows = nat[c, pl.ds(r, n4, stride=4), :]
            mod4[c, r * n4:(r + 1) * n4, :] = rows
            o4_ref[c, :, r * LANES:(r + 1) * LANES] = rows.astype(o4_ref.dtype)
        for r in range(16):
            a, r4 = divmod(r, 4)
            rows = mod4[c, pl.ds(r4 * n4 + a, n16, stride=4), :]
            o16_ref[c, :, r * LANES:(r + 1) * LANES] = rows.astype(o16_ref.dtype)


def _in_proj(x2d, norm_w, w_bf16, *, batch, seq, tm=1024, tn=512):
    m, d = x2d.shape
    n = w_bf16.shape[1]
    n_attn = 3 * N_ATTN_HEADS * ATTN_HEAD_DIM
    ns = tn // LANES
    tiles = m // tm
    chunks = tm // NORM_ROWS
    nj = n // tn
    attn_tiles = n_attn // tn
    assert nj >= chunks
    per_b = seq // tm
    q_tiles = N_ATTN_HEADS * ATTN_HEAD_DIM // tn

    def t_of(i):
        return jnp.maximum(i - 1, 0)

    def ja(i, j):
        return jnp.where(i == 0, 0, jnp.minimum(j, attn_tiles - 1))

    def jr(i, j):
        return jnp.where(i == 0, 0, jnp.maximum(j - attn_tiles, 0))

    return pl.pallas_call(
        functools.partial(_inproj_kernel, tm=tm, tn=tn, attn_tiles=attn_tiles, q_tiles=q_tiles),
        grid=(tiles + 1, nj),
        in_specs=[
            pl.BlockSpec((NORM_ROWS, d), lambda i, j: (
                jnp.minimum(i, tiles - 1) * chunks + jnp.minimum(j, chunks - 1), 0)),
            pl.BlockSpec((1, d), lambda i, j: (0, 0)),
            pl.BlockSpec((d, tn), lambda i, j: (0, jnp.where(i == 0, 0, j))),
        ],
        out_specs=[
            pl.BlockSpec((ns, tm, LANES), lambda i, j: (ja(i, j), t_of(i), 0)),
            pl.BlockSpec((ns, None, tm // 4, 4 * LANES),
                         lambda i, j: (ja(i, j), t_of(i) // per_b, t_of(i) % per_b, 0)),
            pl.BlockSpec((ns, None, tm // 16, 16 * LANES),
                         lambda i, j: (ja(i, j), t_of(i) // per_b, t_of(i) % per_b, 0)),
            pl.BlockSpec((ns, tm, LANES), lambda i, j: (jr(i, j), t_of(i), 0)),
        ],
        out_shape=[
            jax.ShapeDtypeStruct((n_attn // LANES, m, LANES), BF16),
            jax.ShapeDtypeStruct((n_attn // LANES, batch, seq // 4, 4 * LANES), BF16),
            jax.ShapeDtypeStruct((n_attn // LANES, batch, seq // 16, 16 * LANES), BF16),
            jax.ShapeDtypeStruct(((n - n_attn) // LANES, m, LANES), BF16),
        ],
        scratch_shapes=[pltpu.VMEM((tm, d), BF16),
                        pltpu.VMEM((tm, d), BF16),
                        pltpu.VMEM((ns, tm, LANES), F32),
                        pltpu.VMEM((ns, tm, LANES), F32)],
        compiler_params=pltpu.CompilerParams(
            dimension_semantics=("arbitrary", "arbitrary"),
            vmem_limit_bytes=VMEM_LIMIT),
        name="in_proj",
    )(x2d, norm_w.reshape(1, d), w_bf16)


def _attn_kernel(slopes_ref, q1, k1, v1, q4, k4, v4, q16, k16, v16, o_ref,
                 bm, macc, lacc, oacc, onat, *, seq):
    blk = ATTN_BLK
    nblk = seq // blk
    per4 = nblk // 4
    slope = slopes_ref[pl.program_id(1)]

    qi = lax.broadcasted_iota(jnp.int32, (blk, 2 * blk), 0)
    kj = lax.broadcasted_iota(jnp.int32, (blk, 2 * blk), 1)
    diff = qi - kj + blk
    valid = (diff >= 0) & (diff <= blk)
    dist = diff.astype(F32)
    for pi, dil in enumerate(DILATIONS):
        bm[pi] = jnp.where(valid, (-LOG2E * slope) * (dist * float(dil)), MASK_VALUE)

    def scores(pi, q_at, k_at):
        kk = k_at()
        bias = bm[pi] if kk.shape[0] == 2 * blk else bm[pi, :, blk:]
        return lax.dot_general(q_at(), kk, (((1,), (1,)), ((), ())),
                               preferred_element_type=F32) + bias

    def finish(s, pi, v_at, dst):
        mx = jnp.max(s, axis=-1, keepdims=True)
        p = jnp.exp2(s - mx).astype(BF16)
        vv = v_at()
        acc = jnp.dot(p, jnp.concatenate([vv, jnp.ones_like(vv)], axis=-1),
                      preferred_element_type=F32)
        oacc[pi, dst, :] = acc[:, :LANES]
        lacc[pi, dst, :] = acc[:, LANES:]
        macc[pi, dst, :] = jnp.broadcast_to(mx, (blk, LANES))

    work = []
    for i in range(nblk):
        rows = slice(i * blk, (i + 1) * blk)
        kr = slice(max(i - 1, 0) * blk, (i + 1) * blk)
        work.append((0, lambda rows=rows: q1[rows, :], lambda kr=kr: k1[kr, :],
                     lambda kr=kr: v1[kr, :], rows))
        r4, n = divmod(i, per4)
        ln = slice(r4 * LANES, (r4 + 1) * LANES)
        qr = slice(n * blk, (n + 1) * blk)
        kr = slice(max(n - 1, 0) * blk, (n + 1) * blk)
        work.append((1, lambda qr=qr, ln=ln: q4[qr, ln], lambda kr=kr, ln=ln: k4[kr, ln],
                     lambda kr=kr, ln=ln: v4[kr, ln], rows))
        ln = slice(i * LANES, (i + 1) * LANES)
        dst = pl.ds((i % 4) * (seq // 4) + i // 4, blk, stride=4)
        work.append((2, lambda ln=ln: q16[:, ln], lambda ln=ln: k16[:, ln],
                     lambda ln=ln: v16[:, ln], dst))

    pending = {}
    for t in range(len(work) + ATTN_LOOKAHEAD):
        if t < len(work):
            pi, q_at, k_at, _, _ = work[t]
            pending[t] = scores(pi, q_at, k_at)
        if t >= ATTN_LOOKAHEAD:
            pi, _, _, v_at, dst = work[t - ATTN_LOOKAHEAD]
            finish(pending.pop(t - ATTN_LOOKAHEAD), pi, v_at, dst)

    for c in range(nblk):
        r4, n = divmod(c, per4)
        rows = slice(c * blk, (c + 1) * blk)
        nat = pl.ds(r4 + 4 * blk * n, blk, stride=4)
        at = (nat, rows, rows)
        ms = [macc[pi, at[pi], :] for pi in range(3)]
        mx = jnp.maximum(jnp.maximum(ms[0], ms[1]), ms[2])
        es = [jnp.exp2(m - mx) for m in ms]
        den = es[0] * lacc[0, nat, :] + es[1] * lacc[1, rows, :] + es[2] * lacc[2, rows, :]
        num = es[0] * oacc[0, nat, :] + es[1] * oacc[1, rows, :] + es[2] * oacc[2, rows, :]
        onat[nat, :] = num * (1.0 / den)
    o_ref[...] = onat[...].astype(o_ref.dtype)


def _attention(nat, mod4, mod16, slopes, *, batch, seq):
    h = N_ATTN_HEADS
    specs = [pl.BlockSpec((None, seq, LANES), lambda b, hh, t=t: (t * h + hh, b, 0))
             for t in range(3)]
    for d in DILATIONS[1:]:
        specs += [pl.BlockSpec((None, None, seq // d, d * LANES),
                               lambda b, hh, t=t: (t * h + hh, b, 0, 0)) for t in range(3)]
    return pl.pallas_call(
        functools.partial(_attn_kernel, seq=seq),
        grid=(batch, h),
        in_specs=[pl.BlockSpec(memory_space=pltpu.SMEM)] + specs,
        out_specs=pl.BlockSpec((None, seq, LANES), lambda b, hh: (hh, b, 0)),
        out_shape=jax.ShapeDtypeStruct((h, batch * seq, LANES), BF16),
        scratch_shapes=[
            pltpu.VMEM((3, ATTN_BLK, 2 * ATTN_BLK), F32),
            pltpu.VMEM((3, seq, LANES), F32),
            pltpu.VMEM((3, seq, LANES), F32),
            pltpu.VMEM((3, seq, LANES), F32),
            pltpu.VMEM((seq, LANES), F32),
        ],
        compiler_params=pltpu.CompilerParams(
            dimension_semantics=("parallel", "parallel"),
            vmem_limit_bytes=VMEM_LIMIT),
        name="dilated_attention",
    )(slopes, nat, nat, nat, mod4, mod4, mod4, mod16, mod16, mod16)


def _ret_kernel(lg_ref, q_ref, k_ref, v_ref, g_ref, o_ref, decay, zeta, xi, states, *, seq):
    c = RET_CHUNK
    dh = RET_HEAD_DIM
    nc = seq // c
    lg = lg_ref[pl.program_id(1)]
    k_scale = 1.0 / math.sqrt(dh)
    assert math.log2(k_scale).is_integer()

    ii = lax.broadcasted_iota(jnp.int32, (c, c), 0)
    jj = lax.broadcasted_iota(jnp.int32, (c, c), 1)
    dif = (ii - jj).astype(F32)
    decay[...] = jnp.where(dif >= 0, jnp.exp(lg * jnp.maximum(dif, 0.0)), 0.0) * k_scale
    idx = lax.broadcasted_iota(jnp.int32, (c, dh), 0).astype(F32)
    zeta[...] = jnp.exp(lg * (c - 1.0 - idx)) * k_scale
    xi[...] = jnp.exp(lg * (idx + 1.0))
    gamma_chunk = jnp.exp(jnp.full((dh, dh), lg * c, F32))

    def wide(ref, n):
        rows = slice(n * c, (n + 1) * c)
        return jnp.concatenate([ref[0, rows, :], ref[1, rows, :]], axis=-1)

    st = jnp.zeros((dh, dh), F32)
    for n in range(nc):
        states[n] = st.astype(BF16)
        if n + 1 < nc:
            kz_t = (wide(k_ref, n).astype(F32) * zeta[...]).T.astype(BF16)
            st = st * gamma_chunk + jnp.dot(kz_t, wide(v_ref, n), preferred_element_type=F32)

    def front(n):
        qn = wide(q_ref, n)
        sc = lax.dot_general(qn, wide(k_ref, n), (((1,), (1,)), ((), ())),
                             preferred_element_type=F32) * decay[...]
        cross = jnp.dot(qn, states[n], preferred_element_type=F32) * xi[...]
        return sc.astype(BF16), cross

    def back(n, sc, cross):
        ret = jnp.dot(sc, wide(v_ref, n), preferred_element_type=F32) + cross
        ret = ret * lax.rsqrt(jnp.mean(ret * ret, axis=-1, keepdims=True) + NORM_EPS)
        gate = wide(g_ref, n).astype(F32)
        out = (gate * jax.nn.sigmoid(gate) * ret).astype(o_ref.dtype)
        o_ref[0, n * c:(n + 1) * c, :] = out[:, :LANES]
        o_ref[1, n * c:(n + 1) * c, :] = out[:, LANES:]

    pending = {}
    for t in range(nc + RET_LOOKAHEAD):
        if t < nc:
            pending[t] = front(t)
        if t >= RET_LOOKAHEAD:
            back(t - RET_LOOKAHEAD, *pending.pop(t - RET_LOOKAHEAD))


def _retention(proj_hm, log_gamma, *, batch, seq):
    h = N_RET_HEADS
    blk = (2, seq, LANES)
    return pl.pallas_call(
        functools.partial(_ret_kernel, seq=seq),
        grid=(batch, h),
        in_specs=[
            pl.BlockSpec(memory_space=pltpu.SMEM),
            pl.BlockSpec(blk, lambda b, hh: (hh, b, 0)),
            pl.BlockSpec(blk, lambda b, hh: (h + hh, b, 0)),
            pl.BlockSpec(blk, lambda b, hh: (2 * h + hh, b, 0)),
            pl.BlockSpec(blk, lambda b, hh: (3 * h + hh, b, 0)),
        ],
        out_specs=pl.BlockSpec(blk, lambda b, hh: (hh, b, 0)),
        out_shape=jax.ShapeDtypeStruct((2 * h, batch * seq, LANES), BF16),
        scratch_shapes=[
            pltpu.VMEM((RET_CHUNK, RET_CHUNK), F32),
            pltpu.VMEM((RET_CHUNK, RET_HEAD_DIM), F32),
            pltpu.VMEM((RET_CHUNK, RET_HEAD_DIM), F32),
            pltpu.VMEM((seq // RET_CHUNK, RET_HEAD_DIM, RET_HEAD_DIM), BF16),
        ],
        compiler_params=pltpu.CompilerParams(
            dimension_semantics=("parallel", "parallel"),
            vmem_limit_bytes=VMEM_LIMIT),
        name="retention",
    )(log_gamma, proj_hm, proj_hm, proj_hm, proj_hm)


def _outproj_kernel(x_ref, a_ref, r_ref, wa_ref, wr_ref, o_ref):
    na = a_ref.shape[0]
    nr = r_ref.shape[0]
    a = jnp.concatenate([a_ref[i] for i in range(na)], axis=-1)
    r = jnp.concatenate([r_ref[i] for i in range(nr)], axis=-1)
    y = jnp.dot(a, wa_ref[...], preferred_element_type=F32)
    y = y + jnp.dot(r, wr_ref[...], preferred_element_type=F32)
    o_ref[...] = x_ref[...] + y


def _out_proj(x2d, attn_hm, ret_hm, w_out_bf16, *, tm=512):
    m, d = x2d.shape
    na, nr = attn_hm.shape[0], ret_hm.shape[0]
    assert na == nr
    return pl.pallas_call(
        _outproj_kernel,
        grid=(m // tm,),
        in_specs=[
            pl.BlockSpec((tm, d), lambda i: (i, 0)),
            pl.BlockSpec((na, tm, LANES), lambda i: (0, i, 0)),
            pl.BlockSpec((nr, tm, LANES), lambda i: (0, i, 0)),
            pl.BlockSpec((na * LANES, d), lambda i: (0, 0)),
            pl.BlockSpec((nr * LANES, d), lambda i: (1, 0)),
        ],
        out_specs=pl.BlockSpec((tm, d), lambda i: (i, 0)),
        out_shape=jax.ShapeDtypeStruct((m, d), F32),
        compiler_params=pltpu.CompilerParams(
            dimension_semantics=("parallel",),
            vmem_limit_bytes=VMEM_LIMIT),
        name="out_proj",
    )(x2d, attn_hm, ret_hm, w_out_bf16, w_out_bf16)


def _ffn_kernel(x_ref, nw_ref, wg_ref, wu_ref, wd_ref, fw_ref, o_ref, h_ref, acc_ref,
                *, tm, final_norm):
    f = pl.program_id(1)

    @pl.when(f == 0)
    def _():
        _norm_into(x_ref, nw_ref, h_ref, tm, zero_ref=acc_ref)

    h = h_ref[...]
    g = jnp.dot(h, wg_ref[...], preferred_element_type=F32)
    u = jnp.dot(h, wu_ref[...], preferred_element_type=F32)
    a = (g * jax.nn.sigmoid(g) * u).astype(BF16)
    y = jnp.dot(a, wd_ref[...], preferred_element_type=F32)
    acc_ref[...] += y

    @pl.when(f == pl.num_programs(1) - 1)
    def _():
        def body(i, c):
            r = pl.multiple_of(i * NORM_ROWS, NORM_ROWS)
            x2 = x_ref[pl.ds(r, NORM_ROWS), :] + acc_ref[pl.ds(r, NORM_ROWS), :]
            if final_norm:
                x2 = _rms_rows(x2, fw_ref[...])
            o_ref[pl.ds(r, NORM_ROWS), :] = x2
            return c
        lax.fori_loop(0, tm // NORM_ROWS, body, 0)


def _ffn(x2d, norm_w, wg, wu, wd, final_w, *, final_norm, tm=512, tf=512):
    m, d = x2d.shape
    hid = wg.shape[1]
    return pl.pallas_call(
        functools.partial(_ffn_kernel, tm=tm, final_norm=final_norm),
        grid=(m // tm, hid // tf),
        in_specs=[
            pl.BlockSpec((tm, d), lambda i, f: (i, 0)),
            pl.BlockSpec((1, d), lambda i, f: (0, 0)),
            pl.BlockSpec((d, tf), lambda i, f: (0, f)),
            pl.BlockSpec((d, tf), lambda i, f: (0, f)),
            pl.BlockSpec((tf, d), lambda i, f: (f, 0)),
            pl.BlockSpec((1, d), lambda i, f: (0, 0)),
        ],
        out_specs=pl.BlockSpec((tm, d), lambda i, f: (i, 0)),
        out_shape=jax.ShapeDtypeStruct((m, d), F32),
        scratch_shapes=[pltpu.VMEM((tm, d), BF16), pltpu.VMEM((tm, d), F32)],
        compiler_params=pltpu.CompilerParams(
            dimension_semantics=("parallel", "arbitrary"),
            vmem_limit_bytes=VMEM_LIMIT),
        name="ffn",
    )(x2d, norm_w.reshape(1, d), wg, wu, wd, final_w.reshape(1, d))


def kernel(x, norm_mix_w, w_in, w_out, norm_ffn_w, w_gate, w_up, w_down, norm_final_w):
    batch, seq, d = x.shape
    depth = w_in.shape[0]
    assert seq == DILATIONS[-1] * ATTN_BLK and seq % RET_CHUNK == 0
    slopes = jnp.exp2(-8.0 * jnp.arange(1, N_ATTN_HEADS + 1, dtype=F32) / N_ATTN_HEADS)
    log_gamma = jnp.log(1.0 - jnp.exp2(-5.0 - jnp.arange(N_RET_HEADS, dtype=F32)))

    xs = x.reshape(batch * seq, d)
    for layer in range(depth):
        w_in_l = w_in[layer].astype(BF16)
        nat, mod4, mod16, proj_ret = _in_proj(xs, norm_mix_w[layer], w_in_l,
                                              batch=batch, seq=seq)
        attn = _attention(nat, mod4, mod16, slopes, batch=batch, seq=seq)
        ret = _retention(proj_ret, log_gamma, batch=batch, seq=seq)
        xs = _out_proj(xs, attn, ret, w_out[layer].astype(BF16))
        xs = _ffn(xs, norm_ffn_w[layer], w_gate[layer].astype(BF16),
                  w_up[layer].astype(BF16), w_down[layer].astype(BF16), norm_final_w,
                  final_norm=(layer == depth - 1))
    return xs.reshape(batch, seq, d)
```

```python
import functools
import math

import jax
import jax.numpy as jnp
from jax import lax
from jax.experimental import pallas as pl
from jax.experimental.pallas import tpu as pltpu

F32 = jnp.float32
BF16 = jnp.bfloat16

LANES = 128
ATTN_HEAD_DIM = 128
N_ATTN_HEADS = 8
RET_HEAD_DIM = 256
N_RET_HEADS = 4
RET_CHUNK = 128
ATTN_BLK = 128
DILATIONS = (1, 4, 16)
ATTN_LOOKAHEAD = 8
RET_LOOKAHEAD = 2
NORM_EPS = 1e-6
MASK_VALUE = -1e30
LOG2E = math.log2(math.e)
QK_SCALE = LOG2E / math.sqrt(ATTN_HEAD_DIM)
NORM_ROWS = 128
VMEM_LIMIT = 48 * 1024 * 1024


def _rms_rows(x, w):
    ms = jnp.mean(x * x, axis=-1, keepdims=True)
    return x * lax.rsqrt(ms + NORM_EPS) * w


def _inproj_kernel(x_ref, nw_ref, w_ref, o1_ref, o4_ref, o16_ref, ret_ref, h_even, h_odd,
                   nat, mod4, *, tm, tn, attn_tiles, q_tiles):
    i = pl.program_id(0)
    j = pl.program_id(1)

    def norm_chunk(h_ref):
        xr = x_ref.shape[0]
        r = pl.multiple_of(jnp.minimum(j, tm // xr - 1) * xr, xr)
        h_ref[pl.ds(r, xr), :] = _rms_rows(x_ref[...], nw_ref[...]).astype(BF16)

    @pl.when(i == 0)
    def _():
        norm_chunk(h_even)

    for parity, (h_wr, h_rd) in enumerate(((h_even, h_odd), (h_odd, h_even))):
        mine = (i > 0) & (i % 2 == parity)

        @pl.when(mine & (j < attn_tiles))
        def _(h_wr=h_wr, h_rd=h_rd):
            norm_chunk(h_wr)
            _inproj_attn_step(h_rd, w_ref, o1_ref, o4_ref, o16_ref, nat, mod4,
                              jnp.where(j < q_tiles, QK_SCALE, 1.0), tm, tn)

        @pl.when(mine & (j >= attn_tiles))
        def _(h_wr=h_wr, h_rd=h_rd):
            norm_chunk(h_wr)
            acc = jnp.dot(h_rd[...], w_ref[...], preferred_element_type=F32)
            for c in range(tn // LANES):
                ret_ref[c] = acc[:, c * LANES:(c + 1) * LANES].astype(ret_ref.dtype)


def _inproj_attn_step(h_ref, w_ref, o1_ref, o4_ref, o16_ref, nat, mod4, scale, tm, tn):
    acc = jnp.dot(h_ref[...], w_ref[...], preferred_element_type=F32) * scale
    n4, n16 = tm // 4, tm // 16
    for c in range(tn // LANES):
        slab = acc[:, c * LANES:(c + 1) * LANES]
        o1_ref[c] = slab.astype(o1_ref.dtype)
        nat[c] = slab
        for r in range(4):
            rows = nat[c, pl.ds(r, n4, stride=4), :]
            mod4[c, r * n4:(r + 1) * n4, :] = rows
            o4_ref[c, :, r * LANES:(r + 1) * LANES] = rows.astype(o4_ref.dtype)
        for r in range(16):
            a, r4 = divmod(r, 4)
            rows = mod4[c, pl.ds(r4 * n4 + a, n16, stride=4), :]
            o16_ref[c, :, r * LANES:(r + 1) * LANES] = rows.astype(o16_ref.dtype)


def _in_proj(x2d, norm_w, w_bf16, *, batch, seq, tm=1024, tn=1024):
    m, d = x2d.shape
    n = w_bf16.shape[1]
    n_attn = 3 * N_ATTN_HEADS * ATTN_HEAD_DIM
    ns = tn // LANES
    tiles = m // tm
    nj = n // tn
    attn_tiles = n_attn // tn
    chunks = tm // NORM_ROWS
    while chunks > nj:
        chunks //= 2
    xr = tm // chunks
    per_b = seq // tm
    q_tiles = N_ATTN_HEADS * ATTN_HEAD_DIM // tn

    def t_of(i):
        return jnp.maximum(i - 1, 0)

    def ja(i, j):
        return jnp.where(i == 0, 0, jnp.minimum(j, attn_tiles - 1))

    def jr(i, j):
        return jnp.where(i == 0, 0, jnp.maximum(j - attn_tiles, 0))

    return pl.pallas_call(
        functools.partial(_inproj_kernel, tm=tm, tn=tn, attn_tiles=attn_tiles, q_tiles=q_tiles),
        grid=(tiles + 1, nj),
        in_specs=[
            pl.BlockSpec((xr, d), lambda i, j: (
                jnp.minimum(i, tiles - 1) * chunks + jnp.minimum(j, chunks - 1), 0)),
            pl.BlockSpec((1, d), lambda i, j: (0, 0)),
            pl.BlockSpec((d, tn), lambda i, j: (0, jnp.where(i == 0, 0, j))),
        ],
        out_specs=[
            pl.BlockSpec((ns, tm, LANES), lambda i, j: (ja(i, j), t_of(i), 0)),
            pl.BlockSpec((ns, None, tm // 4, 4 * LANES),
                         lambda i, j: (ja(i, j), t_of(i) // per_b, t_of(i) % per_b, 0)),
            pl.BlockSpec((ns, None, tm // 16, 16 * LANES),
                         lambda i, j: (ja(i, j), t_of(i) // per_b, t_of(i) % per_b, 0)),
            pl.BlockSpec((ns, tm, LANES), lambda i, j: (jr(i, j), t_of(i), 0)),
        ],
        out_shape=[
            jax.ShapeDtypeStruct((n_attn // LANES, m, LANES), BF16),
            jax.ShapeDtypeStruct((n_attn // LANES, batch, seq // 4, 4 * LANES), BF16),
            jax.ShapeDtypeStruct((n_attn // LANES, batch, seq // 16, 16 * LANES), BF16),
            jax.ShapeDtypeStruct(((n - n_attn) // LANES, m, LANES), BF16),
        ],
        scratch_shapes=[pltpu.VMEM((tm, d), BF16),
                        pltpu.VMEM((tm, d), BF16),
                        pltpu.VMEM((ns, tm, LANES), F32),
                        pltpu.VMEM((ns, tm, LANES), F32)],
        compiler_params=pltpu.CompilerParams(
            dimension_semantics=("arbitrary", "arbitrary"),
            vmem_limit_bytes=VMEM_LIMIT),
        name="in_proj",
    )(x2d, norm_w.reshape(1, d), w_bf16)


def _attn_kernel(slopes_ref, q1, k1, v1, q4, k4, v4, q16, k16, v16, o_ref,
                 bm, macc, lacc, oacc, onat, *, seq):
    blk = ATTN_BLK
    nblk = seq // blk
    per4 = nblk // 4
    slope = slopes_ref[pl.program_id(1)]

    qi = lax.broadcasted_iota(jnp.int32, (blk, 2 * blk), 0)
    kj = lax.broadcasted_iota(jnp.int32, (blk, 2 * blk), 1)
    diff = qi - kj + blk
    valid = (diff >= 0) & (diff <= blk)
    dist = diff.astype(F32)
    for pi, dil in enumerate(DILATIONS):
        bm[pi] = jnp.where(valid, (-LOG2E * slope) * (dist * float(dil)), MASK_VALUE)

    def scores(pi, q_at, k_at):
        kk = k_at()
        bias = bm[pi] if kk.shape[0] == 2 * blk else bm[pi, :, blk:]
        return lax.dot_general(q_at(), kk, (((1,), (1,)), ((), ())),
                               preferred_element_type=F32) + bias

    def finish(s, pi, v_at, dst):
        mx = jnp.max(s, axis=-1, keepdims=True)
        p = jnp.exp2(s - mx).astype(BF16)
        vv = v_at()
        acc = jnp.dot(p, jnp.concatenate([vv, jnp.ones_like(vv)], axis=-1),
                      preferred_element_type=F32)
        oacc[pi, dst, :] = acc[:, :LANES]
        lacc[pi, dst, :] = acc[:, LANES:]
        macc[pi, dst, :] = jnp.broadcast_to(mx, (blk, LANES))

    def p0_block(i):
        rows = slice(i * blk, (i + 1) * blk)
        kr = slice(max(i - 1, 0) * blk, (i + 1) * blk)
        return (0, lambda: q1[rows, :], lambda: k1[kr, :], lambda: v1[kr, :], rows)

    def p1_block(r4, n):
        ln = slice(r4 * LANES, (r4 + 1) * LANES)
        qr = slice(n * blk, (n + 1) * blk)
        kr = slice(max(n - 1, 0) * blk, (n + 1) * blk)
        dst = slice((r4 * per4 + n) * blk, (r4 * per4 + n + 1) * blk)
        return (1, lambda: q4[qr, ln], lambda: k4[kr, ln], lambda: v4[kr, ln], dst)

    def p2_block(r):
        ln = slice(r * LANES, (r + 1) * LANES)
        dst = pl.ds((r % 4) * (seq // 4) + r // 4, blk, stride=4)
        return (2, lambda: q16[:, ln], lambda: k16[:, ln], lambda: v16[:, ln], dst)

    def combine(n):
        for r4 in range(4):
            rows = slice((r4 * per4 + n) * blk, (r4 * per4 + n + 1) * blk)
            nat = pl.ds(r4 + 4 * blk * n, blk, stride=4)
            at = (nat, rows, rows)
            ms = [macc[pi, at[pi], :] for pi in range(3)]
            mx = jnp.maximum(jnp.maximum(ms[0], ms[1]), ms[2])
            es = [jnp.exp2(m - mx) for m in ms]
            den = es[0] * lacc[0, nat, :] + es[1] * lacc[1, rows, :] + es[2] * lacc[2, rows, :]
            num = es[0] * oacc[0, nat, :] + es[1] * oacc[1, rows, :] + es[2] * oacc[2, rows, :]
            onat[nat, :] = num * (1.0 / den)
        done = slice(4 * blk * n, 4 * blk * (n + 1))
        o_ref[done, :] = onat[done, :].astype(o_ref.dtype)

    work = [p2_block(r) for r in range(nblk)]
    after = {}
    for n in range(per4):
        work += [p0_block(4 * n + a) for a in range(4)] + [p1_block(r4, n) for r4 in range(4)]
        after[len(work) - 1] = n

    pending = {}
    for t in range(len(work) + ATTN_LOOKAHEAD):
        if t < len(work):
            pi, q_at, k_at, _, _ = work[t]
            pending[t] = scores(pi, q_at, k_at)
        b = t - ATTN_LOOKAHEAD
        if b >= 0:
            pi, _, _, v_at, dst = work[b]
            finish(pending.pop(b), pi, v_at, dst)
            if b in after:
                combine(after[b])


def _attention(nat, mod4, mod16, slopes, *, batch, seq):
    h = N_ATTN_HEADS
    specs = [pl.BlockSpec((None, seq, LANES), lambda b, hh, t=t: (t * h + hh, b, 0))
             for t in range(3)]
    for d in DILATIONS[1:]:
        specs += [pl.BlockSpec((None, None, seq // d, d * LANES),
                               lambda b, hh, t=t: (t * h + hh, b, 0, 0)) for t in range(3)]
    return pl.pallas_call(
        functools.partial(_attn_kernel, seq=seq),
        grid=(batch, h),
        in_specs=[pl.BlockSpec(memory_space=pltpu.SMEM)] + specs,
        out_specs=pl.BlockSpec((None, seq, LANES), lambda b, hh: (hh, b, 0)),
        out_shape=jax.ShapeDtypeStruct((h, batch * seq, LANES), BF16),
        scratch_shapes=[
            pltpu.VMEM((3, ATTN_BLK, 2 * ATTN_BLK), F32),
            pltpu.VMEM((3, seq, LANES), F32),
            pltpu.VMEM((3, seq, LANES), F32),
            pltpu.VMEM((3, seq, LANES), F32),
            pltpu.VMEM((seq, LANES), F32),
        ],
        compiler_params=pltpu.CompilerParams(
            dimension_semantics=("parallel", "parallel"),
            vmem_limit_bytes=VMEM_LIMIT),
        name="dilated_attention",
    )(slopes, nat, nat, nat, mod4, mod4, mod4, mod16, mod16, mod16)


def _ret_kernel(lg_ref, q_ref, k_ref, v_ref, g_ref, o_ref, decay, zeta, xi, states, *, seq):
    c = RET_CHUNK
    dh = RET_HEAD_DIM
    nc = seq // c
    lg = lg_ref[pl.program_id(1)]
    k_scale = 1.0 / math.sqrt(dh)
    assert math.log2(k_scale).is_integer()

    ii = lax.broadcasted_iota(jnp.int32, (c, c), 0)
    jj = lax.broadcasted_iota(jnp.int32, (c, c), 1)
    dif = (ii - jj).astype(F32)
    decay[...] = jnp.where(dif >= 0, jnp.exp(lg * jnp.maximum(dif, 0.0)), 0.0) * k_scale
    idx = lax.broadcasted_iota(jnp.int32, (c, dh), 0).astype(F32)
    zeta[...] = jnp.exp(lg * (c - 1.0 - idx)) * k_scale
    xi[...] = jnp.exp(lg * (idx + 1.0))
    gamma_chunk = jnp.exp(jnp.full((dh, dh), lg * c, F32))

    def wide(ref, n):
        rows = slice(n * c, (n + 1) * c)
        return jnp.concatenate([ref[0, rows, :], ref[1, rows, :]], axis=-1)

    st = jnp.zeros((dh, dh), F32)
    for n in range(nc):
        states[n] = st.astype(BF16)
        if n + 1 < nc:
            kz_t = (wide(k_ref, n).astype(F32) * zeta[...]).T.astype(BF16)
            st = st * gamma_chunk + jnp.dot(kz_t, wide(v_ref, n), preferred_element_type=F32)

    def front(n):
        qn = wide(q_ref, n)
        sc = lax.dot_general(qn, wide(k_ref, n), (((1,), (1,)), ((), ())),
                             preferred_element_type=F32) * decay[...]
        cross = jnp.dot(qn, states[n], preferred_element_type=F32) * xi[...]
        return sc.astype(BF16), cross

    def back(n, sc, cross):
        ret = jnp.dot(sc, wide(v_ref, n), preferred_element_type=F32) + cross
        ret = ret * lax.rsqrt(jnp.mean(ret * ret, axis=-1, keepdims=True) + NORM_EPS)
        gate = wide(g_ref, n).astype(F32)
        out = (gate * jax.nn.sigmoid(gate) * ret).astype(o_ref.dtype)
        o_ref[0, n * c:(n + 1) * c, :] = out[:, :LANES]
        o_ref[1, n * c:(n + 1) * c, :] = out[:, LANES:]

    pending = {}
    for t in range(nc + RET_LOOKAHEAD):
        if t < nc:
            pending[t] = front(t)
        if t >= RET_LOOKAHEAD:
            back(t - RET_LOOKAHEAD, *pending.pop(t - RET_LOOKAHEAD))


def _retention(proj_hm, log_gamma, *, batch, seq):
    h = N_RET_HEADS
    blk = (2, seq, LANES)
    return pl.pallas_call(
        functools.partial(_ret_kernel, seq=seq),
        grid=(batch, h),
        in_specs=[
            pl.BlockSpec(memory_space=pltpu.SMEM),
            pl.BlockSpec(blk, lambda b, hh: (hh, b, 0)),
            pl.BlockSpec(blk, lambda b, hh: (h + hh, b, 0)),
            pl.BlockSpec(blk, lambda b, hh: (2 * h + hh, b, 0)),
            pl.BlockSpec(blk, lambda b, hh: (3 * h + hh, b, 0)),
        ],
        out_specs=pl.BlockSpec(blk, lambda b, hh: (hh, b, 0)),
        out_shape=jax.ShapeDtypeStruct((2 * h, batch * seq, LANES), BF16),
        scratch_shapes=[
            pltpu.VMEM((RET_CHUNK, RET_CHUNK), F32),
            pltpu.VMEM((RET_CHUNK, RET_HEAD_DIM), F32),
            pltpu.VMEM((RET_CHUNK, RET_HEAD_DIM), F32),
            pltpu.VMEM((seq // RET_CHUNK, RET_HEAD_DIM, RET_HEAD_DIM), BF16),
        ],
        compiler_params=pltpu.CompilerParams(
            dimension_semantics=("parallel", "parallel"),
            vmem_limit_bytes=VMEM_LIMIT),
        name="retention",
    )(log_gamma, proj_hm, proj_hm, proj_hm, proj_hm)


def _outproj_kernel(x_ref, a_ref, r_ref, wa_ref, wr_ref, o_ref):
    na = a_ref.shape[0]
    nr = r_ref.shape[0]
    a = jnp.concatenate([a_ref[i] for i in range(na)], axis=-1)
    r = jnp.concatenate([r_ref[i] for i in range(nr)], axis=-1)
    y = jnp.dot(a, wa_ref[...], preferred_element_type=F32)
    y = y + jnp.dot(r, wr_ref[...], preferred_element_type=F32)
    o_ref[...] = x_ref[...] + y


def _out_proj(x2d, attn_hm, ret_hm, w_out_bf16, *, tm=512):
    m, d = x2d.shape
    na, nr = attn_hm.shape[0], ret_hm.shape[0]
    assert na == nr
    return pl.pallas_call(
        _outproj_kernel,
        grid=(m // tm,),
        in_specs=[
            pl.BlockSpec((tm, d), lambda i: (i, 0)),
            pl.BlockSpec((na, tm, LANES), lambda i: (0, i, 0)),
            pl.BlockSpec((nr, tm, LANES), lambda i: (0, i, 0)),
            pl.BlockSpec((na * LANES, d), lambda i: (0, 0)),
            pl.BlockSpec((nr * LANES, d), lambda i: (1, 0)),
        ],
        out_specs=pl.BlockSpec((tm, d), lambda i: (i, 0)),
        out_shape=jax.ShapeDtypeStruct((m, d), F32),
        compiler_params=pltpu.CompilerParams(
            dimension_semantics=("parallel",),
            vmem_limit_bytes=VMEM_LIMIT),
        name="out_proj",
    )(x2d, attn_hm, ret_hm, w_out_bf16, w_out_bf16)


def _ffn_kernel(x_ref, nw_ref, wg_ref, wu_ref, wd_ref, fw_ref, o_ref,
                h_even, h_odd, xs_even, xs_odd, acc_ref, *, tm, final_norm):
    i = pl.program_id(0)
    f = pl.program_id(1)
    xr = x_ref.shape[0]

    def stage_chunk(h_ref, xs_ref):
        r = pl.multiple_of(jnp.minimum(f, tm // xr - 1) * xr, xr)
        x = x_ref[...]
        xs_ref[pl.ds(r, xr), :] = x
        h_ref[pl.ds(r, xr), :] = _rms_rows(x, nw_ref[...]).astype(BF16)

    @pl.when(i == 0)
    def _():
        stage_chunk(h_even, xs_even)

    def step(h_wr, xs_wr, h_rd, first):
        stage_chunk(h_wr, xs_wr)
        h = h_rd[...]
        g = jnp.dot(h, wg_ref[...], preferred_element_type=F32)
        u = jnp.dot(h, wu_ref[...], preferred_element_type=F32)
        a = (g * jax.nn.sigmoid(g) * u).astype(BF16)
        y = jnp.dot(a, wd_ref[...], preferred_element_type=F32)
        acc_ref[...] = y if first else acc_ref[...] + y

    for parity, (h_wr, xs_wr, h_rd, xs_rd) in enumerate(
            ((h_even, xs_even, h_odd, xs_odd), (h_odd, xs_odd, h_even, xs_even))):
        mine = (i > 0) & (i % 2 == parity)
        pl.when(mine & (f == 0))(functools.partial(step, h_wr, xs_wr, h_rd, True))
        pl.when(mine & (f > 0))(functools.partial(step, h_wr, xs_wr, h_rd, False))

        @pl.when(mine & (f == pl.num_programs(1) - 1))
        def _(xs_rd=xs_rd):
            def body(c, carry):
                r = pl.multiple_of(c * NORM_ROWS, NORM_ROWS)
                x2 = xs_rd[pl.ds(r, NORM_ROWS), :] + acc_ref[pl.ds(r, NORM_ROWS), :]
                if final_norm:
                    x2 = _rms_rows(x2, fw_ref[...])
                o_ref[pl.ds(r, NORM_ROWS), :] = x2
                return carry
            lax.fori_loop(0, tm // NORM_ROWS, body, 0)


def _ffn(x2d, norm_w, wg, wu, wd, final_w, *, final_norm, tm=512, tf=512):
    m, d = x2d.shape
    hid = wg.shape[1]
    tiles = m // tm
    nf = hid // tf
    chunks = tm // NORM_ROWS
    while chunks > nf:
        chunks //= 2
    xr = tm // chunks

    def fw(i, f):
        return jnp.where(i == 0, 0, f)

    return pl.pallas_call(
        functools.partial(_ffn_kernel, tm=tm, final_norm=final_norm),
        grid=(tiles + 1, nf),
        in_specs=[
            pl.BlockSpec((xr, d), lambda i, f: (
                jnp.minimum(i, tiles - 1) * chunks + jnp.minimum(f, chunks - 1), 0)),
            pl.BlockSpec((1, d), lambda i, f: (0, 0)),
            pl.BlockSpec((d, tf), lambda i, f: (0, fw(i, f))),
            pl.BlockSpec((d, tf), lambda i, f: (0, fw(i, f))),
            pl.BlockSpec((tf, d), lambda i, f: (fw(i, f), 0)),
            pl.BlockSpec((1, d), lambda i, f: (0, 0)),
        ],
        out_specs=pl.BlockSpec((tm, d), lambda i, f: (jnp.maximum(i - 1, 0), 0)),
        out_shape=jax.ShapeDtypeStruct((m, d), F32),
        scratch_shapes=[pltpu.VMEM((tm, d), BF16), pltpu.VMEM((tm, d), BF16),
                        pltpu.VMEM((tm, d), F32), pltpu.VMEM((tm, d), F32),
                        pltpu.VMEM((tm, d), F32)],
        compiler_params=pltpu.CompilerParams(
            dimension_semantics=("arbitrary", "arbitrary"),
            vmem_limit_bytes=VMEM_LIMIT),
        name="ffn",
    )(x2d, norm_w.reshape(1, d), wg, wu, wd, final_w.reshape(1, d))


def kernel(x, norm_mix_w, w_in, w_out, norm_ffn_w, w_gate, w_up, w_down, norm_final_w):
    batch, seq, d = x.shape
    depth = w_in.shape[0]
    assert seq == DILATIONS[-1] * ATTN_BLK and seq % RET_CHUNK == 0
    slopes = jnp.exp2(-8.0 * jnp.arange(1, N_ATTN_HEADS + 1, dtype=F32) / N_ATTN_HEADS)
    log_gamma = jnp.log(1.0 - jnp.exp2(-5.0 - jnp.arange(N_RET_HEADS, dtype=F32)))

    xs = x.reshape(batch * seq, d)
    for layer in range(depth):
        w_in_l = w_in[layer].astype(BF16)
        nat, mod4, mod16, proj_ret = _in_proj(xs, norm_mix_w[layer], w_in_l,
                                              batch=batch, seq=seq)
        attn = _attention(nat, mod4, mod16, slopes, batch=batch, seq=seq)
        ret = _retention(proj_ret, log_gamma, batch=batch, seq=seq)
        xs = _out_proj(xs, attn, ret, w_out[layer].astype(BF16))
        xs = _ffn(xs, norm_ffn_w[layer], w_gate[layer].astype(BF16),
                  w_up[layer].astype(BF16), w_down[layer].astype(BF16), norm_final_w,
                  final_norm=(layer == depth - 1))
    return xs.reshape(batch, seq, d)
```

```python
import functools
import math

import jax
import jax.numpy as jnp
from jax import lax
from jax.experimental import pallas as pl
from jax.experimental.pallas import tpu as pltpu

F32 = jnp.float32
BF16 = jnp.bfloat16

LANES = 128
ATTN_HEAD_DIM = 128
N_ATTN_HEADS = 8
RET_HEAD_DIM = 256
N_RET_HEADS = 4
RET_CHUNK = 128
ATTN_BLK = 128
DILATIONS = (1, 4, 16)
ATTN_LOOKAHEAD = 8
RET_LOOKAHEAD = 2
NORM_EPS = 1e-6
MASK_VALUE = -1e30
LOG2E = math.log2(math.e)
QK_SCALE = LOG2E / math.sqrt(ATTN_HEAD_DIM)
NORM_ROWS = 128
VMEM_LIMIT = 48 * 1024 * 1024


def _rms_rows(x, w):
    ms = jnp.mean(x * x, axis=-1, keepdims=True)
    return x * lax.rsqrt(ms + NORM_EPS) * w


def _inproj_kernel(x_ref, nw_ref, w_ref, o1_ref, o4_ref, o16_ref, ret_ref, h_even, h_odd,
                   nat, mod4, *, tm, tn, attn_tiles, q_tiles):
    i = pl.program_id(0)
    j = pl.program_id(1)
    xr = x_ref.shape[0]
    n4, n16 = tm // 4, tm // 16

    def norm_chunk(h_ref):
        r = pl.multiple_of(jnp.minimum(j, tm // xr - 1) * xr, xr)
        h_ref[pl.ds(r, xr), :] = _rms_rows(x_ref[...], nw_ref[...]).astype(BF16)

    def step(h_wr, h_rd, store):
        norm_chunk(h_wr)
        acc = jnp.dot(h_rd[...], w_ref[...], preferred_element_type=F32)
        for c in range(tn // LANES):
            store(c, acc[:, c * LANES:(c + 1) * LANES])

    @pl.when(i == 0)
    def _():
        norm_chunk(h_even)

    def store_attn(c, slab):
        slab = slab * jnp.where(j < q_tiles, QK_SCALE, 1.0)
        o1_ref[c] = slab.astype(o1_ref.dtype)
        nat[c] = slab
        for r in range(4):
            rows = nat[c, pl.ds(r, n4, stride=4), :]
            mod4[c, r * n4:(r + 1) * n4, :] = rows
            o4_ref[c, :, r * LANES:(r + 1) * LANES] = rows.astype(o4_ref.dtype)
        for r in range(16):
            a, r4 = divmod(r, 4)
            rows = mod4[c, pl.ds(r4 * n4 + a, n16, stride=4), :]
            o16_ref[c, :, r * LANES:(r + 1) * LANES] = rows.astype(o16_ref.dtype)

    def store_ret(c, slab):
        ret_ref[c] = slab.astype(ret_ref.dtype)

    for parity, (h_wr, h_rd) in enumerate(((h_even, h_odd), (h_odd, h_even))):
        mine = (i > 0) & (i % 2 == parity)
        pl.when(mine & (j < attn_tiles))(functools.partial(step, h_wr, h_rd, store_attn))
        pl.when(mine & (j >= attn_tiles))(functools.partial(step, h_wr, h_rd, store_ret))


def _in_proj(x2d, norm_w, w_bf16, *, batch, seq, tm=1024, tn=1024):
    m, d = x2d.shape
    n = w_bf16.shape[1]
    n_attn = 3 * N_ATTN_HEADS * ATTN_HEAD_DIM
    ns = tn // LANES
    tiles = m // tm
    nj = n // tn
    attn_tiles = n_attn // tn
    assert 0 < attn_tiles < nj
    chunks = tm // NORM_ROWS
    while chunks > nj:
        chunks //= 2
    xr = tm // chunks
    per_b = seq // tm
    q_tiles = N_ATTN_HEADS * ATTN_HEAD_DIM // tn

    def t_of(i):
        return jnp.maximum(i - 1, 0)

    def ja(i, j):
        return jnp.where(i == 0, 0, jnp.minimum(j, attn_tiles - 1))

    def jr(i, j):
        return jnp.where(i == 0, 0, jnp.maximum(j - attn_tiles, 0))

    return pl.pallas_call(
        functools.partial(_inproj_kernel, tm=tm, tn=tn, attn_tiles=attn_tiles, q_tiles=q_tiles),
        grid=(tiles + 1, nj),
        in_specs=[
            pl.BlockSpec((xr, d), lambda i, j: (
                jnp.minimum(i, tiles - 1) * chunks + jnp.minimum(j, chunks - 1), 0)),
            pl.BlockSpec((1, d), lambda i, j: (0, 0)),
            pl.BlockSpec((d, tn), lambda i, j: (0, jnp.where(i == 0, 0, j))),
        ],
        out_specs=[
            pl.BlockSpec((ns, tm, LANES), lambda i, j: (ja(i, j), t_of(i), 0)),
            pl.BlockSpec((ns, None, tm // 4, 4 * LANES),
                         lambda i, j: (ja(i, j), t_of(i) // per_b, t_of(i) % per_b, 0)),
            pl.BlockSpec((ns, None, tm // 16, 16 * LANES),
                         lambda i, j: (ja(i, j), t_of(i) // per_b, t_of(i) % per_b, 0)),
            pl.BlockSpec((ns, tm, LANES), lambda i, j: (jr(i, j), t_of(i), 0)),
        ],
        out_shape=[
            jax.ShapeDtypeStruct((n_attn // LANES, m, LANES), BF16),
            jax.ShapeDtypeStruct((n_attn // LANES, batch, seq // 4, 4 * LANES), BF16),
            jax.ShapeDtypeStruct((n_attn // LANES, batch, seq // 16, 16 * LANES), BF16),
            jax.ShapeDtypeStruct(((n - n_attn) // LANES, m, LANES), BF16),
        ],
        scratch_shapes=[pltpu.VMEM((tm, d), BF16),
                        pltpu.VMEM((tm, d), BF16),
                        pltpu.VMEM((ns, tm, LANES), F32),
                        pltpu.VMEM((ns, tm, LANES), F32)],
        compiler_params=pltpu.CompilerParams(
            dimension_semantics=("arbitrary", "arbitrary"),
            vmem_limit_bytes=VMEM_LIMIT),
        name="in_proj",
    )(x2d, norm_w.reshape(1, d), w_bf16)


def _attn_kernel(slopes_ref, q1, k1, v1, q4, k4, v4, q16, k16, v16, o_ref,
                 bm, macc, lacc, oacc, onat, *, seq):
    blk = ATTN_BLK
    nblk = seq // blk
    per4 = nblk // 4
    slope = slopes_ref[pl.program_id(1)]

    qi = lax.broadcasted_iota(jnp.int32, (blk, 2 * blk), 0)
    kj = lax.broadcasted_iota(jnp.int32, (blk, 2 * blk), 1)
    diff = qi - kj + blk
    valid = (diff >= 0) & (diff <= blk)
    dist = diff.astype(F32)
    for pi, dil in enumerate(DILATIONS):
        bm[pi] = jnp.where(valid, (-LOG2E * slope) * (dist * float(dil)), MASK_VALUE)

    def scores(pi, q_at, k_at):
        kk = k_at()
        bias = bm[pi] if kk.shape[0] == 2 * blk else bm[pi, :, blk:]
        return lax.dot_general(q_at(), kk, (((1,), (1,)), ((), ())),
                               preferred_element_type=F32) + bias

    def finish(s, pi, v_at, dst):
        mx = jnp.max(s, axis=-1, keepdims=True)
        p = jnp.exp2(s - mx).astype(BF16)
        vv = v_at()
        acc = jnp.dot(p, jnp.concatenate([vv, jnp.ones_like(vv)], axis=-1),
                      preferred_element_type=F32)
        oacc[pi, dst, :] = acc[:, :LANES]
        lacc[pi, dst, :] = acc[:, LANES:]
        macc[pi, dst, :] = jnp.broadcast_to(mx, (blk, LANES))

    def p0_block(i):
        rows = slice(i * blk, (i + 1) * blk)
        kr = slice(max(i - 1, 0) * blk, (i + 1) * blk)
        return (0, lambda: q1[rows, :], lambda: k1[kr, :], lambda: v1[kr, :], rows)

    def p1_block(r4, n):
        ln = slice(r4 * LANES, (r4 + 1) * LANES)
        qr = slice(n * blk, (n + 1) * blk)
        kr = slice(max(n - 1, 0) * blk, (n + 1) * blk)
        dst = slice((r4 * per4 + n) * blk, (r4 * per4 + n + 1) * blk)
        return (1, lambda: q4[qr, ln], lambda: k4[kr, ln], lambda: v4[kr, ln], dst)

    def p2_block(r):
        ln = slice(r * LANES, (r + 1) * LANES)
        dst = pl.ds((r % 4) * (seq // 4) + r // 4, blk, stride=4)
        return (2, lambda: q16[:, ln], lambda: k16[:, ln], lambda: v16[:, ln], dst)

    def combine(n):
        for r4 in range(4):
            rows = slice((r4 * per4 + n) * blk, (r4 * per4 + n + 1) * blk)
            nat = pl.ds(r4 + 4 * blk * n, blk, stride=4)
            at = (nat, rows, rows)
            ms = [macc[pi, at[pi], :] for pi in range(3)]
            mx = jnp.maximum(jnp.maximum(ms[0], ms[1]), ms[2])
            es = [jnp.exp2(m - mx) for m in ms]
            den = es[0] * lacc[0, nat, :] + es[1] * lacc[1, rows, :] + es[2] * lacc[2, rows, :]
            num = es[0] * oacc[0, nat, :] + es[1] * oacc[1, rows, :] + es[2] * oacc[2, rows, :]
            onat[nat, :] = num * (1.0 / den)
        done = slice(4 * blk * n, 4 * blk * (n + 1))
        o_ref[done, :] = onat[done, :].astype(o_ref.dtype)

    work = [p2_block(r) for r in range(nblk)]
    after = {}
    for n in range(per4):
        work += [p0_block(4 * n + a) for a in range(4)] + [p1_block(r4, n) for r4 in range(4)]
        after[len(work) - 1] = n

    pending = {}
    for t in range(len(work) + ATTN_LOOKAHEAD):
        if t < len(work):
            pi, q_at, k_at, _, _ = work[t]
            pending[t] = scores(pi, q_at, k_at)
        b = t - ATTN_LOOKAHEAD
        if b >= 0:
            pi, _, _, v_at, dst = work[b]
            finish(pending.pop(b), pi, v_at, dst)
            if b in after:
                combine(after[b])


def _attention(nat, mod4, mod16, slopes, *, batch, seq):
    h = N_ATTN_HEADS
    specs = [pl.BlockSpec((None, seq, LANES), lambda b, hh, t=t: (t * h + hh, b, 0))
             for t in range(3)]
    for d in DILATIONS[1:]:
        specs += [pl.BlockSpec((None, None, seq // d, d * LANES),
                               lambda b, hh, t=t: (t * h + hh, b, 0, 0)) for t in range(3)]
    return pl.pallas_call(
        functools.partial(_attn_kernel, seq=seq),
        grid=(batch, h),
        in_specs=[pl.BlockSpec(memory_space=pltpu.SMEM)] + specs,
        out_specs=pl.BlockSpec((None, seq, LANES), lambda b, hh: (hh, b, 0)),
        out_shape=jax.ShapeDtypeStruct((h, batch * seq, LANES), BF16),
        scratch_shapes=[
            pltpu.VMEM((3, ATTN_BLK, 2 * ATTN_BLK), F32),
            pltpu.VMEM((3, seq, LANES), F32),
            pltpu.VMEM((3, seq, LANES), F32),
            pltpu.VMEM((3, seq, LANES), F32),
            pltpu.VMEM((seq, LANES), F32),
        ],
        compiler_params=pltpu.CompilerParams(
            dimension_semantics=("parallel", "parallel"),
            vmem_limit_bytes=VMEM_LIMIT),
        name="dilated_attention",
    )(slopes, nat, nat, nat, mod4, mod4, mod4, mod16, mod16, mod16)


def _ret_kernel(lg_ref, q_ref, k_ref, v_ref, g_ref, o_ref, decay, zeta, xi, states, *, seq):
    c = RET_CHUNK
    dh = RET_HEAD_DIM
    nc = seq // c
    lg = lg_ref[pl.program_id(1)]
    k_scale = 1.0 / math.sqrt(dh)
    assert math.log2(k_scale).is_integer()

    ii = lax.broadcasted_iota(jnp.int32, (c, c), 0)
    jj = lax.broadcasted_iota(jnp.int32, (c, c), 1)
    dif = (ii - jj).astype(F32)
    decay[...] = jnp.where(dif >= 0, jnp.exp(lg * jnp.maximum(dif, 0.0)), 0.0) * k_scale
    idx = lax.broadcasted_iota(jnp.int32, (c, dh), 0).astype(F32)
    zeta[...] = jnp.exp(lg * (c - 1.0 - idx)) * k_scale
    xi[...] = jnp.exp(lg * (idx + 1.0))
    gamma_chunk = jnp.exp(jnp.full((dh, dh), lg * c, F32))

    def wide(ref, n):
        rows = slice(n * c, (n + 1) * c)
        return jnp.concatenate([ref[0, rows, :], ref[1, rows, :]], axis=-1)

    st = jnp.zeros((dh, dh), F32)
    for n in range(nc):
        states[n] = st.astype(BF16)
        if n + 1 < nc:
            kz_t = (wide(k_ref, n).astype(F32) * zeta[...]).T.astype(BF16)
            st = st * gamma_chunk + jnp.dot(kz_t, wide(v_ref, n), preferred_element_type=F32)

    def front(n):
        qn = wide(q_ref, n)
        sc = lax.dot_general(qn, wide(k_ref, n), (((1,), (1,)), ((), ())),
                             preferred_element_type=F32) * decay[...]
        cross = jnp.dot(qn, states[n], preferred_element_type=F32) * xi[...]
        return sc.astype(BF16), cross

    def back(n, sc, cross):
        ret = jnp.dot(sc, wide(v_ref, n), preferred_element_type=F32) + cross
        ret = ret * lax.rsqrt(jnp.mean(ret * ret, axis=-1, keepdims=True) + NORM_EPS)
        gate = wide(g_ref, n).astype(F32)
        out = (gate * jax.nn.sigmoid(gate) * ret).astype(o_ref.dtype)
        o_ref[0, n * c:(n + 1) * c, :] = out[:, :LANES]
        o_ref[1, n * c:(n + 1) * c, :] = out[:, LANES:]

    pending = {}
    for t in range(nc + RET_LOOKAHEAD):
        if t < nc:
            pending[t] = front(t)
        if t >= RET_LOOKAHEAD:
            back(t - RET_LOOKAHEAD, *pending.pop(t - RET_LOOKAHEAD))


def _retention(proj_hm, log_gamma, *, batch, seq):
    h = N_RET_HEADS
    blk = (2, seq, LANES)
    return pl.pallas_call(
        functools.partial(_ret_kernel, seq=seq),
        grid=(batch, h),
        in_specs=[
            pl.BlockSpec(memory_space=pltpu.SMEM),
            pl.BlockSpec(blk, lambda b, hh: (hh, b, 0)),
            pl.BlockSpec(blk, lambda b, hh: (h + hh, b, 0)),
            pl.BlockSpec(blk, lambda b, hh: (2 * h + hh, b, 0)),
            pl.BlockSpec(blk, lambda b, hh: (3 * h + hh, b, 0)),
        ],
        out_specs=pl.BlockSpec(blk, lambda b, hh: (hh, b, 0)),
        out_shape=jax.ShapeDtypeStruct((2 * h, batch * seq, LANES), BF16),
        scratch_shapes=[
            pltpu.VMEM((RET_CHUNK, RET_CHUNK), F32),
            pltpu.VMEM((RET_CHUNK, RET_HEAD_DIM), F32),
            pltpu.VMEM((RET_CHUNK, RET_HEAD_DIM), F32),
            pltpu.VMEM((seq // RET_CHUNK, RET_HEAD_DIM, RET_HEAD_DIM), BF16),
        ],
        compiler_params=pltpu.CompilerParams(
            dimension_semantics=("parallel", "parallel"),
            vmem_limit_bytes=VMEM_LIMIT),
        name="retention",
    )(log_gamma, proj_hm, proj_hm, proj_hm, proj_hm)


def _outproj_kernel(x_ref, a_ref, r_ref, wa_ref, wr_ref, o_ref):
    na = a_ref.shape[0]
    nr = r_ref.shape[0]
    a = jnp.concatenate([a_ref[i] for i in range(na)], axis=-1)
    r = jnp.concatenate([r_ref[i] for i in range(nr)], axis=-1)
    y = jnp.dot(a, wa_ref[...], preferred_element_type=F32)
    y = y + jnp.dot(r, wr_ref[...], preferred_element_type=F32)
    o_ref[...] = x_ref[...] + y


def _out_proj(x2d, attn_hm, ret_hm, w_out_bf16, *, tm=512):
    m, d = x2d.shape
    na, nr = attn_hm.shape[0], ret_hm.shape[0]
    assert na == nr
    return pl.pallas_call(
        _outproj_kernel,
        grid=(m // tm,),
        in_specs=[
            pl.BlockSpec((tm, d), lambda i: (i, 0)),
            pl.BlockSpec((na, tm, LANES), lambda i: (0, i, 0)),
            pl.BlockSpec((nr, tm, LANES), lambda i: (0, i, 0)),
            pl.BlockSpec((na * LANES, d), lambda i: (0, 0)),
            pl.BlockSpec((nr * LANES, d), lambda i: (1, 0)),
        ],
        out_specs=pl.BlockSpec((tm, d), lambda i: (i, 0)),
        out_shape=jax.ShapeDtypeStruct((m, d), F32),
        compiler_params=pltpu.CompilerParams(
            dimension_semantics=("parallel",),
            vmem_limit_bytes=VMEM_LIMIT),
        name="out_proj",
    )(x2d, attn_hm, ret_hm, w_out_bf16, w_out_bf16)


def _ffn_kernel(x_ref, nw_ref, wg_ref, wu_ref, wd_ref, fw_ref, o_ref, h_ref, acc_ref,
                *, tm, final_norm):
    f = pl.program_id(1)

    @pl.when(f == 0)
    def _():
        def body(i, c):
            r = pl.multiple_of(i * NORM_ROWS, NORM_ROWS)
            h_ref[pl.ds(r, NORM_ROWS), :] = _rms_rows(
                x_ref[pl.ds(r, NORM_ROWS), :], nw_ref[...]).astype(h_ref.dtype)
            acc_ref[pl.ds(r, NORM_ROWS), :] = jnp.zeros((NORM_ROWS, acc_ref.shape[1]), F32)
            return c
        lax.fori_loop(0, tm // NORM_ROWS, body, 0)

    h = h_ref[...]
    g = jnp.dot(h, wg_ref[...], preferred_element_type=F32)
    u = jnp.dot(h, wu_ref[...], preferred_element_type=F32)
    a = (g * jax.nn.sigmoid(g) * u).astype(BF16)
    y = jnp.dot(a, wd_ref[...], preferred_element_type=F32)
    acc_ref[...] += y

    @pl.when(f == pl.num_programs(1) - 1)
    def _():
        def body(i, c):
            r = pl.multiple_of(i * NORM_ROWS, NORM_ROWS)
            x2 = x_ref[pl.ds(r, NORM_ROWS), :] + acc_ref[pl.ds(r, NORM_ROWS), :]
            if final_norm:
                x2 = _rms_rows(x2, fw_ref[...])
            o_ref[pl.ds(r, NORM_ROWS), :] = x2
            return c
        lax.fori_loop(0, tm // NORM_ROWS, body, 0)


def _ffn(x2d, norm_w, wg, wu, wd, final_w, *, final_norm, tm=512, tf=512):
    m, d = x2d.shape
    hid = wg.shape[1]
    return pl.pallas_call(
        functools.partial(_ffn_kernel, tm=tm, final_norm=final_norm),
        grid=(m // tm, hid // tf),
        in_specs=[
            pl.BlockSpec((tm, d), lambda i, f: (i, 0)),
            pl.BlockSpec((1, d), lambda i, f: (0, 0)),
            pl.BlockSpec((d, tf), lambda i, f: (0, f)),
            pl.BlockSpec((d, tf), lambda i, f: (0, f)),
            pl.BlockSpec((tf, d), lambda i, f: (f, 0)),
            pl.BlockSpec((1, d), lambda i, f: (0, 0)),
        ],
        out_specs=pl.BlockSpec((tm, d), lambda i, f: (i, 0)),
        out_shape=jax.ShapeDtypeStruct((m, d), F32),
        scratch_shapes=[pltpu.VMEM((tm, d), BF16), pltpu.VMEM((tm, d), F32)],
        compiler_params=pltpu.CompilerParams(
            dimension_semantics=("parallel", "arbitrary"),
            vmem_limit_bytes=VMEM_LIMIT),
        name="ffn",
    )(x2d, norm_w.reshape(1, d), wg, wu, wd, final_w.reshape(1, d))


def kernel(x, norm_mix_w, w_in, w_out, norm_ffn_w, w_gate, w_up, w_down, norm_final_w):
    batch, seq, d = x.shape
    depth = w_in.shape[0]
    assert seq == DILATIONS[-1] * ATTN_BLK and seq % RET_CHUNK == 0
    slopes = jnp.exp2(-8.0 * jnp.arange(1, N_ATTN_HEADS + 1, dtype=F32) / N_ATTN_HEADS)
    log_gamma = jnp.log(1.0 - jnp.exp2(-5.0 - jnp.arange(N_RET_HEADS, dtype=F32)))

    xs = x.reshape(batch * seq, d)
    for layer in range(depth):
        w_in_l = w_in[layer].astype(BF16)
        nat, mod4, mod16, proj_ret = _in_proj(xs, norm_mix_w[layer], w_in_l,
                                              batch=batch, seq=seq)
        attn = _attention(nat, mod4, mod16, slopes, batch=batch, seq=seq)
        ret = _retention(proj_ret, log_gamma, batch=batch, seq=seq)
        xs = _out_proj(xs, attn, ret, w_out[layer].astype(BF16))
        xs = _ffn(xs, norm_ffn_w[layer], w_gate[layer].astype(BF16),
                  w_up[layer].astype(BF16), w_down[layer].astype(BF16), norm_final_w,
                  final_norm=(layer == depth - 1))
    return xs.reshape(batch, seq, d)
```

```python
import functools
import math

import jax
import jax.numpy as jnp
from jax import lax
from jax.experimental import pallas as pl
from jax.experimental.pallas import tpu as pltpu

F32 = jnp.float32
BF16 = jnp.bfloat16

LANES = 128
ATTN_HEAD_DIM = 128
N_ATTN_HEADS = 8
RET_HEAD_DIM = 256
N_RET_HEADS = 4
RET_CHUNK = 128
ATTN_BLK = 128
DILATIONS = (1, 4, 16)
ATTN_LOOKAHEAD = 8
RET_LOOKAHEAD = 2
NORM_EPS = 1e-6
MASK_VALUE = -1e30
LOG2E = math.log2(math.e)
QK_SCALE = LOG2E / math.sqrt(ATTN_HEAD_DIM)
NORM_ROWS = 128
FFN_CAST_BLOCKS = 64
VMEM_LIMIT = 56 * 1024 * 1024


def _rms_rows(x, w):
    ms = jnp.mean(x * x, axis=-1, keepdims=True)
    return x * lax.rsqrt(ms + NORM_EPS) * w


def _inproj_kernel(*refs, tm, tn, attn_tiles, q_tiles, casts):
    nc = len(casts)
    x_ref, nw_ref, w_ref = refs[:3]
    cast_in = refs[3:3 + nc]
    o1_ref, o4_ref, o16_ref, ret_ref = refs[3 + nc:7 + nc]
    cast_out = refs[7 + nc:7 + 2 * nc]
    h_even, h_odd, nat, mod4 = refs[7 + 2 * nc:]
    i = pl.program_id(0)
    j = pl.program_id(1)
    xr = x_ref.shape[0]
    n4, n16 = tm // 4, tm // 16

    step_no = (i - 1) * pl.num_programs(1) + j
    for src, dst, (first, count) in zip(cast_in, cast_out, casts):
        @pl.when((step_no >= first) & (step_no < first + count))
        def _(src=src, dst=dst):
            dst[...] = src[...].astype(dst.dtype)

    def norm_chunk(h_ref):
        r = pl.multiple_of(jnp.minimum(j, tm // xr - 1) * xr, xr)
        h_ref[pl.ds(r, xr), :] = _rms_rows(x_ref[...], nw_ref[...]).astype(BF16)

    def step(h_wr, h_rd, store):
        norm_chunk(h_wr)
        acc = jnp.dot(h_rd[...], w_ref[...], preferred_element_type=F32)
        for c in range(tn // LANES):
            store(c, acc[:, c * LANES:(c + 1) * LANES])

    @pl.when(i == 0)
    def _():
        norm_chunk(h_even)

    def store_attn(c, slab):
        slab = slab * jnp.where(j < q_tiles, QK_SCALE, 1.0)
        o1_ref[c] = slab.astype(o1_ref.dtype)
        b = c % nat.shape[0]
        nat[b] = slab
        for r in range(4):
            rows = nat[b, pl.ds(r, n4, stride=4), :]
            mod4[b, r * n4:(r + 1) * n4, :] = rows
            o4_ref[c, :, r * LANES:(r + 1) * LANES] = rows.astype(o4_ref.dtype)
        for r in range(16):
            a, r4 = divmod(r, 4)
            rows = mod4[b, pl.ds(r4 * n4 + a, n16, stride=4), :]
            o16_ref[c, :, r * LANES:(r + 1) * LANES] = rows.astype(o16_ref.dtype)

    def store_ret(c, slab):
        ret_ref[c] = slab.astype(ret_ref.dtype)

    for parity, (h_wr, h_rd) in enumerate(((h_even, h_odd), (h_odd, h_even))):
        mine = (i > 0) & (i % 2 == parity)
        pl.when(mine & (j < attn_tiles))(functools.partial(step, h_wr, h_rd, store_attn))
        pl.when(mine & (j >= attn_tiles))(functools.partial(step, h_wr, h_rd, store_ret))


def _in_proj(x2d, norm_w, w_bf16, side_f32, *, batch, seq, tm=1024, tn=1024):
    m, d = x2d.shape
    n = w_bf16.shape[1]
    n_attn = 3 * N_ATTN_HEADS * ATTN_HEAD_DIM
    ns = tn // LANES
    tiles = m // tm
    nj = n // tn
    attn_tiles = n_attn // tn
    assert 0 < attn_tiles < nj
    chunks = tm // NORM_ROWS
    while chunks > nj:
        chunks //= 2
    xr = tm // chunks
    per_b = seq // tm
    q_tiles = N_ATTN_HEADS * ATTN_HEAD_DIM // tn

    def t_of(i):
        return jnp.maximum(i - 1, 0)

    def ja(i, j):
        return jnp.where(i == 0, 0, jnp.minimum(j, attn_tiles - 1))

    def jr(i, j):
        return jnp.where(i == 0, 0, jnp.maximum(j - attn_tiles, 0))

    casts, cast_specs, cast_shapes = [], [], []
    for arr, first, blocks in side_f32:
        rows, cols = arr.shape
        assert rows % (blocks * 16) == 0
        assert first + blocks <= tiles * nj

        def at(i, j, first=first, blocks=blocks):
            return (jnp.clip((i - 1) * nj + j - first, 0, blocks - 1), 0)

        casts.append((first, blocks))
        cast_specs.append(pl.BlockSpec((rows // blocks, cols), at))
        cast_shapes.append(jax.ShapeDtypeStruct(arr.shape, BF16))

    return pl.pallas_call(
        functools.partial(_inproj_kernel, tm=tm, tn=tn, attn_tiles=attn_tiles, q_tiles=q_tiles,
                          casts=tuple(casts)),
        grid=(tiles + 1, nj),
        in_specs=[
            pl.BlockSpec((xr, d), lambda i, j: (
                jnp.minimum(i, tiles - 1) * chunks + jnp.minimum(j, chunks - 1), 0)),
            pl.BlockSpec((1, d), lambda i, j: (0, 0)),
            pl.BlockSpec((d, tn), lambda i, j: (0, jnp.where(i == 0, 0, j))),
        ] + cast_specs,
        out_specs=[
            pl.BlockSpec((ns, tm, LANES), lambda i, j: (ja(i, j), t_of(i), 0)),
            pl.BlockSpec((ns, None, tm // 4, 4 * LANES),
                         lambda i, j: (ja(i, j), t_of(i) // per_b, t_of(i) % per_b, 0)),
            pl.BlockSpec((ns, None, tm // 16, 16 * LANES),
                         lambda i, j: (ja(i, j), t_of(i) // per_b, t_of(i) % per_b, 0)),
            pl.BlockSpec((ns, tm, LANES), lambda i, j: (jr(i, j), t_of(i), 0)),
        ] + cast_specs,
        out_shape=[
            jax.ShapeDtypeStruct((n_attn // LANES, m, LANES), BF16),
            jax.ShapeDtypeStruct((n_attn // LANES, batch, seq // 4, 4 * LANES), BF16),
            jax.ShapeDtypeStruct((n_attn // LANES, batch, seq // 16, 16 * LANES), BF16),
            jax.ShapeDtypeStruct(((n - n_attn) // LANES, m, LANES), BF16),
        ] + cast_shapes,
        scratch_shapes=[pltpu.VMEM((tm, d), BF16),
                        pltpu.VMEM((tm, d), BF16),
                        pltpu.VMEM((2, tm, LANES), F32),
                        pltpu.VMEM((2, tm, LANES), F32)],
        compiler_params=pltpu.CompilerParams(
            dimension_semantics=("arbitrary", "arbitrary"),
            vmem_limit_bytes=VMEM_LIMIT),
        name="in_proj",
    )(x2d, norm_w.reshape(1, d), w_bf16, *[arr for arr, _, _ in side_f32])


def _attn_kernel(slopes_ref, q1, k1, v1, q4, k4, v4, q16, k16, v16, o_ref,
                 bm, macc, lacc, oacc, onat, *, seq):
    blk = ATTN_BLK
    nblk = seq // blk
    per4 = nblk // 4
    slope = slopes_ref[pl.program_id(1)]

    qi = lax.broadcasted_iota(jnp.int32, (blk, 2 * blk), 0)
    kj = lax.broadcasted_iota(jnp.int32, (blk, 2 * blk), 1)
    diff = qi - kj + blk
    valid = (diff >= 0) & (diff <= blk)
    dist = diff.astype(F32)
    for pi, dil in enumerate(DILATIONS):
        bm[pi] = jnp.where(valid, (-LOG2E * slope) * (dist * float(dil)), MASK_VALUE)

    def scores(pi, q_at, k_at):
        kk = k_at()
        bias = bm[pi] if kk.shape[0] == 2 * blk else bm[pi, :, blk:]
        return lax.dot_general(q_at(), kk, (((1,), (1,)), ((), ())),
                               preferred_element_type=F32) + bias

    def finish(s, pi, v_at, dst):
        mx = jnp.max(s, axis=-1, keepdims=True)
        p = jnp.exp2(s - mx).astype(BF16)
        vv = v_at()
        acc = jnp.dot(p, jnp.concatenate([vv, jnp.ones_like(vv)], axis=-1),
                      preferred_element_type=F32)
        oacc[pi, dst, :] = acc[:, :LANES]
        lacc[pi, dst, :] = acc[:, LANES:]
        macc[pi, dst, :] = jnp.broadcast_to(mx, (blk, LANES))

    def p0_block(i):
        rows = slice(i * blk, (i + 1) * blk)
        kr = slice(max(i - 1, 0) * blk, (i + 1) * blk)
        return (0, lambda: q1[rows, :], lambda: k1[kr, :], lambda: v1[kr, :], rows)

    def p1_block(r4, n):
        ln = slice(r4 * LANES, (r4 + 1) * LANES)
        qr = slice(n * blk, (n + 1) * blk)
        kr = slice(max(n - 1, 0) * blk, (n + 1) * blk)
        dst = slice((r4 * per4 + n) * blk, (r4 * per4 + n + 1) * blk)
        return (1, lambda: q4[qr, ln], lambda: k4[kr, ln], lambda: v4[kr, ln], dst)

    def p2_block(r):
        ln = slice(r * LANES, (r + 1) * LANES)
        dst = pl.ds((r % 4) * (seq // 4) + r // 4, blk, stride=4)
        return (2, lambda: q16[:, ln], lambda: k16[:, ln], lambda: v16[:, ln], dst)

    def combine(n):
        for r4 in range(4):
            rows = slice((r4 * per4 + n) * blk, (r4 * per4 + n + 1) * blk)
            nat = pl.ds(r4 + 4 * blk * n, blk, stride=4)
            at = (nat, rows, rows)
            ms = [macc[pi, at[pi], :] for pi in range(3)]
            mx = jnp.maximum(jnp.maximum(ms[0], ms[1]), ms[2])
            es = [jnp.exp2(m - mx) for m in ms]
            den = es[0] * lacc[0, nat, :] + es[1] * lacc[1, rows, :] + es[2] * lacc[2, rows, :]
            num = es[0] * oacc[0, nat, :] + es[1] * oacc[1, rows, :] + es[2] * oacc[2, rows, :]
            onat[nat, :] = num * (1.0 / den)
        done = slice(4 * blk * n, 4 * blk * (n + 1))
        o_ref[done, :] = onat[done, :].astype(o_ref.dtype)

    work = [p2_block(r) for r in range(nblk)]
    after = {}
    for n in range(per4):
        work += [p0_block(4 * n + a) for a in range(4)] + [p1_block(r4, n) for r4 in range(4)]
        after[len(work) - 1] = n

    pending = {}
    for t in range(len(work) + ATTN_LOOKAHEAD):
        if t < len(work):
            pi, q_at, k_at, _, _ = work[t]
            pending[t] = scores(pi, q_at, k_at)
        b = t - ATTN_LOOKAHEAD
        if b >= 0:
            pi, _, _, v_at, dst = work[b]
            finish(pending.pop(b), pi, v_at, dst)
            if b in after:
                combine(after[b])


def _attention(nat, mod4, mod16, slopes, *, batch, seq):
    h = N_ATTN_HEADS
    specs = [pl.BlockSpec((None, seq, LANES), lambda b, hh, t=t: (t * h + hh, b, 0))
             for t in range(3)]
    for d in DILATIONS[1:]:
        specs += [pl.BlockSpec((None, None, seq // d, d * LANES),
                               lambda b, hh, t=t: (t * h + hh, b, 0, 0)) for t in range(3)]
    return pl.pallas_call(
        functools.partial(_attn_kernel, seq=seq),
        grid=(batch, h),
        in_specs=[pl.BlockSpec(memory_space=pltpu.SMEM)] + specs,
        out_specs=pl.BlockSpec((None, seq, LANES), lambda b, hh: (hh, b, 0)),
        out_shape=jax.ShapeDtypeStruct((h, batch * seq, LANES), BF16),
        scratch_shapes=[
            pltpu.VMEM((3, ATTN_BLK, 2 * ATTN_BLK), F32),
            pltpu.VMEM((3, seq, LANES), F32),
            pltpu.VMEM((3, seq, LANES), F32),
            pltpu.VMEM((3, seq, LANES), F32),
            pltpu.VMEM((seq, LANES), F32),
        ],
        compiler_params=pltpu.CompilerParams(
            dimension_semantics=("parallel", "parallel"),
            vmem_limit_bytes=VMEM_LIMIT),
        name="dilated_attention",
    )(slopes, nat, nat, nat, mod4, mod4, mod4, mod16, mod16, mod16)


def _ret_kernel(lg_ref, q_ref, k_ref, v_ref, g_ref, o_ref, decay, zeta, xi, states, *, seq):
    c = RET_CHUNK
    dh = RET_HEAD_DIM
    nc = seq // c
    lg = lg_ref[pl.program_id(1)]
    k_scale = 1.0 / math.sqrt(dh)
    assert math.log2(k_scale).is_integer()

    ii = lax.broadcasted_iota(jnp.int32, (c, c), 0)
    jj = lax.broadcasted_iota(jnp.int32, (c, c), 1)
    dif = (ii - jj).astype(F32)
    decay[...] = jnp.where(dif >= 0, jnp.exp(lg * jnp.maximum(dif, 0.0)), 0.0) * k_scale
    idx = lax.broadcasted_iota(jnp.int32, (c, dh), 0).astype(F32)
    zeta[...] = jnp.exp(lg * (c - 1.0 - idx)) * k_scale
    xi[...] = jnp.exp(lg * (idx + 1.0))
    gamma_chunk = jnp.exp(jnp.full((dh, dh), lg * c, F32))

    def wide(ref, n):
        rows = slice(n * c, (n + 1) * c)
        return jnp.concatenate([ref[0, rows, :], ref[1, rows, :]], axis=-1)

    st = jnp.zeros((dh, dh), F32)
    for n in range(nc):
        states[n] = st.astype(BF16)
        if n + 1 < nc:
            kz_t = (wide(k_ref, n).astype(F32) * zeta[...]).T.astype(BF16)
            st = st * gamma_chunk + jnp.dot(kz_t, wide(v_ref, n), preferred_element_type=F32)

    def front(n):
        qn = wide(q_ref, n)
        sc = lax.dot_general(qn, wide(k_ref, n), (((1,), (1,)), ((), ())),
                             preferred_element_type=F32) * decay[...]
        cross = jnp.dot(qn, states[n], preferred_element_type=F32) * xi[...]
        return sc.astype(BF16), cross

    def back(n, sc, cross):
        ret = jnp.dot(sc, wide(v_ref, n), preferred_element_type=F32) + cross
        ret = ret * lax.rsqrt(jnp.mean(ret * ret, axis=-1, keepdims=True) + NORM_EPS)
        gate = wide(g_ref, n).astype(F32)
        out = (gate * jax.nn.sigmoid(gate) * ret).astype(o_ref.dtype)
        o_ref[0, n * c:(n + 1) * c, :] = out[:, :LANES]
        o_ref[1, n * c:(n + 1) * c, :] = out[:, LANES:]

    pending = {}
    for t in range(nc + RET_LOOKAHEAD):
        if t < nc:
            pending[t] = front(t)
        if t >= RET_LOOKAHEAD:
            back(t - RET_LOOKAHEAD, *pending.pop(t - RET_LOOKAHEAD))


def _retention(proj_hm, log_gamma, *, batch, seq):
    h = N_RET_HEADS
    blk = (2, seq, LANES)
    return pl.pallas_call(
        functools.partial(_ret_kernel, seq=seq),
        grid=(batch, h),
        in_specs=[
            pl.BlockSpec(memory_space=pltpu.SMEM),
            pl.BlockSpec(blk, lambda b, hh: (hh, b, 0)),
            pl.BlockSpec(blk, lambda b, hh: (h + hh, b, 0)),
            pl.BlockSpec(blk, lambda b, hh: (2 * h + hh, b, 0)),
            pl.BlockSpec(blk, lambda b, hh: (3 * h + hh, b, 0)),
        ],
        out_specs=pl.BlockSpec(blk, lambda b, hh: (hh, b, 0)),
        out_shape=jax.ShapeDtypeStruct((2 * h, batch * seq, LANES), BF16),
        scratch_shapes=[
            pltpu.VMEM((RET_CHUNK, RET_CHUNK), F32),
            pltpu.VMEM((RET_CHUNK, RET_HEAD_DIM), F32),
            pltpu.VMEM((RET_CHUNK, RET_HEAD_DIM), F32),
            pltpu.VMEM((seq // RET_CHUNK, RET_HEAD_DIM, RET_HEAD_DIM), BF16),
        ],
        compiler_params=pltpu.CompilerParams(
            dimension_semantics=("parallel", "parallel"),
            vmem_limit_bytes=VMEM_LIMIT),
        name="retention",
    )(log_gamma, proj_hm, proj_hm, proj_hm, proj_hm)


def _outproj_kernel(x_ref, a_ref, r_ref, wa_ref, wr_ref, o_ref):
    na = a_ref.shape[0]
    nr = r_ref.shape[0]
    a = jnp.concatenate([a_ref[i] for i in range(na)], axis=-1)
    r = jnp.concatenate([r_ref[i] for i in range(nr)], axis=-1)
    y = jnp.dot(a, wa_ref[...], preferred_element_type=F32)
    y = y + jnp.dot(r, wr_ref[...], preferred_element_type=F32)
    o_ref[...] = x_ref[...] + y


def _out_proj(x2d, attn_hm, ret_hm, w_out_bf16, *, tm=512):
    m, d = x2d.shape
    na, nr = attn_hm.shape[0], ret_hm.shape[0]
    assert na == nr
    return pl.pallas_call(
        _outproj_kernel,
        grid=(m // tm,),
        in_specs=[
            pl.BlockSpec((tm, d), lambda i: (i, 0)),
            pl.BlockSpec((na, tm, LANES), lambda i: (0, i, 0)),
            pl.BlockSpec((nr, tm, LANES), lambda i: (0, i, 0)),
            pl.BlockSpec((na * LANES, d), lambda i: (0, 0)),
            pl.BlockSpec((nr * LANES, d), lambda i: (1, 0)),
        ],
        out_specs=pl.BlockSpec((tm, d), lambda i: (i, 0)),
        out_shape=jax.ShapeDtypeStruct((m, d), F32),
        compiler_params=pltpu.CompilerParams(
            dimension_semantics=("parallel",),
            vmem_limit_bytes=VMEM_LIMIT),
        name="out_proj",
    )(x2d, attn_hm, ret_hm, w_out_bf16, w_out_bf16)


def _ffn_kernel(x_ref, nw_ref, wg_ref, wu_ref, wd_ref, fw_ref, o_ref, h_ref, acc_ref,
                *, tm, final_norm):
    f = pl.program_id(1)

    @pl.when(f == 0)
    def _():
        def body(i, c):
            r = pl.multiple_of(i * NORM_ROWS, NORM_ROWS)
            h_ref[pl.ds(r, NORM_ROWS), :] = _rms_rows(
                x_ref[pl.ds(r, NORM_ROWS), :], nw_ref[...]).astype(h_ref.dtype)
            acc_ref[pl.ds(r, NORM_ROWS), :] = jnp.zeros((NORM_ROWS, acc_ref.shape[1]), F32)
            return c
        lax.fori_loop(0, tm // NORM_ROWS, body, 0)

    h = h_ref[...]
    g = jnp.dot(h, wg_ref[...], preferred_element_type=F32)
    u = jnp.dot(h, wu_ref[...], preferred_element_type=F32)
    a = (g * jax.nn.sigmoid(g) * u).astype(BF16)
    y = jnp.dot(a, wd_ref[...], preferred_element_type=F32)
    acc_ref[...] += y

    @pl.when(f == pl.num_programs(1) - 1)
    def _():
        def body(i, c):
            r = pl.multiple_of(i * NORM_ROWS, NORM_ROWS)
            x2 = x_ref[pl.ds(r, NORM_ROWS), :] + acc_ref[pl.ds(r, NORM_ROWS), :]
            if final_norm:
                x2 = _rms_rows(x2, fw_ref[...])
            o_ref[pl.ds(r, NORM_ROWS), :] = x2
            return c
        lax.fori_loop(0, tm // NORM_ROWS, body, 0)


def _ffn(x2d, norm_w, wg, wu, wd, final_w, *, final_norm, tm=512, tf=512):
    m, d = x2d.shape
    hid = wg.shape[1]
    return pl.pallas_call(
        functools.partial(_ffn_kernel, tm=tm, final_norm=final_norm),
        grid=(m // tm, hid // tf),
        in_specs=[
            pl.BlockSpec((tm, d), lambda i, f: (i, 0)),
            pl.BlockSpec((1, d), lambda i, f: (0, 0)),
            pl.BlockSpec((d, tf), lambda i, f: (0, f)),
            pl.BlockSpec((d, tf), lambda i, f: (0, f)),
            pl.BlockSpec((tf, d), lambda i, f: (f, 0)),
            pl.BlockSpec((1, d), lambda i, f: (0, 0)),
        ],
        out_specs=pl.BlockSpec((tm, d), lambda i, f: (i, 0)),
        out_shape=jax.ShapeDtypeStruct((m, d), F32),
        scratch_shapes=[pltpu.VMEM((tm, d), BF16), pltpu.VMEM((tm, d), F32)],
        compiler_params=pltpu.CompilerParams(
            dimension_semantics=("parallel", "arbitrary"),
            vmem_limit_bytes=VMEM_LIMIT),
        name="ffn",
    )(x2d, norm_w.reshape(1, d), wg, wu, wd, final_w.reshape(1, d))


def kernel(x, norm_mix_w, w_in, w_out, norm_ffn_w, w_gate, w_up, w_down, norm_final_w):
    batch, seq, d = x.shape
    depth = w_in.shape[0]
    assert seq == DILATIONS[-1] * ATTN_BLK and seq % RET_CHUNK == 0
    slopes = jnp.exp2(-8.0 * jnp.arange(1, N_ATTN_HEADS + 1, dtype=F32) / N_ATTN_HEADS)
    log_gamma = jnp.log(1.0 - jnp.exp2(-5.0 - jnp.arange(N_RET_HEADS, dtype=F32)))

    xs = x.reshape(batch * seq, d)
    for layer in range(depth):
        w_in_l = w_in[layer].astype(BF16)
        side = ((w_gate[layer], 0, FFN_CAST_BLOCKS), (w_up[layer], 0, FFN_CAST_BLOCKS),
                (w_down[layer], FFN_CAST_BLOCKS, FFN_CAST_BLOCKS // 2))
        nat, mod4, mod16, proj_ret, wg, wu, wd = _in_proj(xs, norm_mix_w[layer], w_in_l, side,
                                                          batch=batch, seq=seq)
        attn = _attention(nat, mod4, mod16, slopes, batch=batch, seq=seq)
        ret = _retention(proj_ret, log_gamma, batch=batch, seq=seq)
        xs = _out_proj(xs, attn, ret, w_out[layer].astype(BF16))
        xs = _ffn(xs, norm_ffn_w[layer], wg, wu, wd, norm_final_w,
                  final_norm=(layer == depth - 1))
    return xs.reshape(batch, seq, d)
```

```python
import functools
import math

import jax
import jax.numpy as jnp
from jax import lax
from jax.experimental import pallas as pl
from jax.experimental.pallas import tpu as pltpu

F32 = jnp.float32
BF16 = jnp.bfloat16

LANES = 128
ATTN_HEAD_DIM = 128
N_ATTN_HEADS = 8
RET_HEAD_DIM = 256
N_RET_HEADS = 4
RET_CHUNK = 128
ATTN_BLK = 128
DILATIONS = (1, 4, 16)
ATTN_LOOKAHEAD = 8
RET_LOOKAHEAD = 2
NORM_EPS = 1e-6
MASK_VALUE = -1e30
LOG2E = math.log2(math.e)
QK_SCALE = LOG2E / math.sqrt(ATTN_HEAD_DIM)
NORM_ROWS = 128
FFN_CAST_BLOCKS = 64
VMEM_LIMIT = 56 * 1024 * 1024


def _rms_rows(x, w):
    ms = jnp.mean(x * x, axis=-1, keepdims=True)
    return x * lax.rsqrt(ms + NORM_EPS) * w


def _inproj_kernel(*refs, tm, tn, attn_tiles, q_tiles, casts):
    nc = len(casts)
    x_ref, nw_ref, w_ref = refs[:3]
    cast_in = refs[3:3 + nc]
    o1_ref, o4_ref, o16_ref, ret_ref = refs[3 + nc:7 + nc]
    cast_out = refs[7 + nc:7 + 2 * nc]
    h_even, h_odd, nat, mod4 = refs[7 + 2 * nc:]
    i = pl.program_id(0)
    j = pl.program_id(1)
    xr = x_ref.shape[0]
    n4, n16 = tm // 4, tm // 16

    step_no = (i - 1) * pl.num_programs(1) + j
    for src, dst, (first, count) in zip(cast_in, cast_out, casts):
        @pl.when((step_no >= first) & (step_no < first + count))
        def _(src=src, dst=dst):
            dst[...] = src[...].astype(dst.dtype)

    def norm_chunk(h_ref):
        r = pl.multiple_of(jnp.minimum(j, tm // xr - 1) * xr, xr)
        h_ref[pl.ds(r, xr), :] = _rms_rows(x_ref[...], nw_ref[...]).astype(BF16)

    def step(h_wr, h_rd, store):
        norm_chunk(h_wr)
        acc = jnp.dot(h_rd[...], w_ref[...], preferred_element_type=F32)
        for c in range(tn // LANES):
            store(c, acc[:, c * LANES:(c + 1) * LANES])

    @pl.when(i == 0)
    def _():
        norm_chunk(h_even)

    def store_attn(c, slab):
        slab = slab * jnp.where(j < q_tiles, QK_SCALE, 1.0)
        o1_ref[c] = slab.astype(o1_ref.dtype)
        b = c % nat.shape[0]
        nat[b] = slab
        for r in range(4):
            rows = nat[b, pl.ds(r, n4, stride=4), :]
            mod4[b, r * n4:(r + 1) * n4, :] = rows
            o4_ref[c, :, r * LANES:(r + 1) * LANES] = rows.astype(o4_ref.dtype)
        for r in range(16):
            a, r4 = divmod(r, 4)
            rows = mod4[b, pl.ds(r4 * n4 + a, n16, stride=4), :]
            o16_ref[c, :, r * LANES:(r + 1) * LANES] = rows.astype(o16_ref.dtype)

    def store_ret(c, slab):
        ret_ref[c] = slab.astype(ret_ref.dtype)

    for parity, (h_wr, h_rd) in enumerate(((h_even, h_odd), (h_odd, h_even))):
        mine = (i > 0) & (i % 2 == parity)
        pl.when(mine & (j < attn_tiles))(functools.partial(step, h_wr, h_rd, store_attn))
        pl.when(mine & (j >= attn_tiles))(functools.partial(step, h_wr, h_rd, store_ret))


def _in_proj(x2d, norm_w, w_bf16, side_f32, *, batch, seq, tm=1024, tn=1024):
    m, d = x2d.shape
    n = w_bf16.shape[1]
    n_attn = 3 * N_ATTN_HEADS * ATTN_HEAD_DIM
    ns = tn // LANES
    tiles = m // tm
    nj = n // tn
    attn_tiles = n_attn // tn
    assert 0 < attn_tiles < nj
    chunks = tm // NORM_ROWS
    while chunks > nj:
        chunks //= 2
    xr = tm // chunks
    per_b = seq // tm
    q_tiles = N_ATTN_HEADS * ATTN_HEAD_DIM // tn

    def t_of(i):
        return jnp.maximum(i - 1, 0)

    def ja(i, j):
        return jnp.where(i == 0, 0, jnp.minimum(j, attn_tiles - 1))

    def jr(i, j):
        return jnp.where(i == 0, 0, jnp.maximum(j - attn_tiles, 0))

    casts, cast_specs, cast_shapes = [], [], []
    for arr, first, blocks in side_f32:
        rows, cols = arr.shape
        assert rows % (blocks * 16) == 0
        assert first + blocks <= tiles * nj

        def at(i, j, first=first, blocks=blocks):
            return (jnp.clip((i - 1) * nj + j - first, 0, blocks - 1), 0)

        casts.append((first, blocks))
        cast_specs.append(pl.BlockSpec((rows // blocks, cols), at))
        cast_shapes.append(jax.ShapeDtypeStruct(arr.shape, BF16))

    return pl.pallas_call(
        functools.partial(_inproj_kernel, tm=tm, tn=tn, attn_tiles=attn_tiles, q_tiles=q_tiles,
                          casts=tuple(casts)),
        grid=(tiles + 1, nj),
        in_specs=[
            pl.BlockSpec((xr, d), lambda i, j: (
                jnp.minimum(i, tiles - 1) * chunks + jnp.minimum(j, chunks - 1), 0)),
            pl.BlockSpec((1, d), lambda i, j: (0, 0)),
            pl.BlockSpec((d, tn), lambda i, j: (0, jnp.where(i == 0, 0, j))),
        ] + cast_specs,
        out_specs=[
            pl.BlockSpec((ns, tm, LANES), lambda i, j: (ja(i, j), t_of(i), 0)),
            pl.BlockSpec((ns, None, tm // 4, 4 * LANES),
                         lambda i, j: (ja(i, j), t_of(i) // per_b, t_of(i) % per_b, 0)),
            pl.BlockSpec((ns, None, tm // 16, 16 * LANES),
                         lambda i, j: (ja(i, j), t_of(i) // per_b, t_of(i) % per_b, 0)),
            pl.BlockSpec((ns, tm, LANES), lambda i, j: (jr(i, j), t_of(i), 0)),
        ] + cast_specs,
        out_shape=[
            jax.ShapeDtypeStruct((n_attn // LANES, m, LANES), BF16),
            jax.ShapeDtypeStruct((n_attn // LANES, batch, seq // 4, 4 * LANES), BF16),
            jax.ShapeDtypeStruct((n_attn // LANES, batch, seq // 16, 16 * LANES), BF16),
            jax.ShapeDtypeStruct(((n - n_attn) // LANES, m, LANES), BF16),
        ] + cast_shapes,
        scratch_shapes=[pltpu.VMEM((tm, d), BF16),
                        pltpu.VMEM((tm, d), BF16),
                        pltpu.VMEM((2, tm, LANES), F32),
                        pltpu.VMEM((2, tm, LANES), F32)],
        compiler_params=pltpu.CompilerParams(
            dimension_semantics=("arbitrary", "arbitrary"),
            vmem_limit_bytes=VMEM_LIMIT),
        name="in_proj",
    )(x2d, norm_w.reshape(1, d), w_bf16, *[arr for arr, _, _ in side_f32])


def _attn_kernel(slopes_ref, q1, k1, v1, q4, k4, v4, q16, k16, v16, o_ref,
                 bm, macc, lacc, oacc, onat, *, seq):
    blk = ATTN_BLK
    nblk = seq // blk
    per4 = nblk // 4
    slope = slopes_ref[pl.program_id(1)]

    qi = lax.broadcasted_iota(jnp.int32, (blk, 2 * blk), 0)
    kj = lax.broadcasted_iota(jnp.int32, (blk, 2 * blk), 1)
    diff = qi - kj + blk
    valid = (diff >= 0) & (diff <= blk)
    dist = diff.astype(F32)
    for pi, dil in enumerate(DILATIONS):
        bm[pi] = jnp.where(valid, (-LOG2E * slope) * (dist * float(dil)), MASK_VALUE)

    def scores(pi, q_at, k_at):
        kk = k_at()
        bias = bm[pi] if kk.shape[0] == 2 * blk else bm[pi, :, blk:]
        return lax.dot_general(q_at(), kk, (((1,), (1,)), ((), ())),
                               preferred_element_type=F32) + bias

    def finish(s, pi, v_at, dst):
        mx = jnp.max(s, axis=-1, keepdims=True)
        p = jnp.exp2(s - mx).astype(BF16)
        vv = v_at()
        acc = jnp.dot(p, jnp.concatenate([vv, jnp.ones_like(vv)], axis=-1),
                      preferred_element_type=F32)
        oacc[pi, dst, :] = acc[:, :LANES]
        lacc[pi, dst, :] = acc[:, LANES:]
        macc[pi, dst, :] = jnp.broadcast_to(mx, (blk, LANES))

    def p0_block(i):
        rows = slice(i * blk, (i + 1) * blk)
        kr = slice(max(i - 1, 0) * blk, (i + 1) * blk)
        return (0, lambda: q1[rows, :], lambda: k1[kr, :], lambda: v1[kr, :], rows)

    def p1_block(r4, n):
        ln = slice(r4 * LANES, (r4 + 1) * LANES)
        qr = slice(n * blk, (n + 1) * blk)
        kr = slice(max(n - 1, 0) * blk, (n + 1) * blk)
        dst = slice((r4 * per4 + n) * blk, (r4 * per4 + n + 1) * blk)
        return (1, lambda: q4[qr, ln], lambda: k4[kr, ln], lambda: v4[kr, ln], dst)

    def p2_block(r):
        ln = slice(r * LANES, (r + 1) * LANES)
        dst = pl.ds((r % 4) * (seq // 4) + r // 4, blk, stride=4)
        return (2, lambda: q16[:, ln], lambda: k16[:, ln], lambda: v16[:, ln], dst)

    def combine(n):
        for r4 in range(4):
            rows = slice((r4 * per4 + n) * blk, (r4 * per4 + n + 1) * blk)
            nat = pl.ds(r4 + 4 * blk * n, blk, stride=4)
            at = (nat, rows, rows)
            ms = [macc[pi, at[pi], :] for pi in range(3)]
            mx = jnp.maximum(jnp.maximum(ms[0], ms[1]), ms[2])
            es = [jnp.exp2(m - mx) for m in ms]
            den = es[0] * lacc[0, nat, :] + es[1] * lacc[1, rows, :] + es[2] * lacc[2, rows, :]
            num = es[0] * oacc[0, nat, :] + es[1] * oacc[1, rows, :] + es[2] * oacc[2, rows, :]
            onat[nat, :] = num * (1.0 / den)
        done = slice(4 * blk * n, 4 * blk * (n + 1))
        o_ref[done, :] = onat[done, :].astype(o_ref.dtype)

    work = [p2_block(r) for r in range(nblk)]
    after = {}
    for n in range(per4):
        work += [p0_block(4 * n + a) for a in range(4)] + [p1_block(r4, n) for r4 in range(4)]
        after[len(work) - 1] = n

    pending = {}
    for t in range(len(work) + ATTN_LOOKAHEAD):
        if t < len(work):
            pi, q_at, k_at, _, _ = work[t]
            pending[t] = scores(pi, q_at, k_at)
        b = t - ATTN_LOOKAHEAD
        if b >= 0:
            pi, _, _, v_at, dst = work[b]
            finish(pending.pop(b), pi, v_at, dst)
            if b in after:
                combine(after[b])


def _attention(nat, mod4, mod16, slopes, *, batch, seq):
    h = N_ATTN_HEADS
    specs = [pl.BlockSpec((None, seq, LANES), lambda b, hh, t=t: (t * h + hh, b, 0))
             for t in range(3)]
    for d in DILATIONS[1:]:
        specs += [pl.BlockSpec((None, None, seq // d, d * LANES),
                               lambda b, hh, t=t: (t * h + hh, b, 0, 0)) for t in range(3)]
    return pl.pallas_call(
        functools.partial(_attn_kernel, seq=seq),
        grid=(batch, h),
        in_specs=[pl.BlockSpec(memory_space=pltpu.SMEM)] + specs,
        out_specs=pl.BlockSpec((None, seq, LANES), lambda b, hh: (hh, b, 0)),
        out_shape=jax.ShapeDtypeStruct((h, batch * seq, LANES), BF16),
        scratch_shapes=[
            pltpu.VMEM((3, ATTN_BLK, 2 * ATTN_BLK), F32),
            pltpu.VMEM((3, seq, LANES), F32),
            pltpu.VMEM((3, seq, LANES), F32),
            pltpu.VMEM((3, seq, LANES), F32),
            pltpu.VMEM((seq, LANES), F32),
        ],
        compiler_params=pltpu.CompilerParams(
            dimension_semantics=("parallel", "parallel"),
            vmem_limit_bytes=VMEM_LIMIT),
        name="dilated_attention",
    )(slopes, nat, nat, nat, mod4, mod4, mod4, mod16, mod16, mod16)


def _ret_kernel(lg_ref, q_ref, k_ref, v_ref, g_ref, o_ref, decay, zeta, xi, states, *, seq):
    c = RET_CHUNK
    dh = RET_HEAD_DIM
    nc = seq // c
    lg = lg_ref[pl.program_id(1)]
    k_scale = 1.0 / math.sqrt(dh)
    assert math.log2(k_scale).is_integer()

    ii = lax.broadcasted_iota(jnp.int32, (c, c), 0)
    jj = lax.broadcasted_iota(jnp.int32, (c, c), 1)
    dif = (ii - jj).astype(F32)
    decay[...] = jnp.where(dif >= 0, jnp.exp(lg * jnp.maximum(dif, 0.0)), 0.0) * k_scale
    idx = lax.broadcasted_iota(jnp.int32, (c, dh), 0).astype(F32)
    zeta[...] = jnp.exp(lg * (c - 1.0 - idx)) * k_scale
    xi[...] = jnp.exp(lg * (idx + 1.0))
    gamma_chunk = jnp.exp(jnp.full((dh, dh), lg * c, F32))

    def wide(ref, n):
        rows = slice(n * c, (n + 1) * c)
        return jnp.concatenate([ref[0, rows, :], ref[1, rows, :]], axis=-1)

    st = jnp.zeros((dh, dh), F32)
    for n in range(nc):
        states[n] = st.astype(BF16)
        if n + 1 < nc:
            kz = (wide(k_ref, n).astype(F32) * zeta[...]).astype(BF16)
            kv = lax.dot_general(kz, wide(v_ref, n), (((0,), (0,)), ((), ())),
                                 preferred_element_type=F32)
            st = st * gamma_chunk + kv

    def front(n):
        qn = wide(q_ref, n)
        sc = lax.dot_general(qn, wide(k_ref, n), (((1,), (1,)), ((), ())),
                             preferred_element_type=F32) * decay[...]
        cross = jnp.dot(qn, states[n], preferred_element_type=F32) * xi[...]
        return sc.astype(BF16), cross

    def back(n, sc, cross):
        ret = jnp.dot(sc, wide(v_ref, n), preferred_element_type=F32) + cross
        ret = ret * lax.rsqrt(jnp.mean(ret * ret, axis=-1, keepdims=True) + NORM_EPS)
        gate = wide(g_ref, n).astype(F32)
        out = (gate * jax.nn.sigmoid(gate) * ret).astype(o_ref.dtype)
        o_ref[0, n * c:(n + 1) * c, :] = out[:, :LANES]
        o_ref[1, n * c:(n + 1) * c, :] = out[:, LANES:]

    pending = {}
    for t in range(nc + RET_LOOKAHEAD):
        if t < nc:
            pending[t] = front(t)
        if t >= RET_LOOKAHEAD:
            back(t - RET_LOOKAHEAD, *pending.pop(t - RET_LOOKAHEAD))


def _retention(proj_hm, log_gamma, *, batch, seq):
    h = N_RET_HEADS
    blk = (2, seq, LANES)
    return pl.pallas_call(
        functools.partial(_ret_kernel, seq=seq),
        grid=(batch, h),
        in_specs=[
            pl.BlockSpec(memory_space=pltpu.SMEM),
            pl.BlockSpec(blk, lambda b, hh: (hh, b, 0)),
            pl.BlockSpec(blk, lambda b, hh: (h + hh, b, 0)),
            pl.BlockSpec(blk, lambda b, hh: (2 * h + hh, b, 0)),
            pl.BlockSpec(blk, lambda b, hh: (3 * h + hh, b, 0)),
        ],
        out_specs=pl.BlockSpec(blk, lambda b, hh: (hh, b, 0)),
        out_shape=jax.ShapeDtypeStruct((2 * h, batch * seq, LANES), BF16),
        scratch_shapes=[
            pltpu.VMEM((RET_CHUNK, RET_CHUNK), F32),
            pltpu.VMEM((RET_CHUNK, RET_HEAD_DIM), F32),
            pltpu.VMEM((RET_CHUNK, RET_HEAD_DIM), F32),
            pltpu.VMEM((seq // RET_CHUNK, RET_HEAD_DIM, RET_HEAD_DIM), BF16),
        ],
        compiler_params=pltpu.CompilerParams(
            dimension_semantics=("parallel", "parallel"),
            vmem_limit_bytes=VMEM_LIMIT),
        name="retention",
    )(log_gamma, proj_hm, proj_hm, proj_hm, proj_hm)


def _outproj_kernel(x_ref, a_ref, r_ref, wa_ref, wr_ref, o_ref):
    na = a_ref.shape[0]
    nr = r_ref.shape[0]
    a = jnp.concatenate([a_ref[i] for i in range(na)], axis=-1)
    r = jnp.concatenate([r_ref[i] for i in range(nr)], axis=-1)
    y = jnp.dot(a, wa_ref[...], preferred_element_type=F32)
    y = y + jnp.dot(r, wr_ref[...], preferred_element_type=F32)
    o_ref[...] = x_ref[...] + y


def _out_proj(x2d, attn_hm, ret_hm, w_out_bf16, *, tm=512):
    m, d = x2d.shape
    na, nr = attn_hm.shape[0], ret_hm.shape[0]
    assert na == nr
    return pl.pallas_call(
        _outproj_kernel,
        grid=(m // tm,),
        in_specs=[
            pl.BlockSpec((tm, d), lambda i: (i, 0)),
            pl.BlockSpec((na, tm, LANES), lambda i: (0, i, 0)),
            pl.BlockSpec((nr, tm, LANES), lambda i: (0, i, 0)),
            pl.BlockSpec((na * LANES, d), lambda i: (0, 0)),
            pl.BlockSpec((nr * LANES, d), lambda i: (1, 0)),
        ],
        out_specs=pl.BlockSpec((tm, d), lambda i: (i, 0)),
        out_shape=jax.ShapeDtypeStruct((m, d), F32),
        compiler_params=pltpu.CompilerParams(
            dimension_semantics=("parallel",),
            vmem_limit_bytes=VMEM_LIMIT),
        name="out_proj",
    )(x2d, attn_hm, ret_hm, w_out_bf16, w_out_bf16)


def _ffn_kernel(x_ref, nw_ref, wg_ref, wu_ref, wd_ref, fw_ref, o_ref, h_ref, acc_ref,
                *, tm, final_norm):
    f = pl.program_id(1)

    @pl.when(f == 0)
    def _():
        def body(i, c):
            r = pl.multiple_of(i * NORM_ROWS, NORM_ROWS)
            h_ref[pl.ds(r, NORM_ROWS), :] = _rms_rows(
                x_ref[pl.ds(r, NORM_ROWS), :], nw_ref[...]).astype(h_ref.dtype)
            acc_ref[pl.ds(r, NORM_ROWS), :] = jnp.zeros((NORM_ROWS, acc_ref.shape[1]), F32)
            return c
        lax.fori_loop(0, tm // NORM_ROWS, body, 0)

    h = h_ref[...]
    g = jnp.dot(h, wg_ref[...], preferred_element_type=F32)
    u = jnp.dot(h, wu_ref[...], preferred_element_type=F32)
    a = (g * jax.nn.sigmoid(g) * u).astype(BF16)
    y = jnp.dot(a, wd_ref[...], preferred_element_type=F32)
    acc_ref[...] += y

    @pl.when(f == pl.num_programs(1) - 1)
    def _():
        def body(i, c):
            r = pl.multiple_of(i * NORM_ROWS, NORM_ROWS)
            x2 = x_ref[pl.ds(r, NORM_ROWS), :] + acc_ref[pl.ds(r, NORM_ROWS), :]
            if final_norm:
                x2 = _rms_rows(x2, fw_ref[...])
            o_ref[pl.ds(r, NORM_ROWS), :] = x2
            return c
        lax.fori_loop(0, tm // NORM_ROWS, body, 0)


def _ffn(x2d, norm_w, wg, wu, wd, final_w, *, final_norm, tm=512, tf=512):
    m, d = x2d.shape
    hid = wg.shape[1]
    return pl.pallas_call(
        functools.partial(_ffn_kernel, tm=tm, final_norm=final_norm),
        grid=(m // tm, hid // tf),
        in_specs=[
            pl.BlockSpec((tm, d), lambda i, f: (i, 0)),
            pl.BlockSpec((1, d), lambda i, f: (0, 0)),
            pl.BlockSpec((d, tf), lambda i, f: (0, f)),
            pl.BlockSpec((d, tf), lambda i, f: (0, f)),
            pl.BlockSpec((tf, d), lambda i, f: (f, 0)),
            pl.BlockSpec((1, d), lambda i, f: (0, 0)),
        ],
        out_specs=pl.BlockSpec((tm, d), lambda i, f: (i, 0)),
        out_shape=jax.ShapeDtypeStruct((m, d), F32),
        scratch_shapes=[pltpu.VMEM((tm, d), BF16), pltpu.VMEM((tm, d), F32)],
        compiler_params=pltpu.CompilerParams(
            dimension_semantics=("parallel", "arbitrary"),
            vmem_limit_bytes=VMEM_LIMIT),
        name="ffn",
    )(x2d, norm_w.reshape(1, d), wg, wu, wd, final_w.reshape(1, d))


def kernel(x, norm_mix_w, w_in, w_out, norm_ffn_w, w_gate, w_up, w_down, norm_final_w):
    batch, seq, d = x.shape
    depth = w_in.shape[0]
    assert seq == DILATIONS[-1] * ATTN_BLK and seq % RET_CHUNK == 0
    slopes = jnp.exp2(-8.0 * jnp.arange(1, N_ATTN_HEADS + 1, dtype=F32) / N_ATTN_HEADS)
    log_gamma = jnp.log(1.0 - jnp.exp2(-5.0 - jnp.arange(N_RET_HEADS, dtype=F32)))

    xs = x.reshape(batch * seq, d)
    for layer in range(depth):
        w_in_l = w_in[layer].astype(BF16)
        side = ((w_gate[layer], 0, FFN_CAST_BLOCKS), (w_up[layer], 0, FFN_CAST_BLOCKS),
                (w_down[layer], FFN_CAST_BLOCKS, FFN_CAST_BLOCKS // 2),
                (w_out[layer], FFN_CAST_BLOCKS, FFN_CAST_BLOCKS // 2))
        nat, mod4, mod16, proj_ret, wg, wu, wd, wo = _in_proj(
            xs, norm_mix_w[layer], w_in_l, side, batch=batch, seq=seq)
        attn = _attention(nat, mod4, mod16, slopes, batch=batch, seq=seq)
        ret = _retention(proj_ret, log_gamma, batch=batch, seq=seq)
        xs = _out_proj(xs, attn, ret, wo)
        xs = _ffn(xs, norm_ffn_w[layer], wg, wu, wd, norm_final_w,
                  final_norm=(layer == depth - 1))
    return xs.reshape(batch, seq, d)
```

```python
import functools
import math

import jax
import jax.numpy as jnp
from jax import lax
from jax.experimental import pallas as pl
from jax.experimental.pallas import tpu as pltpu

F32 = jnp.float32
BF16 = jnp.bfloat16

LANES = 128
ATTN_HEAD_DIM = 128
N_ATTN_HEADS = 8
RET_HEAD_DIM = 256
N_RET_HEADS = 4
RET_CHUNK = 128
ATTN_BLK = 128
DILATIONS = (1, 4, 16)
ATTN_LOOKAHEAD = 8
RET_LOOKAHEAD = 2
NORM_EPS = 1e-6
MASK_VALUE = -1e30
LOG2E = math.log2(math.e)
QK_SCALE = LOG2E / math.sqrt(ATTN_HEAD_DIM)
NORM_ROWS = 128
FFN_CAST_BLOCKS = 64
VMEM_LIMIT = 60 * 1024 * 1024


def _rms_rows(x, w):
    ms = jnp.mean(x * x, axis=-1, keepdims=True)
    return x * lax.rsqrt(ms + NORM_EPS) * w


def _inproj_kernel(*refs, tm, tn, attn_tiles, q_tiles, casts):
    nc = len(casts)
    x_ref, nw_ref, w_ref = refs[:3]
    cast_in = refs[3:3 + nc]
    o1_ref, o4_ref, o16_ref, ret_ref = refs[3 + nc:7 + nc]
    cast_out = refs[7 + nc:7 + 2 * nc]
    h_even, h_odd, nat, mod4 = refs[7 + 2 * nc:]
    i = pl.program_id(0)
    j = pl.program_id(1)
    xr = x_ref.shape[0]
    n4, n16 = tm // 4, tm // 16

    step_no = (i - 1) * pl.num_programs(1) + j
    for src, dst, (first, count) in zip(cast_in, cast_out, casts):
        @pl.when((step_no >= first) & (step_no < first + count))
        def _(src=src, dst=dst):
            dst[...] = src[...].astype(dst.dtype)

    def norm_chunk(h_ref):
        r = pl.multiple_of(jnp.minimum(j, tm // xr - 1) * xr, xr)
        h_ref[pl.ds(r, xr), :] = _rms_rows(x_ref[...], nw_ref[...]).astype(BF16)

    def step(h_wr, h_rd, store):
        norm_chunk(h_wr)
        acc = jnp.dot(h_rd[...], w_ref[...], preferred_element_type=F32)
        for c in range(tn // LANES):
            store(c, acc[:, c * LANES:(c + 1) * LANES])

    @pl.when(i == 0)
    def _():
        norm_chunk(h_even)

    def store_attn(c, slab):
        slab = slab * jnp.where(j < q_tiles, QK_SCALE, 1.0)
        o1_ref[c] = slab.astype(o1_ref.dtype)
        b = c % nat.shape[0]
        nat[b] = slab
        for r in range(4):
            rows = nat[b, pl.ds(r, n4, stride=4), :]
            mod4[b, r * n4:(r + 1) * n4, :] = rows
            o4_ref[c, :, r * LANES:(r + 1) * LANES] = rows.astype(o4_ref.dtype)
        for r in range(16):
            a, r4 = divmod(r, 4)
            rows = mod4[b, pl.ds(r4 * n4 + a, n16, stride=4), :]
            o16_ref[c, :, r * LANES:(r + 1) * LANES] = rows.astype(o16_ref.dtype)

    def store_ret(c, slab):
        ret_ref[c] = slab.astype(ret_ref.dtype)

    for parity, (h_wr, h_rd) in enumerate(((h_even, h_odd), (h_odd, h_even))):
        mine = (i > 0) & (i % 2 == parity)
        pl.when(mine & (j < attn_tiles))(functools.partial(step, h_wr, h_rd, store_attn))
        pl.when(mine & (j >= attn_tiles))(functools.partial(step, h_wr, h_rd, store_ret))


def _in_proj(x2d, norm_w, w_bf16, side_f32, *, batch, seq, tm=1024, tn=1024):
    m, d = x2d.shape
    n = w_bf16.shape[1]
    n_attn = 3 * N_ATTN_HEADS * ATTN_HEAD_DIM
    ns = tn // LANES
    tiles = m // tm
    nj = n // tn
    attn_tiles = n_attn // tn
    assert 0 < attn_tiles < nj
    chunks = tm // NORM_ROWS
    while chunks > nj:
        chunks //= 2
    xr = tm // chunks
    per_b = seq // tm
    q_tiles = N_ATTN_HEADS * ATTN_HEAD_DIM // tn

    def t_of(i):
        return jnp.maximum(i - 1, 0)

    def ja(i, j):
        return jnp.where(i == 0, 0, jnp.minimum(j, attn_tiles - 1))

    def jr(i, j):
        return jnp.where(i == 0, 0, jnp.maximum(j - attn_tiles, 0))

    casts, cast_specs, cast_shapes = [], [], []
    for arr, first, blocks in side_f32:
        rows, cols = arr.shape
        assert rows % (blocks * 16) == 0
        assert first + blocks <= tiles * nj

        def at(i, j, first=first, blocks=blocks):
            return (jnp.clip((i - 1) * nj + j - first, 0, blocks - 1), 0)

        casts.append((first, blocks))
        cast_specs.append(pl.BlockSpec((rows // blocks, cols), at))
        cast_shapes.append(jax.ShapeDtypeStruct(arr.shape, BF16))

    return pl.pallas_call(
        functools.partial(_inproj_kernel, tm=tm, tn=tn, attn_tiles=attn_tiles, q_tiles=q_tiles,
                          casts=tuple(casts)),
        grid=(tiles + 1, nj),
        in_specs=[
            pl.BlockSpec((xr, d), lambda i, j: (
                jnp.minimum(i, tiles - 1) * chunks + jnp.minimum(j, chunks - 1), 0)),
            pl.BlockSpec((1, d), lambda i, j: (0, 0)),
            pl.BlockSpec((d, tn), lambda i, j: (0, jnp.where(i == 0, 0, j))),
        ] + cast_specs,
        out_specs=[
            pl.BlockSpec((ns, tm, LANES), lambda i, j: (ja(i, j), t_of(i), 0)),
            pl.BlockSpec((ns, None, tm // 4, 4 * LANES),
                         lambda i, j: (ja(i, j), t_of(i) // per_b, t_of(i) % per_b, 0)),
            pl.BlockSpec((ns, None, tm // 16, 16 * LANES),
                         lambda i, j: (ja(i, j), t_of(i) // per_b, t_of(i) % per_b, 0)),
            pl.BlockSpec((ns, tm, LANES), lambda i, j: (jr(i, j), t_of(i), 0)),
        ] + cast_specs,
        out_shape=[
            jax.ShapeDtypeStruct((n_attn // LANES, m, LANES), BF16),
            jax.ShapeDtypeStruct((n_attn // LANES, batch, seq // 4, 4 * LANES), BF16),
            jax.ShapeDtypeStruct((n_attn // LANES, batch, seq // 16, 16 * LANES), BF16),
            jax.ShapeDtypeStruct(((n - n_attn) // LANES, m, LANES), BF16),
        ] + cast_shapes,
        scratch_shapes=[pltpu.VMEM((tm, d), BF16),
                        pltpu.VMEM((tm, d), BF16),
                        pltpu.VMEM((2, tm, LANES), F32),
                        pltpu.VMEM((2, tm, LANES), F32)],
        compiler_params=pltpu.CompilerParams(
            dimension_semantics=("arbitrary", "arbitrary"),
            vmem_limit_bytes=VMEM_LIMIT),
        name="in_proj",
    )(x2d, norm_w.reshape(1, d), w_bf16, *[arr for arr, _, _ in side_f32])


def _attn_kernel(slopes_ref, q1, k1, v1, q4, k4, v4, q16, k16, v16, o_ref,
                 bm, macc, lacc, oacc, onat, *, seq):
    blk = ATTN_BLK
    nblk = seq // blk
    per4 = nblk // 4
    slope = slopes_ref[pl.program_id(1)]

    qi = lax.broadcasted_iota(jnp.int32, (blk, 2 * blk), 0)
    kj = lax.broadcasted_iota(jnp.int32, (blk, 2 * blk), 1)
    diff = qi - kj + blk
    valid = (diff >= 0) & (diff <= blk)
    dist = diff.astype(F32)
    for pi, dil in enumerate(DILATIONS):
        bm[pi] = jnp.where(valid, (-LOG2E * slope) * (dist * float(dil)), MASK_VALUE)

    def scores(pi, q_at, k_at):
        kk = k_at()
        bias = bm[pi] if kk.shape[0] == 2 * blk else bm[pi, :, blk:]
        return lax.dot_general(q_at(), kk, (((1,), (1,)), ((), ())),
                               preferred_element_type=F32) + bias

    def finish(s, pi, v_at, dst):
        mx = jnp.max(s, axis=-1, keepdims=True)
        p = jnp.exp2(s - mx).astype(BF16)
        vv = v_at()
        acc = jnp.dot(p, jnp.concatenate([vv, jnp.ones_like(vv)], axis=-1),
                      preferred_element_type=F32)
        oacc[pi, dst, :] = acc[:, :LANES]
        lacc[pi, dst, :] = acc[:, LANES:]
        macc[pi, dst, :] = jnp.broadcast_to(mx, (blk, LANES))

    def p0_block(i):
        rows = slice(i * blk, (i + 1) * blk)
        kr = slice(max(i - 1, 0) * blk, (i + 1) * blk)
        return (0, lambda: q1[rows, :], lambda: k1[kr, :], lambda: v1[kr, :], rows)

    def p1_block(r4, n):
        ln = slice(r4 * LANES, (r4 + 1) * LANES)
        qr = slice(n * blk, (n + 1) * blk)
        kr = slice(max(n - 1, 0) * blk, (n + 1) * blk)
        dst = slice((r4 * per4 + n) * blk, (r4 * per4 + n + 1) * blk)
        return (1, lambda: q4[qr, ln], lambda: k4[kr, ln], lambda: v4[kr, ln], dst)

    def p2_block(r):
        ln = slice(r * LANES, (r + 1) * LANES)
        dst = pl.ds((r % 4) * (seq // 4) + r // 4, blk, stride=4)
        return (2, lambda: q16[:, ln], lambda: k16[:, ln], lambda: v16[:, ln], dst)

    def combine(n):
        for r4 in range(4):
            rows = slice((r4 * per4 + n) * blk, (r4 * per4 + n + 1) * blk)
            nat = pl.ds(r4 + 4 * blk * n, blk, stride=4)
            at = (nat, rows, rows)
            ms = [macc[pi, at[pi], :] for pi in range(3)]
            mx = jnp.maximum(jnp.maximum(ms[0], ms[1]), ms[2])
            es = [jnp.exp2(m - mx) for m in ms]
            den = es[0] * lacc[0, nat, :] + es[1] * lacc[1, rows, :] + es[2] * lacc[2, rows, :]
            num = es[0] * oacc[0, nat, :] + es[1] * oacc[1, rows, :] + es[2] * oacc[2, rows, :]
            onat[nat, :] = num * (1.0 / den)
        done = slice(4 * blk * n, 4 * blk * (n + 1))
        o_ref[done, :] = onat[done, :].astype(o_ref.dtype)

    work = [p2_block(r) for r in range(nblk)]
    after = {}
    for n in range(per4):
        work += [p0_block(4 * n + a) for a in range(4)] + [p1_block(r4, n) for r4 in range(4)]
        after[len(work) - 1] = n

    pending = {}
    for t in range(len(work) + ATTN_LOOKAHEAD):
        if t < len(work):
            pi, q_at, k_at, _, _ = work[t]
            pending[t] = scores(pi, q_at, k_at)
        b = t - ATTN_LOOKAHEAD
        if b >= 0:
            pi, _, _, v_at, dst = work[b]
            finish(pending.pop(b), pi, v_at, dst)
            if b in after:
                combine(after[b])


def _attention(nat, mod4, mod16, slopes, *, batch, seq):
    h = N_ATTN_HEADS
    specs = [pl.BlockSpec((None, seq, LANES), lambda b, hh, t=t: (t * h + hh, b, 0))
             for t in range(3)]
    for d in DILATIONS[1:]:
        specs += [pl.BlockSpec((None, None, seq // d, d * LANES),
                               lambda b, hh, t=t: (t * h + hh, b, 0, 0)) for t in range(3)]
    return pl.pallas_call(
        functools.partial(_attn_kernel, seq=seq),
        grid=(batch, h),
        in_specs=[pl.BlockSpec(memory_space=pltpu.SMEM)] + specs,
        out_specs=pl.BlockSpec((None, seq, LANES), lambda b, hh: (hh, b, 0)),
        out_shape=jax.ShapeDtypeStruct((h, batch * seq, LANES), BF16),
        scratch_shapes=[
            pltpu.VMEM((3, ATTN_BLK, 2 * ATTN_BLK), F32),
            pltpu.VMEM((3, seq, LANES), F32),
            pltpu.VMEM((3, seq, LANES), F32),
            pltpu.VMEM((3, seq, LANES), F32),
            pltpu.VMEM((seq, LANES), F32),
        ],
        compiler_params=pltpu.CompilerParams(
            dimension_semantics=("parallel", "parallel"),
            vmem_limit_bytes=VMEM_LIMIT),
        name="dilated_attention",
    )(slopes, nat, nat, nat, mod4, mod4, mod4, mod16, mod16, mod16)


def _ret_kernel(lg_ref, q_ref, k_ref, v_ref, g_ref, o_ref, decay, zeta, xi, states, *, seq):
    c = RET_CHUNK
    dh = RET_HEAD_DIM
    nc = seq // c
    lg = lg_ref[pl.program_id(1)]
    k_scale = 1.0 / math.sqrt(dh)
    assert math.log2(k_scale).is_integer()

    ii = lax.broadcasted_iota(jnp.int32, (c, c), 0)
    jj = lax.broadcasted_iota(jnp.int32, (c, c), 1)
    dif = (ii - jj).astype(F32)
    decay[...] = jnp.where(dif >= 0, jnp.exp(lg * jnp.maximum(dif, 0.0)), 0.0) * k_scale
    idx = lax.broadcasted_iota(jnp.int32, (c, dh), 0).astype(F32)
    zeta[...] = jnp.exp(lg * (c - 1.0 - idx)) * k_scale
    xi[...] = jnp.exp(lg * (idx + 1.0))
    gamma_chunk = jnp.exp(jnp.full((dh, dh), lg * c, F32))

    def wide(ref, n):
        rows = slice(n * c, (n + 1) * c)
        return jnp.concatenate([ref[0, rows, :], ref[1, rows, :]], axis=-1)

    st = jnp.zeros((dh, dh), F32)
    for n in range(nc):
        states[n] = st.astype(BF16)
        if n + 1 < nc:
            kz = (wide(k_ref, n).astype(F32) * zeta[...]).astype(BF16)
            kv = lax.dot_general(kz, wide(v_ref, n), (((0,), (0,)), ((), ())),
                                 preferred_element_type=F32)
            st = st * gamma_chunk + kv

    def front(n):
        qn = wide(q_ref, n)
        sc = lax.dot_general(qn, wide(k_ref, n), (((1,), (1,)), ((), ())),
                             preferred_element_type=F32) * decay[...]
        cross = jnp.dot(qn, states[n], preferred_element_type=F32) * xi[...]
        return sc.astype(BF16), cross

    def back(n, sc, cross):
        ret = jnp.dot(sc, wide(v_ref, n), preferred_element_type=F32) + cross
        ret = ret * lax.rsqrt(jnp.mean(ret * ret, axis=-1, keepdims=True) + NORM_EPS)
        gate = wide(g_ref, n).astype(F32)
        out = (gate * jax.nn.sigmoid(gate) * ret).astype(o_ref.dtype)
        o_ref[0, n * c:(n + 1) * c, :] = out[:, :LANES]
        o_ref[1, n * c:(n + 1) * c, :] = out[:, LANES:]

    pending = {}
    for t in range(nc + RET_LOOKAHEAD):
        if t < nc:
            pending[t] = front(t)
        if t >= RET_LOOKAHEAD:
            back(t - RET_LOOKAHEAD, *pending.pop(t - RET_LOOKAHEAD))


def _retention(proj_hm, log_gamma, *, batch, seq):
    h = N_RET_HEADS
    blk = (2, seq, LANES)
    return pl.pallas_call(
        functools.partial(_ret_kernel, seq=seq),
        grid=(batch, h),
        in_specs=[
            pl.BlockSpec(memory_space=pltpu.SMEM),
            pl.BlockSpec(blk, lambda b, hh: (hh, b, 0)),
            pl.BlockSpec(blk, lambda b, hh: (h + hh, b, 0)),
            pl.BlockSpec(blk, lambda b, hh: (2 * h + hh, b, 0)),
            pl.BlockSpec(blk, lambda b, hh: (3 * h + hh, b, 0)),
        ],
        out_specs=pl.BlockSpec(blk, lambda b, hh: (hh, b, 0)),
        out_shape=jax.ShapeDtypeStruct((2 * h, batch * seq, LANES), BF16),
        scratch_shapes=[
            pltpu.VMEM((RET_CHUNK, RET_CHUNK), F32),
            pltpu.VMEM((RET_CHUNK, RET_HEAD_DIM), F32),
            pltpu.VMEM((RET_CHUNK, RET_HEAD_DIM), F32),
            pltpu.VMEM((seq // RET_CHUNK, RET_HEAD_DIM, RET_HEAD_DIM), BF16),
        ],
        compiler_params=pltpu.CompilerParams(
            dimension_semantics=("parallel", "parallel"),
            vmem_limit_bytes=VMEM_LIMIT),
        name="retention",
    )(log_gamma, proj_hm, proj_hm, proj_hm, proj_hm)


def _outproj_kernel(x_ref, a_ref, r_ref, wa_ref, wr_ref, o_ref):
    na = a_ref.shape[0]
    nr = r_ref.shape[0]
    a = jnp.concatenate([a_ref[i] for i in range(na)], axis=-1)
    r = jnp.concatenate([r_ref[i] for i in range(nr)], axis=-1)
    y = jnp.dot(a, wa_ref[...], preferred_element_type=F32)
    y = y + jnp.dot(r, wr_ref[...], preferred_element_type=F32)
    o_ref[...] = x_ref[...] + y


def _out_proj(x2d, attn_hm, ret_hm, w_out_bf16, *, tm=512):
    m, d = x2d.shape
    na, nr = attn_hm.shape[0], ret_hm.shape[0]
    assert na == nr
    return pl.pallas_call(
        _outproj_kernel,
        grid=(m // tm,),
        in_specs=[
            pl.BlockSpec((tm, d), lambda i: (i, 0)),
            pl.BlockSpec((na, tm, LANES), lambda i: (0, i, 0)),
            pl.BlockSpec((nr, tm, LANES), lambda i: (0, i, 0)),
            pl.BlockSpec((na * LANES, d), lambda i: (0, 0)),
            pl.BlockSpec((nr * LANES, d), lambda i: (1, 0)),
        ],
        out_specs=pl.BlockSpec((tm, d), lambda i: (i, 0)),
        out_shape=jax.ShapeDtypeStruct((m, d), F32),
        compiler_params=pltpu.CompilerParams(
            dimension_semantics=("parallel",),
            vmem_limit_bytes=VMEM_LIMIT),
        name="out_proj",
    )(x2d, attn_hm, ret_hm, w_out_bf16, w_out_bf16)


def _ffn_kernel(x_ref, nw_ref, wg_ref, wu_ref, wd_ref, fw_ref, o_ref, h_ref, acc_ref,
                *, tm, final_norm):
    f = pl.program_id(1)

    @pl.when(f == 0)
    def _():
        def body(i, c):
            r = pl.multiple_of(i * NORM_ROWS, NORM_ROWS)
            h_ref[pl.ds(r, NORM_ROWS), :] = _rms_rows(
                x_ref[pl.ds(r, NORM_ROWS), :], nw_ref[...]).astype(h_ref.dtype)
            acc_ref[pl.ds(r, NORM_ROWS), :] = jnp.zeros((NORM_ROWS, acc_ref.shape[1]), F32)
            return c
        lax.fori_loop(0, tm // NORM_ROWS, body, 0)

    h = h_ref[...]
    g = jnp.dot(h, wg_ref[...], preferred_element_type=F32)
    u = jnp.dot(h, wu_ref[...], preferred_element_type=F32)
    a = (g * jax.nn.sigmoid(g) * u).astype(BF16)
    y = jnp.dot(a, wd_ref[...], preferred_element_type=F32)
    acc_ref[...] += y

    @pl.when(f == pl.num_programs(1) - 1)
    def _():
        def body(i, c):
            r = pl.multiple_of(i * NORM_ROWS, NORM_ROWS)
            x2 = x_ref[pl.ds(r, NORM_ROWS), :] + acc_ref[pl.ds(r, NORM_ROWS), :]
            if final_norm:
                x2 = _rms_rows(x2, fw_ref[...])
            o_ref[pl.ds(r, NORM_ROWS), :] = x2
            return c
        lax.fori_loop(0, tm // NORM_ROWS, body, 0)


def _ffn(x2d, norm_w, wg, wu, wd, final_w, *, final_norm, tm=1024, tf=256):
    m, d = x2d.shape
    hid = wg.shape[1]
    return pl.pallas_call(
        functools.partial(_ffn_kernel, tm=tm, final_norm=final_norm),
        grid=(m // tm, hid // tf),
        in_specs=[
            pl.BlockSpec((tm, d), lambda i, f: (i, 0)),
            pl.BlockSpec((1, d), lambda i, f: (0, 0)),
            pl.BlockSpec((d, tf), lambda i, f: (0, f)),
            pl.BlockSpec((d, tf), lambda i, f: (0, f)),
            pl.BlockSpec((tf, d), lambda i, f: (f, 0)),
            pl.BlockSpec((1, d), lambda i, f: (0, 0)),
        ],
        out_specs=pl.BlockSpec((tm, d), lambda i, f: (i, 0)),
        out_shape=jax.ShapeDtypeStruct((m, d), F32),
        scratch_shapes=[pltpu.VMEM((tm, d), BF16), pltpu.VMEM((tm, d), F32)],
        compiler_params=pltpu.CompilerParams(
            dimension_semantics=("parallel", "arbitrary"),
            vmem_limit_bytes=VMEM_LIMIT),
        name="ffn",
    )(x2d, norm_w.reshape(1, d), wg, wu, wd, final_w.reshape(1, d))


def kernel(x, norm_mix_w, w_in, w_out, norm_ffn_w, w_gate, w_up, w_down, norm_final_w):
    batch, seq, d = x.shape
    depth = w_in.shape[0]
    assert seq == DILATIONS[-1] * ATTN_BLK and seq % RET_CHUNK == 0
    slopes = jnp.exp2(-8.0 * jnp.arange(1, N_ATTN_HEADS + 1, dtype=F32) / N_ATTN_HEADS)
    log_gamma = jnp.log(1.0 - jnp.exp2(-5.0 - jnp.arange(N_RET_HEADS, dtype=F32)))

    xs = x.reshape(batch * seq, d)
    for layer in range(depth):
        w_in_l = w_in[layer].astype(BF16)
        side = ((w_gate[layer], 0, FFN_CAST_BLOCKS), (w_up[layer], 0, FFN_CAST_BLOCKS),
                (w_down[layer], FFN_CAST_BLOCKS, FFN_CAST_BLOCKS // 2),
                (w_out[layer], FFN_CAST_BLOCKS, FFN_CAST_BLOCKS // 2))
        nat, mod4, mod16, proj_ret, wg, wu, wd, wo = _in_proj(
            xs, norm_mix_w[layer], w_in_l, side, batch=batch, seq=seq)
        attn = _attention(nat, mod4, mod16, slopes, batch=batch, seq=seq)
        ret = _retention(proj_ret, log_gamma, batch=batch, seq=seq)
        xs = _out_proj(xs, attn, ret, wo)
        xs = _ffn(xs, norm_ffn_w[layer], wg, wu, wd, norm_final_w,
                  final_norm=(layer == depth - 1))
    return xs.reshape(batch, seq, d)
```

```python
import functools
import math

import jax
import jax.numpy as jnp
from jax import lax
from jax.experimental import pallas as pl
from jax.experimental.pallas import tpu as pltpu

F32 = jnp.float32
BF16 = jnp.bfloat16

LANES = 128
ATTN_HEAD_DIM = 128
N_ATTN_HEADS = 8
RET_HEAD_DIM = 256
N_RET_HEADS = 4
RET_CHUNK = 128
ATTN_BLK = 128
DILATIONS = (1, 4, 16)
ATTN_LOOKAHEAD = 8
RET_LOOKAHEAD = 2
NORM_EPS = 1e-6
MASK_VALUE = -1e30
LOG2E = math.log2(math.e)
QK_SCALE = LOG2E / math.sqrt(ATTN_HEAD_DIM)
NORM_ROWS = 128
FFN_CAST_BLOCKS = 64
VMEM_LIMIT = 60 * 1024 * 1024


def _rms_rows(x, w):
    ms = jnp.mean(x * x, axis=-1, keepdims=True)
    return x * lax.rsqrt(ms + NORM_EPS) * w


def _inproj_kernel(*refs, tm, tn, attn_tiles, q_tiles, casts):
    nc = len(casts)
    x_ref, nw_ref, w_ref = refs[:3]
    cast_in = refs[3:3 + nc]
    o1_ref, o4_ref, o16_ref, ret_ref = refs[3 + nc:7 + nc]
    cast_out = refs[7 + nc:7 + 2 * nc]
    h_even, h_odd, nat, mod4 = refs[7 + 2 * nc:]
    i = pl.program_id(0)
    j = pl.program_id(1)
    xr = x_ref.shape[0]
    n4, n16 = tm // 4, tm // 16

    step_no = (i - 1) * pl.num_programs(1) + j
    for src, dst, (first, count) in zip(cast_in, cast_out, casts):
        @pl.when((step_no >= first) & (step_no < first + count))
        def _(src=src, dst=dst):
            dst[...] = src[...].astype(dst.dtype)

    def norm_chunk(h_ref):
        r = pl.multiple_of(jnp.minimum(j, tm // xr - 1) * xr, xr)
        h_ref[pl.ds(r, xr), :] = _rms_rows(x_ref[...], nw_ref[...]).astype(BF16)

    def step(h_wr, h_rd, store):
        norm_chunk(h_wr)
        acc = jnp.dot(h_rd[...], w_ref[...], preferred_element_type=F32)
        for c in range(tn // LANES):
            store(c, acc[:, c * LANES:(c + 1) * LANES])

    @pl.when(i == 0)
    def _():
        norm_chunk(h_even)

    def store_attn(c, slab):
        slab = slab * jnp.where(j < q_tiles, QK_SCALE, 1.0)
        o1_ref[c] = slab.astype(o1_ref.dtype)
        b = c % nat.shape[0]
        nat[b] = slab
        for r in range(4):
            rows = nat[b, pl.ds(r, n4, stride=4), :]
            mod4[b, r * n4:(r + 1) * n4, :] = rows
            o4_ref[c, :, r * LANES:(r + 1) * LANES] = rows.astype(o4_ref.dtype)
        for r in range(16):
            a, r4 = divmod(r, 4)
            rows = mod4[b, pl.ds(r4 * n4 + a, n16, stride=4), :]
            o16_ref[c, :, r * LANES:(r + 1) * LANES] = rows.astype(o16_ref.dtype)

    def store_ret(c, slab):
        ret_ref[c] = slab.astype(ret_ref.dtype)

    for parity, (h_wr, h_rd) in enumerate(((h_even, h_odd), (h_odd, h_even))):
        mine = (i > 0) & (i % 2 == parity)
        pl.when(mine & (j < attn_tiles))(functools.partial(step, h_wr, h_rd, store_attn))
        pl.when(mine & (j >= attn_tiles))(functools.partial(step, h_wr, h_rd, store_ret))


def _in_proj(x2d, norm_w, w_bf16, side_f32, *, batch, seq, tm=1024, tn=1024):
    m, d = x2d.shape
    n = w_bf16.shape[1]
    n_attn = 3 * N_ATTN_HEADS * ATTN_HEAD_DIM
    ns = tn // LANES
    tiles = m // tm
    nj = n // tn
    attn_tiles = n_attn // tn
    assert 0 < attn_tiles < nj
    chunks = tm // NORM_ROWS
    while chunks > nj:
        chunks //= 2
    xr = tm // chunks
    per_b = seq // tm
    q_tiles = N_ATTN_HEADS * ATTN_HEAD_DIM // tn

    def t_of(i):
        return jnp.maximum(i - 1, 0)

    def ja(i, j):
        return jnp.where(i == 0, 0, jnp.minimum(j, attn_tiles - 1))

    def jr(i, j):
        return jnp.where(i == 0, 0, jnp.maximum(j - attn_tiles, 0))

    casts, cast_specs, cast_shapes = [], [], []
    for arr, first, blocks in side_f32:
        rows, cols = arr.shape
        assert rows % (blocks * 16) == 0
        assert first + blocks <= tiles * nj

        def at(i, j, first=first, blocks=blocks):
            return (jnp.clip((i - 1) * nj + j - first, 0, blocks - 1), 0)

        casts.append((first, blocks))
        cast_specs.append(pl.BlockSpec((rows // blocks, cols), at))
        cast_shapes.append(jax.ShapeDtypeStruct(arr.shape, BF16))

    return pl.pallas_call(
        functools.partial(_inproj_kernel, tm=tm, tn=tn, attn_tiles=attn_tiles, q_tiles=q_tiles,
                          casts=tuple(casts)),
        grid=(tiles + 1, nj),
        in_specs=[
            pl.BlockSpec((xr, d), lambda i, j: (
                jnp.minimum(i, tiles - 1) * chunks + jnp.minimum(j, chunks - 1), 0)),
            pl.BlockSpec((1, d), lambda i, j: (0, 0)),
            pl.BlockSpec((d, tn), lambda i, j: (0, jnp.where(i == 0, 0, j))),
        ] + cast_specs,
        out_specs=[
            pl.BlockSpec((ns, tm, LANES), lambda i, j: (ja(i, j), t_of(i), 0)),
            pl.BlockSpec((ns, None, tm // 4, 4 * LANES),
                         lambda i, j: (ja(i, j), t_of(i) // per_b, t_of(i) % per_b, 0)),
            pl.BlockSpec((ns, None, tm // 16, 16 * LANES),
                         lambda i, j: (ja(i, j), t_of(i) // per_b, t_of(i) % per_b, 0)),
            pl.BlockSpec((ns, tm, LANES), lambda i, j: (jr(i, j), t_of(i), 0)),
        ] + cast_specs,
        out_shape=[
            jax.ShapeDtypeStruct((n_attn // LANES, m, LANES), BF16),
            jax.ShapeDtypeStruct((n_attn // LANES, batch, seq // 4, 4 * LANES), BF16),
            jax.ShapeDtypeStruct((n_attn // LANES, batch, seq // 16, 16 * LANES), BF16),
            jax.ShapeDtypeStruct(((n - n_attn) // LANES, m, LANES), BF16),
        ] + cast_shapes,
        scratch_shapes=[pltpu.VMEM((tm, d), BF16),
                        pltpu.VMEM((tm, d), BF16),
                        pltpu.VMEM((2, tm, LANES), F32),
                        pltpu.VMEM((2, tm, LANES), F32)],
        compiler_params=pltpu.CompilerParams(
            dimension_semantics=("arbitrary", "arbitrary"),
            vmem_limit_bytes=VMEM_LIMIT),
        name="in_proj",
    )(x2d, norm_w.reshape(1, d), w_bf16, *[arr for arr, _, _ in side_f32])


def _attn_kernel(slopes_ref, qkv1, qkv4, qkv16, o_ref, bm, macc, lacc, oacc, onat, *, seq):
    q1, k1, v1 = (qkv1.at[t] for t in range(3))
    q4, k4, v4 = (qkv4.at[t] for t in range(3))
    q16, k16, v16 = (qkv16.at[t] for t in range(3))
    blk = ATTN_BLK
    nblk = seq // blk
    per4 = nblk // 4
    slope = slopes_ref[pl.program_id(1)]

    qi = lax.broadcasted_iota(jnp.int32, (blk, 2 * blk), 0)
    kj = lax.broadcasted_iota(jnp.int32, (blk, 2 * blk), 1)
    diff = qi - kj + blk
    valid = (diff >= 0) & (diff <= blk)
    dist = diff.astype(F32)
    for pi, dil in enumerate(DILATIONS):
        bm[pi] = jnp.where(valid, (-LOG2E * slope) * (dist * float(dil)), MASK_VALUE)

    def scores(pi, q_at, k_at):
        kk = k_at()
        bias = bm[pi] if kk.shape[0] == 2 * blk else bm[pi, :, blk:]
        return lax.dot_general(q_at(), kk, (((1,), (1,)), ((), ())),
                               preferred_element_type=F32) + bias

    def finish(s, pi, v_at, dst):
        mx = jnp.max(s, axis=-1, keepdims=True)
        p = jnp.exp2(s - mx).astype(BF16)
        vv = v_at()
        acc = jnp.dot(p, jnp.concatenate([vv, jnp.ones_like(vv)], axis=-1),
                      preferred_element_type=F32)
        oacc[pi, dst, :] = acc[:, :LANES]
        lacc[pi, dst, :] = acc[:, LANES:]
        macc[pi, dst, :] = jnp.broadcast_to(mx, (blk, LANES))

    def p0_block(i):
        rows = slice(i * blk, (i + 1) * blk)
        kr = slice(max(i - 1, 0) * blk, (i + 1) * blk)
        return (0, lambda: q1[rows, :], lambda: k1[kr, :], lambda: v1[kr, :], rows)

    def p1_block(r4, n):
        ln = slice(r4 * LANES, (r4 + 1) * LANES)
        qr = slice(n * blk, (n + 1) * blk)
        kr = slice(max(n - 1, 0) * blk, (n + 1) * blk)
        dst = slice((r4 * per4 + n) * blk, (r4 * per4 + n + 1) * blk)
        return (1, lambda: q4[qr, ln], lambda: k4[kr, ln], lambda: v4[kr, ln], dst)

    def p2_block(r):
        ln = slice(r * LANES, (r + 1) * LANES)
        dst = pl.ds((r % 4) * (seq // 4) + r // 4, blk, stride=4)
        return (2, lambda: q16[:, ln], lambda: k16[:, ln], lambda: v16[:, ln], dst)

    def combine(n):
        for r4 in range(4):
            rows = slice((r4 * per4 + n) * blk, (r4 * per4 + n + 1) * blk)
            nat = pl.ds(r4 + 4 * blk * n, blk, stride=4)
            at = (nat, rows, rows)
            ms = [macc[pi, at[pi], :] for pi in range(3)]
            mx = jnp.maximum(jnp.maximum(ms[0], ms[1]), ms[2])
            es = [jnp.exp2(m - mx) for m in ms]
            den = es[0] * lacc[0, nat, :] + es[1] * lacc[1, rows, :] + es[2] * lacc[2, rows, :]
            num = es[0] * oacc[0, nat, :] + es[1] * oacc[1, rows, :] + es[2] * oacc[2, rows, :]
            onat[nat, :] = num * (1.0 / den)
        done = slice(4 * blk * n, 4 * blk * (n + 1))
        o_ref[done, :] = onat[done, :].astype(o_ref.dtype)

    work = [p2_block(r) for r in range(nblk)]
    after = {}
    for n in range(per4):
        work += [p0_block(4 * n + a) for a in range(4)] + [p1_block(r4, n) for r4 in range(4)]
        after[len(work) - 1] = n

    pending = {}
    for t in range(len(work) + ATTN_LOOKAHEAD):
        if t < len(work):
            pi, q_at, k_at, _, _ = work[t]
            pending[t] = scores(pi, q_at, k_at)
        b = t - ATTN_LOOKAHEAD
        if b >= 0:
            pi, _, _, v_at, dst = work[b]
            finish(pending.pop(b), pi, v_at, dst)
            if b in after:
                combine(after[b])


def _attention(nat, mod4, mod16, slopes, *, batch, seq):
    h = N_ATTN_HEADS
    views = [a.reshape((3, h) + a.shape[1:]) for a in (nat, mod4, mod16)]
    specs = [pl.BlockSpec((3, None, seq, LANES), lambda b, hh: (0, hh, b, 0))]
    for d in DILATIONS[1:]:
        specs.append(pl.BlockSpec((3, None, None, seq // d, d * LANES),
                                  lambda b, hh: (0, hh, b, 0, 0)))
    return pl.pallas_call(
        functools.partial(_attn_kernel, seq=seq),
        grid=(batch, h),
        in_specs=[pl.BlockSpec(memory_space=pltpu.SMEM)] + specs,
        out_specs=pl.BlockSpec((None, seq, LANES), lambda b, hh: (hh, b, 0)),
        out_shape=jax.ShapeDtypeStruct((h, batch * seq, LANES), BF16),
        scratch_shapes=[
            pltpu.VMEM((3, ATTN_BLK, 2 * ATTN_BLK), F32),
            pltpu.VMEM((3, seq, LANES), F32),
            pltpu.VMEM((3, seq, LANES), F32),
            pltpu.VMEM((3, seq, LANES), F32),
            pltpu.VMEM((seq, LANES), F32),
        ],
        compiler_params=pltpu.CompilerParams(
            dimension_semantics=("parallel", "parallel"),
            vmem_limit_bytes=VMEM_LIMIT),
        name="dilated_attention",
    )(slopes, *views)


def _ret_kernel(lg_ref, qkvg_ref, o_ref, decay, zeta, xi, states, *, seq):
    q_ref, k_ref, v_ref, g_ref = (qkvg_ref.at[t] for t in range(4))
    c = RET_CHUNK
    dh = RET_HEAD_DIM
    nc = seq // c
    lg = lg_ref[pl.program_id(1)]
    k_scale = 1.0 / math.sqrt(dh)
    assert math.log2(k_scale).is_integer()

    ii = lax.broadcasted_iota(jnp.int32, (c, c), 0)
    jj = lax.broadcasted_iota(jnp.int32, (c, c), 1)
    dif = (ii - jj).astype(F32)
    decay[...] = jnp.where(dif >= 0, jnp.exp(lg * jnp.maximum(dif, 0.0)), 0.0) * k_scale
    idx = lax.broadcasted_iota(jnp.int32, (c, dh), 0).astype(F32)
    zeta[...] = jnp.exp(lg * (c - 1.0 - idx)) * k_scale
    xi[...] = jnp.exp(lg * (idx + 1.0))
    gamma_chunk = jnp.exp(jnp.full((dh, dh), lg * c, F32))

    def wide(ref, n):
        rows = slice(n * c, (n + 1) * c)
        return jnp.concatenate([ref[0, rows, :], ref[1, rows, :]], axis=-1)

    st = jnp.zeros((dh, dh), F32)
    for n in range(nc):
        states[n] = st.astype(BF16)
        if n + 1 < nc:
            kz = (wide(k_ref, n).astype(F32) * zeta[...]).astype(BF16)
            kv = lax.dot_general(kz, wide(v_ref, n), (((0,), (0,)), ((), ())),
                                 preferred_element_type=F32)
            st = st * gamma_chunk + kv

    def front(n):
        qn = wide(q_ref, n)
        sc = lax.dot_general(qn, wide(k_ref, n), (((1,), (1,)), ((), ())),
                             preferred_element_type=F32) * decay[...]
        cross = jnp.dot(qn, states[n], preferred_element_type=F32) * xi[...]
        return sc.astype(BF16), cross

    def back(n, sc, cross):
        ret = jnp.dot(sc, wide(v_ref, n), preferred_element_type=F32) + cross
        ret = ret * lax.rsqrt(jnp.mean(ret * ret, axis=-1, keepdims=True) + NORM_EPS)
        gate = wide(g_ref, n).astype(F32)
        out = (gate * jax.nn.sigmoid(gate) * ret).astype(o_ref.dtype)
        o_ref[0, n * c:(n + 1) * c, :] = out[:, :LANES]
        o_ref[1, n * c:(n + 1) * c, :] = out[:, LANES:]

    pending = {}
    for t in range(nc + RET_LOOKAHEAD):
        if t < nc:
            pending[t] = front(t)
        if t >= RET_LOOKAHEAD:
            back(t - RET_LOOKAHEAD, *pending.pop(t - RET_LOOKAHEAD))


def _retention(proj_hm, log_gamma, *, batch, seq):
    h = N_RET_HEADS
    blk = (2, seq, LANES)
    view = proj_hm.reshape((4, h, 2) + proj_hm.shape[1:])
    return pl.pallas_call(
        functools.partial(_ret_kernel, seq=seq),
        grid=(batch, h),
        in_specs=[
            pl.BlockSpec(memory_space=pltpu.SMEM),
            pl.BlockSpec((4, None, 2, seq, LANES), lambda b, hh: (0, hh, 0, b, 0)),
        ],
        out_specs=pl.BlockSpec(blk, lambda b, hh: (hh, b, 0)),
        out_shape=jax.ShapeDtypeStruct((2 * h, batch * seq, LANES), BF16),
        scratch_shapes=[
            pltpu.VMEM((RET_CHUNK, RET_CHUNK), F32),
            pltpu.VMEM((RET_CHUNK, RET_HEAD_DIM), F32),
            pltpu.VMEM((RET_CHUNK, RET_HEAD_DIM), F32),
            pltpu.VMEM((seq // RET_CHUNK, RET_HEAD_DIM, RET_HEAD_DIM), BF16),
        ],
        compiler_params=pltpu.CompilerParams(
            dimension_semantics=("parallel", "parallel"),
            vmem_limit_bytes=VMEM_LIMIT),
        name="retention",
    )(log_gamma, view)


def _outproj_kernel(x_ref, a_ref, r_ref, wa_ref, wr_ref, o_ref):
    na = a_ref.shape[0]
    nr = r_ref.shape[0]
    a = jnp.concatenate([a_ref[i] for i in range(na)], axis=-1)
    r = jnp.concatenate([r_ref[i] for i in range(nr)], axis=-1)
    y = jnp.dot(a, wa_ref[...], preferred_element_type=F32)
    y = y + jnp.dot(r, wr_ref[...], preferred_element_type=F32)
    o_ref[...] = x_ref[...] + y


def _out_proj(x2d, attn_hm, ret_hm, w_out_bf16, *, tm=512):
    m, d = x2d.shape
    na, nr = attn_hm.shape[0], ret_hm.shape[0]
    assert na == nr
    return pl.pallas_call(
        _outproj_kernel,
        grid=(m // tm,),
        in_specs=[
            pl.BlockSpec((tm, d), lambda i: (i, 0)),
            pl.BlockSpec((na, tm, LANES), lambda i: (0, i, 0)),
            pl.BlockSpec((nr, tm, LANES), lambda i: (0, i, 0)),
            pl.BlockSpec((na * LANES, d), lambda i: (0, 0)),
            pl.BlockSpec((nr * LANES, d), lambda i: (1, 0)),
        ],
        out_specs=pl.BlockSpec((tm, d), lambda i: (i, 0)),
        out_shape=jax.ShapeDtypeStruct((m, d), F32),
        compiler_params=pltpu.CompilerParams(
            dimension_semantics=("parallel",),
            vmem_limit_bytes=VMEM_LIMIT),
        name="out_proj",
    )(x2d, attn_hm, ret_hm, w_out_bf16, w_out_bf16)


def _ffn_kernel(x_ref, nw_ref, wg_ref, wu_ref, wd_ref, fw_ref, o_ref, h_ref, acc_ref,
                *, tm, final_norm):
    f = pl.program_id(1)

    @pl.when(f == 0)
    def _():
        def body(i, c):
            r = pl.multiple_of(i * NORM_ROWS, NORM_ROWS)
            h_ref[pl.ds(r, NORM_ROWS), :] = _rms_rows(
                x_ref[pl.ds(r, NORM_ROWS), :], nw_ref[...]).astype(h_ref.dtype)
            acc_ref[pl.ds(r, NORM_ROWS), :] = jnp.zeros((NORM_ROWS, acc_ref.shape[1]), F32)
            return c
        lax.fori_loop(0, tm // NORM_ROWS, body, 0)

    h = h_ref[...]
    g = jnp.dot(h, wg_ref[...], preferred_element_type=F32)
    u = jnp.dot(h, wu_ref[...], preferred_element_type=F32)
    a = (g * jax.nn.sigmoid(g) * u).astype(BF16)
    y = jnp.dot(a, wd_ref[...], preferred_element_type=F32)
    acc_ref[...] += y

    @pl.when(f == pl.num_programs(1) - 1)
    def _():
        def body(i, c):
            r = pl.multiple_of(i * NORM_ROWS, NORM_ROWS)
            x2 = x_ref[pl.ds(r, NORM_ROWS), :] + acc_ref[pl.ds(r, NORM_ROWS), :]
            if final_norm:
                x2 = _rms_rows(x2, fw_ref[...])
            o_ref[pl.ds(r, NORM_ROWS), :] = x2
            return c
        lax.fori_loop(0, tm // NORM_ROWS, body, 0)


def _ffn(x2d, norm_w, wg, wu, wd, final_w, *, final_norm, tm=1024, tf=256):
    m, d = x2d.shape
    hid = wg.shape[1]
    return pl.pallas_call(
        functools.partial(_ffn_kernel, tm=tm, final_norm=final_norm),
        grid=(m // tm, hid // tf),
        in_specs=[
            pl.BlockSpec((tm, d), lambda i, f: (i, 0)),
            pl.BlockSpec((1, d), lambda i, f: (0, 0)),
            pl.BlockSpec((d, tf), lambda i, f: (0, f)),
            pl.BlockSpec((d, tf), lambda i, f: (0, f)),
            pl.BlockSpec((tf, d), lambda i, f: (f, 0)),
            pl.BlockSpec((1, d), lambda i, f: (0, 0)),
        ],
        out_specs=pl.BlockSpec((tm, d), lambda i, f: (i, 0)),
        out_shape=jax.ShapeDtypeStruct((m, d), F32),
        scratch_shapes=[pltpu.VMEM((tm, d), BF16), pltpu.VMEM((tm, d), F32)],
        compiler_params=pltpu.CompilerParams(
            dimension_semantics=("parallel", "arbitrary"),
            vmem_limit_bytes=VMEM_LIMIT),
        name="ffn",
    )(x2d, norm_w.reshape(1, d), wg, wu, wd, final_w.reshape(1, d))


def kernel(x, norm_mix_w, w_in, w_out, norm_ffn_w, w_gate, w_up, w_down, norm_final_w):
    batch, seq, d = x.shape
    depth = w_in.shape[0]
    assert seq == DILATIONS[-1] * ATTN_BLK and seq % RET_CHUNK == 0
    slopes = jnp.exp2(-8.0 * jnp.arange(1, N_ATTN_HEADS + 1, dtype=F32) / N_ATTN_HEADS)
    log_gamma = jnp.log(1.0 - jnp.exp2(-5.0 - jnp.arange(N_RET_HEADS, dtype=F32)))

    xs = x.reshape(batch * seq, d)
    for layer in range(depth):
        w_in_l = w_in[layer].astype(BF16)
        side = ((w_gate[layer], 0, FFN_CAST_BLOCKS), (w_up[layer], 0, FFN_CAST_BLOCKS),
                (w_down[layer], FFN_CAST_BLOCKS, FFN_CAST_BLOCKS // 2),
                (w_out[layer], FFN_CAST_BLOCKS, FFN_CAST_BLOCKS // 2))
        nat, mod4, mod16, proj_ret, wg, wu, wd, wo = _in_proj(
            xs, norm_mix_w[layer], w_in_l, side, batch=batch, seq=seq)
        attn = _attention(nat, mod4, mod16, slopes, batch=batch, seq=seq)
        ret = _retention(proj_ret, log_gamma, batch=batch, seq=seq)
        xs = _out_proj(xs, attn, ret, wo)
        xs = _ffn(xs, norm_ffn_w[layer], wg, wu, wd, norm_final_w,
                  final_norm=(layer == depth - 1))
    return xs.reshape(batch, seq, d)
```

```python
import functools
import math

import jax
import jax.numpy as jnp
from jax import lax
from jax.experimental import pallas as pl
from jax.experimental.pallas import tpu as pltpu

F32 = jnp.float32
BF16 = jnp.bfloat16

LANES = 128
ATTN_HEAD_DIM = 128
N_ATTN_HEADS = 8
RET_HEAD_DIM = 256
N_RET_HEADS = 4
RET_CHUNK = 128
ATTN_BLK = 128
DILATIONS = (1, 4, 16)
ATTN_LOOKAHEAD = 8
RET_LOOKAHEAD = 2
NORM_EPS = 1e-6
MASK_VALUE = -1e30
LOG2E = math.log2(math.e)
QK_SCALE = LOG2E / math.sqrt(ATTN_HEAD_DIM)
NORM_ROWS = 128
FFN_CAST_BLOCKS = 64
VMEM_LIMIT = 60 * 1024 * 1024


def _rms_rows(x, w):
    ms = jnp.mean(x * x, axis=-1, keepdims=True)
    return x * lax.rsqrt(ms + NORM_EPS) * w


def _inproj_kernel(*refs, tm, tn, attn_tiles, q_tiles, casts):
    nc = len(casts)
    x_ref, nw_ref, w_ref = refs[:3]
    cast_in = refs[3:3 + nc]
    o1_ref, o4_ref, o16_ref, ret_ref = refs[3 + nc:7 + nc]
    cast_out = refs[7 + nc:7 + 2 * nc]
    h_even, h_odd, nat, mod4 = refs[7 + 2 * nc:]
    i = pl.program_id(0)
    j = pl.program_id(1)
    xr = x_ref.shape[0]
    n4, n16 = tm // 4, tm // 16

    step_no = (i - 1) * pl.num_programs(1) + j
    for src, dst, (first, count) in zip(cast_in, cast_out, casts):
        @pl.when((step_no >= first) & (step_no < first + count))
        def _(src=src, dst=dst):
            dst[...] = src[...].astype(dst.dtype)

    def norm_chunk(h_ref):
        r = pl.multiple_of(jnp.minimum(j, tm // xr - 1) * xr, xr)
        h_ref[pl.ds(r, xr), :] = _rms_rows(x_ref[...], nw_ref[...]).astype(BF16)

    def step(h_wr, h_rd, store):
        norm_chunk(h_wr)
        acc = jnp.dot(h_rd[...], w_ref[...], preferred_element_type=F32)
        for c in range(tn // LANES):
            store(c, acc[:, c * LANES:(c + 1) * LANES])

    @pl.when(i == 0)
    def _():
        norm_chunk(h_even)

    def store_attn(c, slab):
        slab = slab * jnp.where(j < q_tiles, QK_SCALE, 1.0)
        o1_ref[c] = slab.astype(o1_ref.dtype)
        b = c % nat.shape[0]
        nat[b] = slab
        for r in range(4):
            rows = nat[b, pl.ds(r, n4, stride=4), :]
            mod4[b, r * n4:(r + 1) * n4, :] = rows
            o4_ref[c, :, r * LANES:(r + 1) * LANES] = rows.astype(o4_ref.dtype)
        for r in range(16):
            a, r4 = divmod(r, 4)
            rows = mod4[b, pl.ds(r4 * n4 + a, n16, stride=4), :]
            o16_ref[c, :, r * LANES:(r + 1) * LANES] = rows.astype(o16_ref.dtype)

    def store_ret(c, slab):
        ret_ref[c] = slab.astype(ret_ref.dtype)

    for parity, (h_wr, h_rd) in enumerate(((h_even, h_odd), (h_odd, h_even))):
        mine = (i > 0) & (i % 2 == parity)
        pl.when(mine & (j < attn_tiles))(functools.partial(step, h_wr, h_rd, store_attn))
        pl.when(mine & (j >= attn_tiles))(functools.partial(step, h_wr, h_rd, store_ret))


def _in_proj(x2d, norm_w, w_bf16, side_f32, *, batch, seq, tm=1024, tn=1024):
    m, d = x2d.shape
    n = w_bf16.shape[1]
    n_attn = 3 * N_ATTN_HEADS * ATTN_HEAD_DIM
    ns = tn // LANES
    tiles = m // tm
    nj = n // tn
    attn_tiles = n_attn // tn
    assert 0 < attn_tiles < nj
    chunks = tm // NORM_ROWS
    while chunks > nj:
        chunks //= 2
    xr = tm // chunks
    per_b = seq // tm
    q_tiles = N_ATTN_HEADS * ATTN_HEAD_DIM // tn

    def t_of(i):
        return jnp.maximum(i - 1, 0)

    def ja(i, j):
        return jnp.where(i == 0, 0, jnp.minimum(j, attn_tiles - 1))

    def jr(i, j):
        return jnp.where(i == 0, 0, jnp.maximum(j - attn_tiles, 0))

    casts, cast_specs, cast_shapes = [], [], []
    for arr, first, blocks in side_f32:
        rows, cols = arr.shape
        assert rows % (blocks * 16) == 0
        assert first + blocks <= tiles * nj

        def at(i, j, first=first, blocks=blocks):
            return (jnp.clip((i - 1) * nj + j - first, 0, blocks - 1), 0)

        casts.append((first, blocks))
        cast_specs.append(pl.BlockSpec((rows // blocks, cols), at))
        cast_shapes.append(jax.ShapeDtypeStruct(arr.shape, BF16))

    return pl.pallas_call(
        functools.partial(_inproj_kernel, tm=tm, tn=tn, attn_tiles=attn_tiles, q_tiles=q_tiles,
                          casts=tuple(casts)),
        grid=(tiles + 1, nj),
        in_specs=[
            pl.BlockSpec((xr, d), lambda i, j: (
                jnp.minimum(i, tiles - 1) * chunks + jnp.minimum(j, chunks - 1), 0)),
            pl.BlockSpec((1, d), lambda i, j: (0, 0)),
            pl.BlockSpec((d, tn), lambda i, j: (0, jnp.where(i == 0, 0, j))),
        ] + cast_specs,
        out_specs=[
            pl.BlockSpec((ns, tm, LANES), lambda i, j: (ja(i, j), t_of(i), 0)),
            pl.BlockSpec((ns, None, tm // 4, 4 * LANES),
                         lambda i, j: (ja(i, j), t_of(i) // per_b, t_of(i) % per_b, 0)),
            pl.BlockSpec((ns, None, tm // 16, 16 * LANES),
                         lambda i, j: (ja(i, j), t_of(i) // per_b, t_of(i) % per_b, 0)),
            pl.BlockSpec((ns, tm, LANES), lambda i, j: (jr(i, j), t_of(i), 0)),
        ] + cast_specs,
        out_shape=[
            jax.ShapeDtypeStruct((n_attn // LANES, m, LANES), BF16),
            jax.ShapeDtypeStruct((n_attn // LANES, batch, seq // 4, 4 * LANES), BF16),
            jax.ShapeDtypeStruct((n_attn // LANES, batch, seq // 16, 16 * LANES), BF16),
            jax.ShapeDtypeStruct(((n - n_attn) // LANES, m, LANES), BF16),
        ] + cast_shapes,
        scratch_shapes=[pltpu.VMEM((tm, d), BF16),
                        pltpu.VMEM((tm, d), BF16),
                        pltpu.VMEM((2, tm, LANES), F32),
                        pltpu.VMEM((2, tm, LANES), F32)],
        compiler_params=pltpu.CompilerParams(
            dimension_semantics=("arbitrary", "arbitrary"),
            vmem_limit_bytes=VMEM_LIMIT),
        name="in_proj",
    )(x2d, norm_w.reshape(1, d), w_bf16, *[arr for arr, _, _ in side_f32])


def _attn_kernel(slopes_ref, qkv1, qkv4, qkv16, o_ref, bm, macc, lacc, oacc, onat, *, seq):
    q1, k1, v1 = (qkv1.at[t] for t in range(3))
    q4, k4, v4 = (qkv4.at[t] for t in range(3))
    q16, k16, v16 = (qkv16.at[t] for t in range(3))
    blk = ATTN_BLK
    nblk = seq // blk
    per4 = nblk // 4
    slope = slopes_ref[pl.program_id(1)]

    qi = lax.broadcasted_iota(jnp.int32, (blk, 2 * blk), 0)
    kj = lax.broadcasted_iota(jnp.int32, (blk, 2 * blk), 1)
    diff = qi - kj + blk
    valid = (diff >= 0) & (diff <= blk)
    dist = diff.astype(F32)
    for pi, dil in enumerate(DILATIONS):
        bm[pi] = jnp.where(valid, (-LOG2E * slope) * (dist * float(dil)), MASK_VALUE)

    def scores(pi, q_at, k_at):
        kk = k_at()
        bias = bm[pi] if kk.shape[0] == 2 * blk else bm[pi, :, blk:]
        return lax.dot_general(q_at(), kk, (((1,), (1,)), ((), ())),
                               preferred_element_type=F32) + bias

    def finish(s, pi, v_at, dst):
        mx = jnp.max(s, axis=-1, keepdims=True)
        p = jnp.exp2(s - mx).astype(BF16)
        vv = v_at()
        acc = jnp.dot(p, jnp.concatenate([vv, jnp.ones_like(vv)], axis=-1),
                      preferred_element_type=F32)
        oacc[pi, dst, :] = acc[:, :LANES]
        lacc[pi, dst, :] = acc[:, LANES:]
        macc[pi, dst, :] = jnp.broadcast_to(mx, (blk, LANES))

    def p0_block(i):
        rows = slice(i * blk, (i + 1) * blk)
        kr = slice(max(i - 1, 0) * blk, (i + 1) * blk)
        return (0, lambda: q1[rows, :], lambda: k1[kr, :], lambda: v1[kr, :], rows)

    def p1_block(r4, n):
        ln = slice(r4 * LANES, (r4 + 1) * LANES)
        qr = slice(n * blk, (n + 1) * blk)
        kr = slice(max(n - 1, 0) * blk, (n + 1) * blk)
        dst = slice((r4 * per4 + n) * blk, (r4 * per4 + n + 1) * blk)
        return (1, lambda: q4[qr, ln], lambda: k4[kr, ln], lambda: v4[kr, ln], dst)

    def p2_block(r):
        ln = slice(r * LANES, (r + 1) * LANES)
        dst = pl.ds((r % 4) * (seq // 4) + r // 4, blk, stride=4)
        return (2, lambda: q16[:, ln], lambda: k16[:, ln], lambda: v16[:, ln], dst)

    def combine(n):
        for r4 in range(4):
            rows = slice((r4 * per4 + n) * blk, (r4 * per4 + n + 1) * blk)
            nat = pl.ds(r4 + 4 * blk * n, blk, stride=4)
            at = (nat, rows, rows)
            ms = [macc[pi, at[pi], :] for pi in range(3)]
            mx = jnp.maximum(jnp.maximum(ms[0], ms[1]), ms[2])
            es = [jnp.exp2(m - mx) for m in ms]
            den = es[0] * lacc[0, nat, :] + es[1] * lacc[1, rows, :] + es[2] * lacc[2, rows, :]
            num = es[0] * oacc[0, nat, :] + es[1] * oacc[1, rows, :] + es[2] * oacc[2, rows, :]
            onat[nat, :] = num * (1.0 / den)
        done = slice(4 * blk * n, 4 * blk * (n + 1))
        o_ref[done, :] = onat[done, :].astype(o_ref.dtype)

    work = [p2_block(r) for r in range(nblk)]
    after = {}
    for n in range(per4):
        work += [p0_block(4 * n + a) for a in range(4)] + [p1_block(r4, n) for r4 in range(4)]
        after[len(work) - 1] = n

    pending = {}
    for t in range(len(work) + ATTN_LOOKAHEAD):
        if t < len(work):
            pi, q_at, k_at, _, _ = work[t]
            pending[t] = scores(pi, q_at, k_at)
        b = t - ATTN_LOOKAHEAD
        if b >= 0:
            pi, _, _, v_at, dst = work[b]
            finish(pending.pop(b), pi, v_at, dst)
            if b in after:
                combine(after[b])


def _attention(nat, mod4, mod16, slopes, *, batch, seq):
    h = N_ATTN_HEADS
    views = [a.reshape((3, h) + a.shape[1:]) for a in (nat, mod4, mod16)]
    specs = [pl.BlockSpec((3, None, seq, LANES), lambda b, hh: (0, hh, b, 0))]
    for d in DILATIONS[1:]:
        specs.append(pl.BlockSpec((3, None, None, seq // d, d * LANES),
                                  lambda b, hh: (0, hh, b, 0, 0)))
    return pl.pallas_call(
        functools.partial(_attn_kernel, seq=seq),
        grid=(batch, h),
        in_specs=[pl.BlockSpec(memory_space=pltpu.SMEM)] + specs,
        out_specs=pl.BlockSpec((None, seq, LANES), lambda b, hh: (hh, b, 0)),
        out_shape=jax.ShapeDtypeStruct((h, batch * seq, LANES), BF16),
        scratch_shapes=[
            pltpu.VMEM((3, ATTN_BLK, 2 * ATTN_BLK), F32),
            pltpu.VMEM((3, seq, LANES), F32),
            pltpu.VMEM((3, seq, LANES), F32),
            pltpu.VMEM((3, seq, LANES), F32),
            pltpu.VMEM((seq, LANES), F32),
        ],
        compiler_params=pltpu.CompilerParams(
            dimension_semantics=("parallel", "parallel"),
            vmem_limit_bytes=VMEM_LIMIT),
        name="dilated_attention",
    )(slopes, *views)


def _ret_kernel(lg_ref, qkvg_ref, o_ref, decay, zeta, xi, states, *, seq):
    q_ref, k_ref, v_ref, g_ref = (qkvg_ref.at[t] for t in range(4))
    c = RET_CHUNK
    dh = RET_HEAD_DIM
    nc = seq // c
    lg = lg_ref[pl.program_id(1)]
    k_scale = 1.0 / math.sqrt(dh)
    assert math.log2(k_scale).is_integer()

    ii = lax.broadcasted_iota(jnp.int32, (c, c), 0)
    jj = lax.broadcasted_iota(jnp.int32, (c, c), 1)
    dif = (ii - jj).astype(F32)
    decay[...] = jnp.where(dif >= 0, jnp.exp(lg * jnp.maximum(dif, 0.0)), 0.0) * k_scale
    idx = lax.broadcasted_iota(jnp.int32, (c, dh), 0).astype(F32)
    zeta[...] = jnp.exp(lg * (c - 1.0 - idx)) * k_scale
    xi[...] = jnp.exp(lg * (idx + 1.0))
    gamma_chunk = jnp.exp(jnp.full((dh, dh), lg * c, F32))

    def wide(ref, n):
        rows = slice(n * c, (n + 1) * c)
        return jnp.concatenate([ref[0, rows, :], ref[1, rows, :]], axis=-1)

    st = jnp.zeros((dh, dh), F32)
    for n in range(nc):
        states[n] = st.astype(BF16)
        if n + 1 < nc:
            kz = (wide(k_ref, n).astype(F32) * zeta[...]).astype(BF16)
            kv = lax.dot_general(kz, wide(v_ref, n), (((0,), (0,)), ((), ())),
                                 preferred_element_type=F32)
            st = st * gamma_chunk + kv

    def front(n):
        qn = wide(q_ref, n)
        sc = lax.dot_general(qn, wide(k_ref, n), (((1,), (1,)), ((), ())),
                             preferred_element_type=F32) * decay[...]
        cross = jnp.dot(qn, states[n], preferred_element_type=F32) * xi[...]
        return sc.astype(BF16), cross

    def back(n, sc, cross):
        ret = jnp.dot(sc, wide(v_ref, n), preferred_element_type=F32) + cross
        ret = ret * lax.rsqrt(jnp.mean(ret * ret, axis=-1, keepdims=True) + NORM_EPS)
        gate = wide(g_ref, n).astype(F32)
        out = (gate * jax.nn.sigmoid(gate) * ret).astype(o_ref.dtype)
        o_ref[0, n * c:(n + 1) * c, :] = out[:, :LANES]
        o_ref[1, n * c:(n + 1) * c, :] = out[:, LANES:]

    pending = {}
    for t in range(nc + RET_LOOKAHEAD):
        if t < nc:
            pending[t] = front(t)
        if t >= RET_LOOKAHEAD:
            back(t - RET_LOOKAHEAD, *pending.pop(t - RET_LOOKAHEAD))


def _retention(proj_hm, log_gamma, *, batch, seq):
    h = N_RET_HEADS
    blk = (2, seq, LANES)
    view = proj_hm.reshape((4, h, 2) + proj_hm.shape[1:])
    return pl.pallas_call(
        functools.partial(_ret_kernel, seq=seq),
        grid=(batch, h),
        in_specs=[
            pl.BlockSpec(memory_space=pltpu.SMEM),
            pl.BlockSpec((4, None, 2, seq, LANES), lambda b, hh: (0, hh, 0, b, 0)),
        ],
        out_specs=pl.BlockSpec(blk, lambda b, hh: (hh, b, 0)),
        out_shape=jax.ShapeDtypeStruct((2 * h, batch * seq, LANES), BF16),
        scratch_shapes=[
            pltpu.VMEM((RET_CHUNK, RET_CHUNK), F32),
            pltpu.VMEM((RET_CHUNK, RET_HEAD_DIM), F32),
            pltpu.VMEM((RET_CHUNK, RET_HEAD_DIM), F32),
            pltpu.VMEM((seq // RET_CHUNK, RET_HEAD_DIM, RET_HEAD_DIM), BF16),
        ],
        compiler_params=pltpu.CompilerParams(
            dimension_semantics=("parallel", "parallel"),
            vmem_limit_bytes=VMEM_LIMIT),
        name="retention",
    )(log_gamma, view)


def _outproj_kernel(x_ref, a_ref, r_ref, wa_ref, wr_ref, o_ref):
    na = a_ref.shape[0]
    nr = r_ref.shape[0]
    a = jnp.concatenate([a_ref[i] for i in range(na)], axis=-1)
    r = jnp.concatenate([r_ref[i] for i in range(nr)], axis=-1)
    y = jnp.dot(a, wa_ref[...], preferred_element_type=F32)
    y = y + jnp.dot(r, wr_ref[...], preferred_element_type=F32)
    o_ref[...] = x_ref[...] + y


def _out_proj(x2d, attn_hm, ret_hm, w_out_bf16, *, tm=512):
    m, d = x2d.shape
    na, nr = attn_hm.shape[0], ret_hm.shape[0]
    assert na == nr
    return pl.pallas_call(
        _outproj_kernel,
        grid=(m // tm,),
        in_specs=[
            pl.BlockSpec((tm, d), lambda i: (i, 0)),
            pl.BlockSpec((na, tm, LANES), lambda i: (0, i, 0)),
            pl.BlockSpec((nr, tm, LANES), lambda i: (0, i, 0)),
            pl.BlockSpec((na * LANES, d), lambda i: (0, 0)),
            pl.BlockSpec((nr * LANES, d), lambda i: (1, 0)),
        ],
        out_specs=pl.BlockSpec((tm, d), lambda i: (i, 0)),
        out_shape=jax.ShapeDtypeStruct((m, d), F32),
        compiler_params=pltpu.CompilerParams(
            dimension_semantics=("parallel",),
            vmem_limit_bytes=VMEM_LIMIT),
        name="out_proj",
    )(x2d, attn_hm, ret_hm, w_out_bf16, w_out_bf16)


def _ffn_kernel(x_ref, nw_ref, wg_ref, wu_ref, wd_ref, fw_ref, o_ref, h_ref, *, tm, final_norm):
    f = pl.program_id(1)

    @pl.when(f == 0)
    def _():
        def body(i, c):
            r = pl.multiple_of(i * NORM_ROWS, NORM_ROWS)
            h_ref[pl.ds(r, NORM_ROWS), :] = _rms_rows(
                x_ref[pl.ds(r, NORM_ROWS), :], nw_ref[...]).astype(h_ref.dtype)
            o_ref[pl.ds(r, NORM_ROWS), :] = jnp.zeros((NORM_ROWS, o_ref.shape[1]), F32)
            return c
        lax.fori_loop(0, tm // NORM_ROWS, body, 0)

    h = h_ref[...]
    g = jnp.dot(h, wg_ref[...], preferred_element_type=F32)
    u = jnp.dot(h, wu_ref[...], preferred_element_type=F32)
    a = (g * jax.nn.sigmoid(g) * u).astype(BF16)
    y = jnp.dot(a, wd_ref[...], preferred_element_type=F32)
    o_ref[...] += y

    @pl.when(f == pl.num_programs(1) - 1)
    def _():
        def body(i, c):
            r = pl.multiple_of(i * NORM_ROWS, NORM_ROWS)
            x2 = x_ref[pl.ds(r, NORM_ROWS), :] + o_ref[pl.ds(r, NORM_ROWS), :]
            if final_norm:
                x2 = _rms_rows(x2, fw_ref[...])
            o_ref[pl.ds(r, NORM_ROWS), :] = x2
            return c
        lax.fori_loop(0, tm // NORM_ROWS, body, 0)


def _ffn(x2d, norm_w, wg, wu, wd, final_w, *, final_norm, tm=1024, tf=512):
    m, d = x2d.shape
    hid = wg.shape[1]
    return pl.pallas_call(
        functools.partial(_ffn_kernel, tm=tm, final_norm=final_norm),
        grid=(m // tm, hid // tf),
        in_specs=[
            pl.BlockSpec((tm, d), lambda i, f: (i, 0)),
            pl.BlockSpec((1, d), lambda i, f: (0, 0)),
            pl.BlockSpec((d, tf), lambda i, f: (0, f)),
            pl.BlockSpec((d, tf), lambda i, f: (0, f)),
            pl.BlockSpec((tf, d), lambda i, f: (f, 0)),
            pl.BlockSpec((1, d), lambda i, f: (0, 0)),
        ],
        out_specs=pl.BlockSpec((tm, d), lambda i, f: (i, 0)),
        out_shape=jax.ShapeDtypeStruct((m, d), F32),
        scratch_shapes=[pltpu.VMEM((tm, d), BF16)],
        compiler_params=pltpu.CompilerParams(
            dimension_semantics=("parallel", "arbitrary"),
            vmem_limit_bytes=VMEM_LIMIT),
        name="ffn",
    )(x2d, norm_w.reshape(1, d), wg, wu, wd, final_w.reshape(1, d))


def kernel(x, norm_mix_w, w_in, w_out, norm_ffn_w, w_gate, w_up, w_down, norm_final_w):
    batch, seq, d = x.shape
    depth = w_in.shape[0]
    assert seq == DILATIONS[-1] * ATTN_BLK and seq % RET_CHUNK == 0
    slopes = jnp.exp2(-8.0 * jnp.arange(1, N_ATTN_HEADS + 1, dtype=F32) / N_ATTN_HEADS)
    log_gamma = jnp.log(1.0 - jnp.exp2(-5.0 - jnp.arange(N_RET_HEADS, dtype=F32)))

    xs = x.reshape(batch * seq, d)
    for layer in range(depth):
        w_in_l = w_in[layer].astype(BF16)
        side = ((w_gate[layer], 0, FFN_CAST_BLOCKS), (w_up[layer], 0, FFN_CAST_BLOCKS),
                (w_down[layer], FFN_CAST_BLOCKS, FFN_CAST_BLOCKS // 2),
                (w_out[layer], FFN_CAST_BLOCKS, FFN_CAST_BLOCKS // 2))
        nat, mod4, mod16, proj_ret, wg, wu, wd, wo = _in_proj(
            xs, norm_mix_w[layer], w_in_l, side, batch=batch, seq=seq)
        attn = _attention(nat, mod4, mod16, slopes, batch=batch, seq=seq)
        ret = _retention(proj_ret, log_gamma, batch=batch, seq=seq)
        xs = _out_proj(xs, attn, ret, wo)
        xs = _ffn(xs, norm_ffn_w[layer], wg, wu, wd, norm_final_w,
                  final_norm=(layer == depth - 1))
    return xs.reshape(batch, seq, d)
```

```python
import functools
import math

import jax
import jax.numpy as jnp
from jax import lax
from jax.experimental import pallas as pl
from jax.experimental.pallas import tpu as pltpu

F32 = jnp.float32
BF16 = jnp.bfloat16

LANES = 128
ATTN_HEAD_DIM = 128
N_ATTN_HEADS = 8
RET_HEAD_DIM = 256
N_RET_HEADS = 4
RET_CHUNK = 128
ATTN_BLK = 128
DILATIONS = (1, 4, 16)
ATTN_LOOKAHEAD = 8
RET_LOOKAHEAD = 2
NORM_EPS = 1e-6
MASK_VALUE = -1e30
LOG2E = math.log2(math.e)
QK_SCALE = LOG2E / math.sqrt(ATTN_HEAD_DIM)
NORM_ROWS = 128
FFN_CAST_BLOCKS = 64
VMEM_LIMIT = 60 * 1024 * 1024


def _rms_rows(x, w):
    ms = jnp.mean(x * x, axis=-1, keepdims=True)
    return x * lax.rsqrt(ms + NORM_EPS) * w


def _inproj_kernel(*refs, tm, tn, attn_tiles, q_tiles, casts):
    nc = len(casts)
    x_ref, nw_ref, w_ref = refs[:3]
    cast_in = refs[3:3 + nc]
    o1_ref, o4_ref, o16_ref, ret_ref = refs[3 + nc:7 + nc]
    cast_out = refs[7 + nc:7 + 2 * nc]
    h_even, h_odd, nat, mod4 = refs[7 + 2 * nc:]
    i = pl.program_id(0)
    j = pl.program_id(1)
    xr = x_ref.shape[0]
    n4, n16 = tm // 4, tm // 16

    step_no = (i - 1) * pl.num_programs(1) + j
    for src, dst, (first, count) in zip(cast_in, cast_out, casts):
        @pl.when((step_no >= first) & (step_no < first + count))
        def _(src=src, dst=dst):
            dst[...] = src[...].astype(dst.dtype)

    def norm_chunk(h_ref):
        r = pl.multiple_of(jnp.minimum(j, tm // xr - 1) * xr, xr)
        h_ref[pl.ds(r, xr), :] = _rms_rows(x_ref[...], nw_ref[...]).astype(BF16)

    def step(h_wr, h_rd, store):
        norm_chunk(h_wr)
        acc = jnp.dot(h_rd[...], w_ref[...], preferred_element_type=F32)
        for c in range(tn // LANES):
            store(c, acc[:, c * LANES:(c + 1) * LANES])

    @pl.when(i == 0)
    def _():
        norm_chunk(h_even)

    def store_attn(c, slab):
        slab = slab * jnp.where(j < q_tiles, QK_SCALE, 1.0)
        o1_ref[c] = slab.astype(o1_ref.dtype)
        b = c % nat.shape[0]
        nat[b] = slab
        for r in range(4):
            rows = nat[b, pl.ds(r, n4, stride=4), :]
            mod4[b, r * n4:(r + 1) * n4, :] = rows
            o4_ref[c, :, r * LANES:(r + 1) * LANES] = rows.astype(o4_ref.dtype)
        for r in range(16):
            a, r4 = divmod(r, 4)
            rows = mod4[b, pl.ds(r4 * n4 + a, n16, stride=4), :]
            o16_ref[c, :, r * LANES:(r + 1) * LANES] = rows.astype(o16_ref.dtype)

    def store_ret(c, slab):
        ret_ref[c] = slab.astype(ret_ref.dtype)

    for parity, (h_wr, h_rd) in enumerate(((h_even, h_odd), (h_odd, h_even))):
        mine = (i > 0) & (i % 2 == parity)
        pl.when(mine & (j < attn_tiles))(functools.partial(step, h_wr, h_rd, store_attn))
        pl.when(mine & (j >= attn_tiles))(functools.partial(step, h_wr, h_rd, store_ret))


def _in_proj(x2d, norm_w, w_bf16, side_f32, *, batch, seq, tm=1024, tn=1024):
    m, d = x2d.shape
    n = w_bf16.shape[1]
    n_attn = 3 * N_ATTN_HEADS * ATTN_HEAD_DIM
    ns = tn // LANES
    tiles = m // tm
    nj = n // tn
    attn_tiles = n_attn // tn
    assert 0 < attn_tiles < nj
    chunks = tm // NORM_ROWS
    while chunks > nj:
        chunks //= 2
    xr = tm // chunks
    per_b = seq // tm
    q_tiles = N_ATTN_HEADS * ATTN_HEAD_DIM // tn

    def t_of(i):
        return jnp.maximum(i - 1, 0)

    def ja(i, j):
        return jnp.where(i == 0, 0, jnp.minimum(j, attn_tiles - 1))

    def jr(i, j):
        return jnp.where(i == 0, 0, jnp.maximum(j - attn_tiles, 0))

    casts, cast_specs, cast_shapes = [], [], []
    for arr, first, blocks in side_f32:
        rows, cols = arr.shape
        assert rows % (blocks * 16) == 0
        assert first + blocks <= tiles * nj

        def at(i, j, first=first, blocks=blocks):
            return (jnp.clip((i - 1) * nj + j - first, 0, blocks - 1), 0)

        casts.append((first, blocks))
        cast_specs.append(pl.BlockSpec((rows // blocks, cols), at))
        cast_shapes.append(jax.ShapeDtypeStruct(arr.shape, BF16))

    return pl.pallas_call(
        functools.partial(_inproj_kernel, tm=tm, tn=tn, attn_tiles=attn_tiles, q_tiles=q_tiles,
                          casts=tuple(casts)),
        grid=(tiles + 1, nj),
        in_specs=[
            pl.BlockSpec((xr, d), lambda i, j: (
                jnp.minimum(i, tiles - 1) * chunks + jnp.minimum(j, chunks - 1), 0)),
            pl.BlockSpec((1, d), lambda i, j: (0, 0)),
            pl.BlockSpec((d, tn), lambda i, j: (0, jnp.where(i == 0, 0, j))),
        ] + cast_specs,
        out_specs=[
            pl.BlockSpec((ns, tm, LANES), lambda i, j: (ja(i, j), t_of(i), 0)),
            pl.BlockSpec((ns, None, tm // 4, 4 * LANES),
                         lambda i, j: (ja(i, j), t_of(i) // per_b, t_of(i) % per_b, 0)),
            pl.BlockSpec((ns, None, tm // 16, 16 * LANES),
                         lambda i, j: (ja(i, j), t_of(i) // per_b, t_of(i) % per_b, 0)),
            pl.BlockSpec((ns, tm, LANES), lambda i, j: (jr(i, j), t_of(i), 0)),
        ] + cast_specs,
        out_shape=[
            jax.ShapeDtypeStruct((n_attn // LANES, m, LANES), BF16),
            jax.ShapeDtypeStruct((n_attn // LANES, batch, seq // 4, 4 * LANES), BF16),
            jax.ShapeDtypeStruct((n_attn // LANES, batch, seq // 16, 16 * LANES), BF16),
            jax.ShapeDtypeStruct(((n - n_attn) // LANES, m, LANES), BF16),
        ] + cast_shapes,
        scratch_shapes=[pltpu.VMEM((tm, d), BF16),
                        pltpu.VMEM((tm, d), BF16),
                        pltpu.VMEM((2, tm, LANES), F32),
                        pltpu.VMEM((2, tm, LANES), F32)],
        compiler_params=pltpu.CompilerParams(
            dimension_semantics=("arbitrary", "arbitrary"),
            vmem_limit_bytes=VMEM_LIMIT),
        name="in_proj",
    )(x2d, norm_w.reshape(1, d), w_bf16, *[arr for arr, _, _ in side_f32])


def _attn_kernel(slopes_ref, qkv1, qkv4, qkv16, o_ref, bm, macc, lacc, oacc, onat, *, seq):
    q1, k1, v1 = (qkv1.at[t] for t in range(3))
    q4, k4, v4 = (qkv4.at[t] for t in range(3))
    q16, k16, v16 = (qkv16.at[t] for t in range(3))
    blk = ATTN_BLK
    nblk = seq // blk
    per4 = nblk // 4
    slope = slopes_ref[pl.program_id(1)]

    qi = lax.broadcasted_iota(jnp.int32, (blk, 2 * blk), 0)
    kj = lax.broadcasted_iota(jnp.int32, (blk, 2 * blk), 1)
    diff = qi - kj + blk
    valid = (diff >= 0) & (diff <= blk)
    dist = diff.astype(F32)
    for pi, dil in enumerate(DILATIONS):
        bm[pi] = jnp.where(valid, (-LOG2E * slope) * (dist * float(dil)), MASK_VALUE)

    def scores(pi, q_at, k_at):
        kk = k_at()
        bias = bm[pi] if kk.shape[0] == 2 * blk else bm[pi, :, blk:]
        return lax.dot_general(q_at(), kk, (((1,), (1,)), ((), ())),
                               preferred_element_type=F32) + bias

    def finish(s, pi, v_at, dst):
        mx = jnp.max(s, axis=-1, keepdims=True)
        p = jnp.exp2(s - mx).astype(BF16)
        vv = v_at()
        acc = jnp.dot(p, jnp.concatenate([vv, jnp.ones_like(vv)], axis=-1),
                      preferred_element_type=F32)
        oacc[pi, dst, :] = acc[:, :LANES]
        lacc[pi, dst, :] = acc[:, LANES:]
        macc[pi, dst, :] = jnp.broadcast_to(mx, (blk, LANES))

    def p0_block(i):
        rows = slice(i * blk, (i + 1) * blk)
        kr = slice(max(i - 1, 0) * blk, (i + 1) * blk)
        return (0, lambda: q1[rows, :], lambda: k1[kr, :], lambda: v1[kr, :], rows)

    def p1_block(r4, n):
        ln = slice(r4 * LANES, (r4 + 1) * LANES)
        qr = slice(n * blk, (n + 1) * blk)
        kr = slice(max(n - 1, 0) * blk, (n + 1) * blk)
        dst = slice((r4 * per4 + n) * blk, (r4 * per4 + n + 1) * blk)
        return (1, lambda: q4[qr, ln], lambda: k4[kr, ln], lambda: v4[kr, ln], dst)

    def p2_block(r):
        ln = slice(r * LANES, (r + 1) * LANES)
        dst = pl.ds((r % 4) * (seq // 4) + r // 4, blk, stride=4)
        return (2, lambda: q16[:, ln], lambda: k16[:, ln], lambda: v16[:, ln], dst)

    def combine(n):
        for r4 in range(4):
            rows = slice((r4 * per4 + n) * blk, (r4 * per4 + n + 1) * blk)
            nat = pl.ds(r4 + 4 * blk * n, blk, stride=4)
            at = (nat, rows, rows)
            ms = [macc[pi, at[pi], :] for pi in range(3)]
            mx = jnp.maximum(jnp.maximum(ms[0], ms[1]), ms[2])
            es = [jnp.exp2(m - mx) for m in ms]
            den = es[0] * lacc[0, nat, :] + es[1] * lacc[1, rows, :] + es[2] * lacc[2, rows, :]
            num = es[0] * oacc[0, nat, :] + es[1] * oacc[1, rows, :] + es[2] * oacc[2, rows, :]
            onat[nat, :] = num * (1.0 / den)
        done = slice(4 * blk * n, 4 * blk * (n + 1))
        o_ref[done, :] = onat[done, :].astype(o_ref.dtype)

    work = [p2_block(r) for r in range(nblk)]
    after = {}
    for n in range(per4):
        work += [p0_block(4 * n + a) for a in range(4)] + [p1_block(r4, n) for r4 in range(4)]
        after[len(work) - 1] = n

    pending = {}
    for t in range(len(work) + ATTN_LOOKAHEAD):
        if t < len(work):
            pi, q_at, k_at, _, _ = work[t]
            pending[t] = scores(pi, q_at, k_at)
        b = t - ATTN_LOOKAHEAD
        if b >= 0:
            pi, _, _, v_at, dst = work[b]
            finish(pending.pop(b), pi, v_at, dst)
            if b in after:
                combine(after[b])


def _attention(nat, mod4, mod16, slopes, *, batch, seq):
    h = N_ATTN_HEADS
    views = [a.reshape((3, h) + a.shape[1:]) for a in (nat, mod4, mod16)]
    specs = [pl.BlockSpec((3, None, seq, LANES), lambda b, hh: (0, hh, b, 0))]
    for d in DILATIONS[1:]:
        specs.append(pl.BlockSpec((3, None, None, seq // d, d * LANES),
                                  lambda b, hh: (0, hh, b, 0, 0)))
    return pl.pallas_call(
        functools.partial(_attn_kernel, seq=seq),
        grid=(batch, h),
        in_specs=[pl.BlockSpec(memory_space=pltpu.SMEM)] + specs,
        out_specs=pl.BlockSpec((None, seq, LANES), lambda b, hh: (hh, b, 0)),
        out_shape=jax.ShapeDtypeStruct((h, batch * seq, LANES), BF16),
        scratch_shapes=[
            pltpu.VMEM((3, ATTN_BLK, 2 * ATTN_BLK), F32),
            pltpu.VMEM((3, seq, LANES), F32),
            pltpu.VMEM((3, seq, LANES), F32),
            pltpu.VMEM((3, seq, LANES), F32),
            pltpu.VMEM((seq, LANES), F32),
        ],
        compiler_params=pltpu.CompilerParams(
            dimension_semantics=("parallel", "parallel"),
            vmem_limit_bytes=VMEM_LIMIT),
        name="dilated_attention",
    )(slopes, *views)


def _ret_kernel(lg_ref, qkvg_ref, o_ref, decay, zeta, xi, states, *, seq):
    q_ref, k_ref, v_ref, g_ref = (qkvg_ref.at[t] for t in range(4))
    c = RET_CHUNK
    dh = RET_HEAD_DIM
    nc = seq // c
    lg = lg_ref[pl.program_id(1)]
    k_scale = 1.0 / math.sqrt(dh)
    assert math.log2(k_scale).is_integer()

    ii = lax.broadcasted_iota(jnp.int32, (c, c), 0)
    jj = lax.broadcasted_iota(jnp.int32, (c, c), 1)
    dif = (ii - jj).astype(F32)
    decay[...] = jnp.where(dif >= 0, jnp.exp(lg * jnp.maximum(dif, 0.0)), 0.0) * k_scale
    idx = lax.broadcasted_iota(jnp.int32, (c, dh), 0).astype(F32)
    zeta[...] = jnp.exp(lg * (c - 1.0 - idx)) * k_scale
    xi[...] = jnp.exp(lg * (idx + 1.0))
    gamma_chunk = jnp.exp(jnp.full((dh, dh), lg * c, F32))

    def wide(ref, n):
        rows = slice(n * c, (n + 1) * c)
        return jnp.concatenate([ref[0, rows, :], ref[1, rows, :]], axis=-1)

    st = jnp.zeros((dh, dh), F32)
    for n in range(nc):
        states[n] = st.astype(BF16)
        if n + 1 < nc:
            kz = (wide(k_ref, n).astype(F32) * zeta[...]).astype(BF16)
            kv = lax.dot_general(kz, wide(v_ref, n), (((0,), (0,)), ((), ())),
                                 preferred_element_type=F32)
            st = st * gamma_chunk + kv

    def front(n):
        qn = wide(q_ref, n)
        sc = lax.dot_general(qn, wide(k_ref, n), (((1,), (1,)), ((), ())),
                             preferred_element_type=F32) * decay[...]
        cross = jnp.dot(qn, states[n], preferred_element_type=F32) * xi[...]
        return sc.astype(BF16), cross

    def back(n, sc, cross):
        ret = jnp.dot(sc, wide(v_ref, n), preferred_element_type=F32) + cross
        ret = ret * lax.rsqrt(jnp.mean(ret * ret, axis=-1, keepdims=True) + NORM_EPS)
        gate = wide(g_ref, n).astype(F32)
        out = (gate * jax.nn.sigmoid(gate) * ret).astype(o_ref.dtype)
        o_ref[0, n * c:(n + 1) * c, :] = out[:, :LANES]
        o_ref[1, n * c:(n + 1) * c, :] = out[:, LANES:]

    pending = {}
    for t in range(nc + RET_LOOKAHEAD):
        if t < nc:
            pending[t] = front(t)
        if t >= RET_LOOKAHEAD:
            back(t - RET_LOOKAHEAD, *pending.pop(t - RET_LOOKAHEAD))


def _retention(proj_hm, log_gamma, *, batch, seq):
    h = N_RET_HEADS
    blk = (2, seq, LANES)
    view = proj_hm.reshape((4, h, 2) + proj_hm.shape[1:])
    return pl.pallas_call(
        functools.partial(_ret_kernel, seq=seq),
        grid=(batch, h),
        in_specs=[
            pl.BlockSpec(memory_space=pltpu.SMEM),
            pl.BlockSpec((4, None, 2, seq, LANES), lambda b, hh: (0, hh, 0, b, 0)),
        ],
        out_specs=pl.BlockSpec(blk, lambda b, hh: (hh, b, 0)),
        out_shape=jax.ShapeDtypeStruct((2 * h, batch * seq, LANES), BF16),
        scratch_shapes=[
            pltpu.VMEM((RET_CHUNK, RET_CHUNK), F32),
            pltpu.VMEM((RET_CHUNK, RET_HEAD_DIM), F32),
            pltpu.VMEM((RET_CHUNK, RET_HEAD_DIM), F32),
            pltpu.VMEM((seq // RET_CHUNK, RET_HEAD_DIM, RET_HEAD_DIM), BF16),
        ],
        compiler_params=pltpu.CompilerParams(
            dimension_semantics=("parallel", "parallel"),
            vmem_limit_bytes=VMEM_LIMIT),
        name="retention",
    )(log_gamma, view)


def _outproj_kernel(x_ref, a_ref, r_ref, w_ref, o_ref):
    mixed = jnp.concatenate([a_ref[i] for i in range(a_ref.shape[0])]
                            + [r_ref[i] for i in range(r_ref.shape[0])], axis=-1)
    o_ref[...] = x_ref[...] + jnp.dot(mixed, w_ref[...], preferred_element_type=F32)


def _out_proj(x2d, attn_hm, ret_hm, w_out_bf16, *, tm=1024):
    m, d = x2d.shape
    na, nr = attn_hm.shape[0], ret_hm.shape[0]
    assert (na + nr) * LANES == w_out_bf16.shape[0]
    return pl.pallas_call(
        _outproj_kernel,
        grid=(m // tm,),
        in_specs=[
            pl.BlockSpec((tm, d), lambda i: (i, 0)),
            pl.BlockSpec((na, tm, LANES), lambda i: (0, i, 0)),
            pl.BlockSpec((nr, tm, LANES), lambda i: (0, i, 0)),
            pl.BlockSpec(w_out_bf16.shape, lambda i: (0, 0), pipeline_mode=pl.Buffered(1)),
        ],
        out_specs=pl.BlockSpec((tm, d), lambda i: (i, 0)),
        out_shape=jax.ShapeDtypeStruct((m, d), F32),
        compiler_params=pltpu.CompilerParams(
            dimension_semantics=("parallel",),
            vmem_limit_bytes=VMEM_LIMIT),
        name="out_proj",
    )(x2d, attn_hm, ret_hm, w_out_bf16)


def _ffn_kernel(x_ref, nw_ref, wg_ref, wu_ref, wd_ref, fw_ref, o_ref, h_ref, *, tm, final_norm):
    f = pl.program_id(1)

    @pl.when(f == 0)
    def _():
        def body(i, c):
            r = pl.multiple_of(i * NORM_ROWS, NORM_ROWS)
            x = x_ref[pl.ds(r, NORM_ROWS), :]
            h_ref[pl.ds(r, NORM_ROWS), :] = _rms_rows(x, nw_ref[...]).astype(h_ref.dtype)
            o_ref[pl.ds(r, NORM_ROWS), :] = x
            return c
        lax.fori_loop(0, tm // NORM_ROWS, body, 0)

    h = h_ref[...]
    g = jnp.dot(h, wg_ref[...], preferred_element_type=F32)
    u = jnp.dot(h, wu_ref[...], preferred_element_type=F32)
    a = (g * jax.nn.sigmoid(g) * u).astype(BF16)
    y = jnp.dot(a, wd_ref[...], preferred_element_type=F32)
    o_ref[...] += y

    if final_norm:
        @pl.when(f == pl.num_programs(1) - 1)
        def _():
            def body(i, c):
                r = pl.multiple_of(i * NORM_ROWS, NORM_ROWS)
                o_ref[pl.ds(r, NORM_ROWS), :] = _rms_rows(o_ref[pl.ds(r, NORM_ROWS), :],
                                                          fw_ref[...])
                return c
            lax.fori_loop(0, tm // NORM_ROWS, body, 0)


def _ffn(x2d, norm_w, wg, wu, wd, final_w, *, final_norm, tm=1024, tf=512):
    m, d = x2d.shape
    hid = wg.shape[1]
    return pl.pallas_call(
        functools.partial(_ffn_kernel, tm=tm, final_norm=final_norm),
        grid=(m // tm, hid // tf),
        in_specs=[
            pl.BlockSpec((tm, d), lambda i, f: (i, 0)),
            pl.BlockSpec((1, d), lambda i, f: (0, 0)),
            pl.BlockSpec((d, tf), lambda i, f: (0, f)),
            pl.BlockSpec((d, tf), lambda i, f: (0, f)),
            pl.BlockSpec((tf, d), lambda i, f: (f, 0)),
            pl.BlockSpec((1, d), lambda i, f: (0, 0)),
        ],
        out_specs=pl.BlockSpec((tm, d), lambda i, f: (i, 0)),
        out_shape=jax.ShapeDtypeStruct((m, d), F32),
        scratch_shapes=[pltpu.VMEM((tm, d), BF16)],
        compiler_params=pltpu.CompilerParams(
            dimension_semantics=("parallel", "arbitrary"),
            vmem_limit_bytes=VMEM_LIMIT),
        name="ffn",
    )(x2d, norm_w.reshape(1, d), wg, wu, wd, final_w.reshape(1, d))


def kernel(x, norm_mix_w, w_in, w_out, norm_ffn_w, w_gate, w_up, w_down, norm_final_w):
    batch, seq, d = x.shape
    depth = w_in.shape[0]
    assert seq == DILATIONS[-1] * ATTN_BLK and seq % RET_CHUNK == 0
    slopes = jnp.exp2(-8.0 * jnp.arange(1, N_ATTN_HEADS + 1, dtype=F32) / N_ATTN_HEADS)
    log_gamma = jnp.log(1.0 - jnp.exp2(-5.0 - jnp.arange(N_RET_HEADS, dtype=F32)))

    xs = x.reshape(batch * seq, d)
    for layer in range(depth):
        w_in_l = w_in[layer].astype(BF16)
        side = ((w_gate[layer], 0, FFN_CAST_BLOCKS), (w_up[layer], 0, FFN_CAST_BLOCKS),
                (w_down[layer], FFN_CAST_BLOCKS, FFN_CAST_BLOCKS // 2),
                (w_out[layer], FFN_CAST_BLOCKS, FFN_CAST_BLOCKS // 2))
        nat, mod4, mod16, proj_ret, wg, wu, wd, wo = _in_proj(
            xs, norm_mix_w[layer], w_in_l, side, batch=batch, seq=seq)
        attn = _attention(nat, mod4, mod16, slopes, batch=batch, seq=seq)
        ret = _retention(proj_ret, log_gamma, batch=batch, seq=seq)
        xs = _out_proj(xs, attn, ret, wo)
        xs = _ffn(xs, norm_ffn_w[layer], wg, wu, wd, norm_final_w,
                  final_norm=(layer == depth - 1))
    return xs.reshape(batch, seq, d)
```

```python
import functools
import math

import jax
import jax.numpy as jnp
from jax import lax
from jax.experimental import pallas as pl
from jax.experimental.pallas import tpu as pltpu

F32 = jnp.float32
BF16 = jnp.bfloat16

LANES = 128
ATTN_HEAD_DIM = 128
N_ATTN_HEADS = 8
RET_HEAD_DIM = 256
N_RET_HEADS = 4
RET_CHUNK = 128
ATTN_BLK = 128
DILATIONS = (1, 4, 16)
ATTN_HEADS_PER_STEP = 2
ATTN_LOOKAHEAD = 14
RET_LOOKAHEAD = 2
NORM_EPS = 1e-6
MASK_VALUE = -1e30
LOG2E = math.log2(math.e)
QK_SCALE = LOG2E / math.sqrt(ATTN_HEAD_DIM)
NORM_ROWS = 128
FFN_CAST_BLOCKS = 64
VMEM_LIMIT = 60 * 1024 * 1024


def _rms_rows(x, w):
    ms = jnp.mean(x * x, axis=-1, keepdims=True)
    return x * lax.rsqrt(ms + NORM_EPS) * w


def _inproj_kernel(*refs, tm, tn, attn_tiles, q_tiles, casts):
    nc = len(casts)
    x_ref, nw_ref, w_ref = refs[:3]
    cast_in = refs[3:3 + nc]
    o1_ref, o4_ref, o16_ref, ret_ref = refs[3 + nc:7 + nc]
    cast_out = refs[7 + nc:7 + 2 * nc]
    h_even, h_odd, nat, mod4 = refs[7 + 2 * nc:]
    i = pl.program_id(0)
    j = pl.program_id(1)
    xr = x_ref.shape[0]
    n4, n16 = tm // 4, tm // 16

    step_no = (i - 1) * pl.num_programs(1) + j
    for src, dst, (first, count) in zip(cast_in, cast_out, casts):
        @pl.when((step_no >= first) & (step_no < first + count))
        def _(src=src, dst=dst):
            dst[...] = src[...].astype(dst.dtype)

    def norm_chunk(h_ref):
        r = pl.multiple_of(jnp.minimum(j, tm // xr - 1) * xr, xr)
        h_ref[pl.ds(r, xr), :] = _rms_rows(x_ref[...], nw_ref[...]).astype(BF16)

    def step(h_wr, h_rd, store):
        norm_chunk(h_wr)
        acc = jnp.dot(h_rd[...], w_ref[...], preferred_element_type=F32)
        for c in range(tn // LANES):
            store(c, acc[:, c * LANES:(c + 1) * LANES])

    @pl.when(i == 0)
    def _():
        norm_chunk(h_even)

    def store_attn(c, slab):
        slab = slab * jnp.where(j < q_tiles, QK_SCALE, 1.0)
        o1_ref[c] = slab.astype(o1_ref.dtype)
        b = c % nat.shape[0]
        nat[b] = slab
        for r in range(4):
            rows = nat[b, pl.ds(r, n4, stride=4), :]
            mod4[b, r * n4:(r + 1) * n4, :] = rows
            o4_ref[c, :, r * LANES:(r + 1) * LANES] = rows.astype(o4_ref.dtype)
        for r in range(16):
            a, r4 = divmod(r, 4)
            rows = mod4[b, pl.ds(r4 * n4 + a, n16, stride=4), :]
            o16_ref[c, :, r * LANES:(r + 1) * LANES] = rows.astype(o16_ref.dtype)

    def store_ret(c, slab):
        ret_ref[c] = slab.astype(ret_ref.dtype)

    for parity, (h_wr, h_rd) in enumerate(((h_even, h_odd), (h_odd, h_even))):
        mine = (i > 0) & (i % 2 == parity)
        pl.when(mine & (j < attn_tiles))(functools.partial(step, h_wr, h_rd, store_attn))
        pl.when(mine & (j >= attn_tiles))(functools.partial(step, h_wr, h_rd, store_ret))


def _in_proj(x2d, norm_w, w_bf16, side_f32, *, batch, seq, tm=1024, tn=1024):
    m, d = x2d.shape
    n = w_bf16.shape[1]
    n_attn = 3 * N_ATTN_HEADS * ATTN_HEAD_DIM
    ns = tn // LANES
    tiles = m // tm
    nj = n // tn
    attn_tiles = n_attn // tn
    assert 0 < attn_tiles < nj
    chunks = tm // NORM_ROWS
    while chunks > nj:
        chunks //= 2
    xr = tm // chunks
    per_b = seq // tm
    q_tiles = N_ATTN_HEADS * ATTN_HEAD_DIM // tn

    def t_of(i):
        return jnp.maximum(i - 1, 0)

    def ja(i, j):
        return jnp.where(i == 0, 0, jnp.minimum(j, attn_tiles - 1))

    def jr(i, j):
        return jnp.where(i == 0, 0, jnp.maximum(j - attn_tiles, 0))

    casts, cast_specs, cast_shapes = [], [], []
    for arr, first, blocks in side_f32:
        rows, cols = arr.shape
        assert rows % (blocks * 16) == 0
        assert first + blocks <= tiles * nj

        def at(i, j, first=first, blocks=blocks):
            return (jnp.clip((i - 1) * nj + j - first, 0, blocks - 1), 0)

        casts.append((first, blocks))
        cast_specs.append(pl.BlockSpec((rows // blocks, cols), at))
        cast_shapes.append(jax.ShapeDtypeStruct(arr.shape, BF16))

    return pl.pallas_call(
        functools.partial(_inproj_kernel, tm=tm, tn=tn, attn_tiles=attn_tiles, q_tiles=q_tiles,
                          casts=tuple(casts)),
        grid=(tiles + 1, nj),
        in_specs=[
            pl.BlockSpec((xr, d), lambda i, j: (
                jnp.minimum(i, tiles - 1) * chunks + jnp.minimum(j, chunks - 1), 0)),
            pl.BlockSpec((1, d), lambda i, j: (0, 0)),
            pl.BlockSpec((d, tn), lambda i, j: (0, jnp.where(i == 0, 0, j))),
        ] + cast_specs,
        out_specs=[
            pl.BlockSpec((ns, tm, LANES), lambda i, j: (ja(i, j), t_of(i), 0)),
            pl.BlockSpec((ns, None, tm // 4, 4 * LANES),
                         lambda i, j: (ja(i, j), t_of(i) // per_b, t_of(i) % per_b, 0)),
            pl.BlockSpec((ns, None, tm // 16, 16 * LANES),
                         lambda i, j: (ja(i, j), t_of(i) // per_b, t_of(i) % per_b, 0)),
            pl.BlockSpec((ns, tm, LANES), lambda i, j: (jr(i, j), t_of(i), 0)),
        ] + cast_specs,
        out_shape=[
            jax.ShapeDtypeStruct((n_attn // LANES, m, LANES), BF16),
            jax.ShapeDtypeStruct((n_attn // LANES, batch, seq // 4, 4 * LANES), BF16),
            jax.ShapeDtypeStruct((n_attn // LANES, batch, seq // 16, 16 * LANES), BF16),
            jax.ShapeDtypeStruct(((n - n_attn) // LANES, m, LANES), BF16),
        ] + cast_shapes,
        scratch_shapes=[pltpu.VMEM((tm, d), BF16),
                        pltpu.VMEM((tm, d), BF16),
                        pltpu.VMEM((2, tm, LANES), F32),
                        pltpu.VMEM((2, tm, LANES), F32)],
        compiler_params=pltpu.CompilerParams(
            dimension_semantics=("arbitrary", "arbitrary"),
            vmem_limit_bytes=VMEM_LIMIT),
        name="in_proj",
    )(x2d, norm_w.reshape(1, d), w_bf16, *[arr for arr, _, _ in side_f32])


def _attn_kernel(slopes_ref, qkv1, qkv4, qkv16, o_ref, bm, macc, lacc, oacc, onat, *, seq):
    blk = ATTN_BLK
    nblk = seq // blk
    per4 = nblk // 4
    heads = o_ref.shape[0]

    qi = lax.broadcasted_iota(jnp.int32, (blk, 2 * blk), 0)
    kj = lax.broadcasted_iota(jnp.int32, (blk, 2 * blk), 1)
    diff = qi - kj + blk
    valid = (diff >= 0) & (diff <= blk)
    dist = diff.astype(F32)
    for hd in range(heads):
        slope = slopes_ref[pl.program_id(1) * heads + hd]
        for pi, dil in enumerate(DILATIONS):
            bm[hd, pi] = jnp.where(valid, (-LOG2E * slope) * (dist * float(dil)), MASK_VALUE)

    def scores(hd, pi, q_at, k_at):
        kk = k_at()
        bias = bm[hd, pi] if kk.shape[0] == 2 * blk else bm[hd, pi, :, blk:]
        return lax.dot_general(q_at(), kk, (((1,), (1,)), ((), ())),
                               preferred_element_type=F32) + bias

    def finish(s, hd, pi, v_at, dst):
        mx = jnp.max(s, axis=-1, keepdims=True)
        p = jnp.exp2(s - mx).astype(BF16)
        vv = v_at()
        acc = jnp.dot(p, jnp.concatenate([vv, jnp.ones_like(vv)], axis=-1),
                      preferred_element_type=F32)
        oacc[hd, pi, dst, :] = acc[:, :LANES]
        lacc[hd, pi, dst, :] = acc[:, LANES:]
        macc[hd, pi, dst, :] = jnp.broadcast_to(mx, (blk, LANES))

    def p0_block(hd, i):
        rows = slice(i * blk, (i + 1) * blk)
        kr = slice(max(i - 1, 0) * blk, (i + 1) * blk)
        return (hd, 0, lambda: qkv1[0, hd, rows, :], lambda: qkv1[1, hd, kr, :],
                lambda: qkv1[2, hd, kr, :], rows)

    def p1_block(hd, r4, n):
        ln = slice(r4 * LANES, (r4 + 1) * LANES)
        qr = slice(n * blk, (n + 1) * blk)
        kr = slice(max(n - 1, 0) * blk, (n + 1) * blk)
        dst = slice((r4 * per4 + n) * blk, (r4 * per4 + n + 1) * blk)
        return (hd, 1, lambda: qkv4[0, hd, qr, ln], lambda: qkv4[1, hd, kr, ln],
                lambda: qkv4[2, hd, kr, ln], dst)

    def p2_block(hd, r):
        ln = slice(r * LANES, (r + 1) * LANES)
        dst = pl.ds((r % 4) * (seq // 4) + r // 4, blk, stride=4)
        return (hd, 2, lambda: qkv16[0, hd, :, ln], lambda: qkv16[1, hd, :, ln],
                lambda: qkv16[2, hd, :, ln], dst)

    def combine(hd, n):
        for r4 in range(4):
            rows = slice((r4 * per4 + n) * blk, (r4 * per4 + n + 1) * blk)
            nat = pl.ds(r4 + 4 * blk * n, blk, stride=4)
            at = (nat, rows, rows)
            ms = [macc[hd, pi, at[pi], :] for pi in range(3)]
            mx = jnp.maximum(jnp.maximum(ms[0], ms[1]), ms[2])
            es = [jnp.exp2(m - mx) for m in ms]
            den = (es[0] * lacc[hd, 0, nat, :] + es[1] * lacc[hd, 1, rows, :]
                   + es[2] * lacc[hd, 2, rows, :])
            num = (es[0] * oacc[hd, 0, nat, :] + es[1] * oacc[hd, 1, rows, :]
                   + es[2] * oacc[hd, 2, rows, :])
            onat[hd, nat, :] = num * (1.0 / den)
        done = slice(4 * blk * n, 4 * blk * (n + 1))
        o_ref[hd, done, :] = onat[hd, done, :].astype(o_ref.dtype)

    per_head = []
    for hd in range(heads):
        items = [(p2_block(hd, r), None) for r in range(nblk)]
        for n in range(per4):
            group = ([p0_block(hd, 4 * n + a) for a in range(4)]
                     + [p1_block(hd, r4, n) for r4 in range(4)])
            items += [(g, None) for g in group[:-1]] + [(group[-1], n)]
        per_head.append(items)
    work, after = [], {}
    for tup in zip(*per_head):
        for item, n in tup:
            if n is not None:
                after[len(work)] = (item[0], n)
            work.append(item)

    pending = {}
    for t in range(len(work) + ATTN_LOOKAHEAD):
        if t < len(work):
            hd, pi, q_at, k_at, _, _ = work[t]
            pending[t] = scores(hd, pi, q_at, k_at)
        b = t - ATTN_LOOKAHEAD
        if b >= 0:
            hd, pi, _, _, v_at, dst = work[b]
            finish(pending.pop(b), hd, pi, v_at, dst)
            if b in after:
                combine(*after[b])


def _attention(nat, mod4, mod16, slopes, *, batch, seq):
    h = N_ATTN_HEADS
    hps = ATTN_HEADS_PER_STEP
    views = [a.reshape((3, h) + a.shape[1:]) for a in (nat, mod4, mod16)]
    specs = [pl.BlockSpec((3, hps, seq, LANES), lambda b, hh: (0, hh, b, 0))]
    for d in DILATIONS[1:]:
        specs.append(pl.BlockSpec((3, hps, None, seq // d, d * LANES),
                                  lambda b, hh: (0, hh, b, 0, 0)))
    return pl.pallas_call(
        functools.partial(_attn_kernel, seq=seq),
        grid=(batch, h // hps),
        in_specs=[pl.BlockSpec(memory_space=pltpu.SMEM)] + specs,
        out_specs=pl.BlockSpec((hps, seq, LANES), lambda b, hh: (hh, b, 0)),
        out_shape=jax.ShapeDtypeStruct((h, batch * seq, LANES), BF16),
        scratch_shapes=[
            pltpu.VMEM((hps, 3, ATTN_BLK, 2 * ATTN_BLK), F32),
            pltpu.VMEM((hps, 3, seq, LANES), F32),
            pltpu.VMEM((hps, 3, seq, LANES), F32),
            pltpu.VMEM((hps, 3, seq, LANES), F32),
            pltpu.VMEM((hps, seq, LANES), F32),
        ],
        compiler_params=pltpu.CompilerParams(
            dimension_semantics=("parallel", "parallel"),
            vmem_limit_bytes=VMEM_LIMIT),
        name="dilated_attention",
    )(slopes, *views)


def _ret_kernel(lg_ref, qkvg_ref, o_ref, decay, zeta, xi, states, *, seq):
    q_ref, k_ref, v_ref, g_ref = (qkvg_ref.at[t] for t in range(4))
    c = RET_CHUNK
    dh = RET_HEAD_DIM
    nc = seq // c
    lg = lg_ref[pl.program_id(1)]
    k_scale = 1.0 / math.sqrt(dh)
    assert math.log2(k_scale).is_integer()

    ii = lax.broadcasted_iota(jnp.int32, (c, c), 0)
    jj = lax.broadcasted_iota(jnp.int32, (c, c), 1)
    dif = (ii - jj).astype(F32)
    decay[...] = jnp.where(dif >= 0, jnp.exp(lg * jnp.maximum(dif, 0.0)), 0.0) * k_scale
    idx = lax.broadcasted_iota(jnp.int32, (c, dh), 0).astype(F32)
    zeta[...] = jnp.exp(lg * (c - 1.0 - idx)) * k_scale
    xi[...] = jnp.exp(lg * (idx + 1.0))
    gamma_chunk = jnp.exp(jnp.full((dh, dh), lg * c, F32))

    def wide(ref, n):
        rows = slice(n * c, (n + 1) * c)
        return jnp.concatenate([ref[0, rows, :], ref[1, rows, :]], axis=-1)

    st = jnp.zeros((dh, dh), F32)
    for n in range(nc):
        states[n] = st.astype(BF16)
        if n + 1 < nc:
            kz = (wide(k_ref, n).astype(F32) * zeta[...]).astype(BF16)
            kv = lax.dot_general(kz, wide(v_ref, n), (((0,), (0,)), ((), ())),
                                 preferred_element_type=F32)
            st = st * gamma_chunk + kv

    def front(n):
        qn = wide(q_ref, n)
        sc = lax.dot_general(qn, wide(k_ref, n), (((1,), (1,)), ((), ())),
                             preferred_element_type=F32) * decay[...]
        cross = jnp.dot(qn, states[n], preferred_element_type=F32) * xi[...]
        return sc.astype(BF16), cross

    def back(n, sc, cross):
        ret = jnp.dot(sc, wide(v_ref, n), preferred_element_type=F32) + cross
        ret = ret * lax.rsqrt(jnp.mean(ret * ret, axis=-1, keepdims=True) + NORM_EPS)
        gate = wide(g_ref, n).astype(F32)
        out = (gate * jax.nn.sigmoid(gate) * ret).astype(o_ref.dtype)
        o_ref[0, n * c:(n + 1) * c, :] = out[:, :LANES]
        o_ref[1, n * c:(n + 1) * c, :] = out[:, LANES:]

    pending = {}
    for t in range(nc + RET_LOOKAHEAD):
        if t < nc:
            pending[t] = front(t)
        if t >= RET_LOOKAHEAD:
            back(t - RET_LOOKAHEAD, *pending.pop(t - RET_LOOKAHEAD))


def _retention(proj_hm, log_gamma, *, batch, seq):
    h = N_RET_HEADS
    blk = (2, seq, LANES)
    view = proj_hm.reshape((4, h, 2) + proj_hm.shape[1:])
    return pl.pallas_call(
        functools.partial(_ret_kernel, seq=seq),
        grid=(batch, h),
        in_specs=[
            pl.BlockSpec(memory_space=pltpu.SMEM),
            pl.BlockSpec((4, None, 2, seq, LANES), lambda b, hh: (0, hh, 0, b, 0)),
        ],
        out_specs=pl.BlockSpec(blk, lambda b, hh: (hh, b, 0)),
        out_shape=jax.ShapeDtypeStruct((2 * h, batch * seq, LANES), BF16),
        scratch_shapes=[
            pltpu.VMEM((RET_CHUNK, RET_CHUNK), F32),
            pltpu.VMEM((RET_CHUNK, RET_HEAD_DIM), F32),
            pltpu.VMEM((RET_CHUNK, RET_HEAD_DIM), F32),
            pltpu.VMEM((seq // RET_CHUNK, RET_HEAD_DIM, RET_HEAD_DIM), BF16),
        ],
        compiler_params=pltpu.CompilerParams(
            dimension_semantics=("parallel", "parallel"),
            vmem_limit_bytes=VMEM_LIMIT),
        name="retention",
    )(log_gamma, view)


def _outproj_kernel(x_ref, a_ref, r_ref, w_ref, o_ref):
    mixed = jnp.concatenate([a_ref[i] for i in range(a_ref.shape[0])]
                            + [r_ref[i] for i in range(r_ref.shape[0])], axis=-1)
    o_ref[...] = x_ref[...] + jnp.dot(mixed, w_ref[...], preferred_element_type=F32)


def _out_proj(x2d, attn_hm, ret_hm, w_out_bf16, *, tm=1024):
    m, d = x2d.shape
    na, nr = attn_hm.shape[0], ret_hm.shape[0]
    assert (na + nr) * LANES == w_out_bf16.shape[0]
    return pl.pallas_call(
        _outproj_kernel,
        grid=(m // tm,),
        in_specs=[
            pl.BlockSpec((tm, d), lambda i: (i, 0)),
            pl.BlockSpec((na, tm, LANES), lambda i: (0, i, 0)),
            pl.BlockSpec((nr, tm, LANES), lambda i: (0, i, 0)),
            pl.BlockSpec(w_out_bf16.shape, lambda i: (0, 0), pipeline_mode=pl.Buffered(1)),
        ],
        out_specs=pl.BlockSpec((tm, d), lambda i: (i, 0)),
        out_shape=jax.ShapeDtypeStruct((m, d), F32),
        compiler_params=pltpu.CompilerParams(
            dimension_semantics=("parallel",),
            vmem_limit_bytes=VMEM_LIMIT),
        name="out_proj",
    )(x2d, attn_hm, ret_hm, w_out_bf16)


def _ffn_kernel(x_ref, nw_ref, wg_ref, wu_ref, wd_ref, fw_ref, o_ref, h_ref, *, tm, final_norm):
    f = pl.program_id(1)

    @pl.when(f == 0)
    def _():
        def body(i, c):
            r = pl.multiple_of(i * NORM_ROWS, NORM_ROWS)
            x = x_ref[pl.ds(r, NORM_ROWS), :]
            h_ref[pl.ds(r, NORM_ROWS), :] = _rms_rows(x, nw_ref[...]).astype(h_ref.dtype)
            o_ref[pl.ds(r, NORM_ROWS), :] = x
            return c
        lax.fori_loop(0, tm // NORM_ROWS, body, 0)

    h = h_ref[...]
    g = jnp.dot(h, wg_ref[...], preferred_element_type=F32)
    u = jnp.dot(h, wu_ref[...], preferred_element_type=F32)
    a = (g * jax.nn.sigmoid(g) * u).astype(BF16)
    y = jnp.dot(a, wd_ref[...], preferred_element_type=F32)
    o_ref[...] += y

    if final_norm:
        @pl.when(f == pl.num_programs(1) - 1)
        def _():
            def body(i, c):
                r = pl.multiple_of(i * NORM_ROWS, NORM_ROWS)
                o_ref[pl.ds(r, NORM_ROWS), :] = _rms_rows(o_ref[pl.ds(r, NORM_ROWS), :],
                                                          fw_ref[...])
                return c
            lax.fori_loop(0, tm // NORM_ROWS, body, 0)


def _ffn(x2d, norm_w, wg, wu, wd, final_w, *, final_norm, tm=1024, tf=512):
    m, d = x2d.shape
    hid = wg.shape[1]
    return pl.pallas_call(
        functools.partial(_ffn_kernel, tm=tm, final_norm=final_norm),
        grid=(m // tm, hid // tf),
        in_specs=[
            pl.BlockSpec((tm, d), lambda i, f: (i, 0)),
            pl.BlockSpec((1, d), lambda i, f: (0, 0)),
            pl.BlockSpec((d, tf), lambda i, f: (0, f)),
            pl.BlockSpec((d, tf), lambda i, f: (0, f)),
            pl.BlockSpec((tf, d), lambda i, f: (f, 0)),
            pl.BlockSpec((1, d), lambda i, f: (0, 0)),
        ],
        out_specs=pl.BlockSpec((tm, d), lambda i, f: (i, 0)),
        out_shape=jax.ShapeDtypeStruct((m, d), F32),
        scratch_shapes=[pltpu.VMEM((tm, d), BF16)],
        compiler_params=pltpu.CompilerParams(
            dimension_semantics=("parallel", "arbitrary"),
            vmem_limit_bytes=VMEM_LIMIT),
        name="ffn",
    )(x2d, norm_w.reshape(1, d), wg, wu, wd, final_w.reshape(1, d))


def kernel(x, norm_mix_w, w_in, w_out, norm_ffn_w, w_gate, w_up, w_down, norm_final_w):
    batch, seq, d = x.shape
    depth = w_in.shape[0]
    assert seq == DILATIONS[-1] * ATTN_BLK and seq % RET_CHUNK == 0
    slopes = jnp.exp2(-8.0 * jnp.arange(1, N_ATTN_HEADS + 1, dtype=F32) / N_ATTN_HEADS)
    log_gamma = jnp.log(1.0 - jnp.exp2(-5.0 - jnp.arange(N_RET_HEADS, dtype=F32)))

    xs = x.reshape(batch * seq, d)
    for layer in range(depth):
        w_in_l = w_in[layer].astype(BF16)
        side = ((w_gate[layer], 0, FFN_CAST_BLOCKS), (w_up[layer], 0, FFN_CAST_BLOCKS),
                (w_down[layer], FFN_CAST_BLOCKS, FFN_CAST_BLOCKS // 2),
                (w_out[layer], FFN_CAST_BLOCKS, FFN_CAST_BLOCKS // 2))
        nat, mod4, mod16, proj_ret, wg, wu, wd, wo = _in_proj(
            xs, norm_mix_w[layer], w_in_l, side, batch=batch, seq=seq)
        attn = _attention(nat, mod4, mod16, slopes, batch=batch, seq=seq)
        ret = _retention(proj_ret, log_gamma, batch=batch, seq=seq)
        xs = _out_proj(xs, attn, ret, wo)
        xs = _ffn(xs, norm_ffn_w[layer], wg, wu, wd, norm_final_w,
                  final_norm=(layer == depth - 1))
    return xs.reshape(batch, seq, d)
```

```python
import functools
import math

import jax
import jax.numpy as jnp
from jax import lax
from jax.experimental import pallas as pl
from jax.experimental.pallas import tpu as pltpu

F32 = jnp.float32
BF16 = jnp.bfloat16

LANES = 128
ATTN_HEAD_DIM = 128
N_ATTN_HEADS = 8
RET_HEAD_DIM = 256
N_RET_HEADS = 4
RET_CHUNK = 128
ATTN_BLK = 128
DILATIONS = (1, 4, 16)
ATTN_HEADS_PER_STEP = 2
ATTN_LOOKAHEAD = 14
RET_LOOKAHEAD = 2
NORM_EPS = 1e-6
MASK_VALUE = -1e30
LOG2E = math.log2(math.e)
QK_SCALE = LOG2E / math.sqrt(ATTN_HEAD_DIM)
NORM_ROWS = 512
INPROJ_X_ROWS = 256
FFN_CAST_BLOCKS = 64
VMEM_LIMIT = 60 * 1024 * 1024


def _rms_rows(x, w):
    ms = jnp.mean(x * x, axis=-1, keepdims=True)
    return x * lax.rsqrt(ms + NORM_EPS) * w


def _inproj_kernel(*refs, tm, tn, attn_tiles, q_tiles, casts):
    nc = len(casts)
    x_ref, nw_ref, w_ref = refs[:3]
    cast_in = refs[3:3 + nc]
    o1_ref, o4_ref, o16_ref, ret_ref = refs[3 + nc:7 + nc]
    cast_out = refs[7 + nc:7 + 2 * nc]
    h_even, h_odd, nat, mod4 = refs[7 + 2 * nc:]
    i = pl.program_id(0)
    j = pl.program_id(1)
    xr = x_ref.shape[0]
    n4, n16 = tm // 4, tm // 16

    step_no = (i - 1) * pl.num_programs(1) + j
    for src, dst, (first, count) in zip(cast_in, cast_out, casts):
        @pl.when((step_no >= first) & (step_no < first + count))
        def _(src=src, dst=dst):
            dst[...] = src[...].astype(dst.dtype)

    def norm_chunk(h_ref):
        r = pl.multiple_of(jnp.minimum(j, tm // xr - 1) * xr, xr)
        h_ref[pl.ds(r, xr), :] = _rms_rows(x_ref[...], nw_ref[...]).astype(BF16)

    def step(h_wr, h_rd, store):
        norm_chunk(h_wr)
        acc = jnp.dot(h_rd[...], w_ref[...], preferred_element_type=F32)
        for c in range(tn // LANES):
            store(c, acc[:, c * LANES:(c + 1) * LANES])

    @pl.when(i == 0)
    def _():
        norm_chunk(h_even)

    def store_attn(c, slab):
        slab = slab * jnp.where(j < q_tiles, QK_SCALE, 1.0)
        o1_ref[c] = slab.astype(o1_ref.dtype)
        b = c % nat.shape[0]
        nat[b] = slab
        for r in range(4):
            rows = nat[b, pl.ds(r, n4, stride=4), :]
            mod4[b, r * n4:(r + 1) * n4, :] = rows
            o4_ref[c, :, r * LANES:(r + 1) * LANES] = rows.astype(o4_ref.dtype)
        for r in range(16):
            a, r4 = divmod(r, 4)
            rows = mod4[b, pl.ds(r4 * n4 + a, n16, stride=4), :]
            o16_ref[c, :, r * LANES:(r + 1) * LANES] = rows.astype(o16_ref.dtype)

    def store_ret(c, slab):
        ret_ref[c] = slab.astype(ret_ref.dtype)

    for parity, (h_wr, h_rd) in enumerate(((h_even, h_odd), (h_odd, h_even))):
        mine = (i > 0) & (i % 2 == parity)
        pl.when(mine & (j < attn_tiles))(functools.partial(step, h_wr, h_rd, store_attn))
        pl.when(mine & (j >= attn_tiles))(functools.partial(step, h_wr, h_rd, store_ret))


def _in_proj(x2d, norm_w, w_bf16, side_f32, *, batch, seq, tm=1024, tn=1024):
    m, d = x2d.shape
    n = w_bf16.shape[1]
    n_attn = 3 * N_ATTN_HEADS * ATTN_HEAD_DIM
    ns = tn // LANES
    tiles = m // tm
    nj = n // tn
    attn_tiles = n_attn // tn
    assert 0 < attn_tiles < nj
    xr = INPROJ_X_ROWS
    chunks = tm // xr
    assert chunks <= nj
    per_b = seq // tm
    q_tiles = N_ATTN_HEADS * ATTN_HEAD_DIM // tn

    def t_of(i):
        return jnp.maximum(i - 1, 0)

    def ja(i, j):
        return jnp.where(i == 0, 0, jnp.minimum(j, attn_tiles - 1))

    def jr(i, j):
        return jnp.where(i == 0, 0, jnp.maximum(j - attn_tiles, 0))

    casts, cast_specs, cast_shapes = [], [], []
    for arr, first, blocks in side_f32:
        rows, cols = arr.shape
        assert rows % (blocks * 16) == 0
        assert first + blocks <= tiles * nj

        def at(i, j, first=first, blocks=blocks):
            return (jnp.clip((i - 1) * nj + j - first, 0, blocks - 1), 0)

        casts.append((first, blocks))
        cast_specs.append(pl.BlockSpec((rows // blocks, cols), at))
        cast_shapes.append(jax.ShapeDtypeStruct(arr.shape, BF16))

    return pl.pallas_call(
        functools.partial(_inproj_kernel, tm=tm, tn=tn, attn_tiles=attn_tiles, q_tiles=q_tiles,
                          casts=tuple(casts)),
        grid=(tiles + 1, nj),
        in_specs=[
            pl.BlockSpec((xr, d), lambda i, j: (
                jnp.minimum(i, tiles - 1) * chunks + jnp.minimum(j, chunks - 1), 0)),
            pl.BlockSpec((1, d), lambda i, j: (0, 0)),
            pl.BlockSpec((d, tn), lambda i, j: (0, jnp.where(i == 0, 0, j))),
        ] + cast_specs,
        out_specs=[
            pl.BlockSpec((ns, tm, LANES), lambda i, j: (ja(i, j), t_of(i), 0)),
            pl.BlockSpec((ns, None, tm // 4, 4 * LANES),
                         lambda i, j: (ja(i, j), t_of(i) // per_b, t_of(i) % per_b, 0)),
            pl.BlockSpec((ns, None, tm // 16, 16 * LANES),
                         lambda i, j: (ja(i, j), t_of(i) // per_b, t_of(i) % per_b, 0)),
            pl.BlockSpec((ns, tm, LANES), lambda i, j: (jr(i, j), t_of(i), 0)),
        ] + cast_specs,
        out_shape=[
            jax.ShapeDtypeStruct((n_attn // LANES, m, LANES), BF16),
            jax.ShapeDtypeStruct((n_attn // LANES, batch, seq // 4, 4 * LANES), BF16),
            jax.ShapeDtypeStruct((n_attn // LANES, batch, seq // 16, 16 * LANES), BF16),
            jax.ShapeDtypeStruct(((n - n_attn) // LANES, m, LANES), BF16),
        ] + cast_shapes,
        scratch_shapes=[pltpu.VMEM((tm, d), BF16),
                        pltpu.VMEM((tm, d), BF16),
                        pltpu.VMEM((2, tm, LANES), F32),
                        pltpu.VMEM((2, tm, LANES), F32)],
        compiler_params=pltpu.CompilerParams(
            dimension_semantics=("arbitrary", "arbitrary"),
            vmem_limit_bytes=VMEM_LIMIT),
        name="in_proj",
    )(x2d, norm_w.reshape(1, d), w_bf16, *[arr for arr, _, _ in side_f32])


def _attn_kernel(slopes_ref, qkv1, qkv4, qkv16, o_ref, bm, macc, lacc, oacc, onat, *, seq):
    blk = ATTN_BLK
    nblk = seq // blk
    per4 = nblk // 4
    heads = o_ref.shape[0]

    qi = lax.broadcasted_iota(jnp.int32, (blk, 2 * blk), 0)
    kj = lax.broadcasted_iota(jnp.int32, (blk, 2 * blk), 1)
    diff = qi - kj + blk
    valid = (diff >= 0) & (diff <= blk)
    dist = diff.astype(F32)
    for hd in range(heads):
        slope = slopes_ref[pl.program_id(1) * heads + hd]
        for pi, dil in enumerate(DILATIONS):
            bm[hd, pi] = jnp.where(valid, (-LOG2E * slope) * (dist * float(dil)), MASK_VALUE)

    def scores(hd, pi, q_at, k_at):
        kk = k_at()
        bias = bm[hd, pi] if kk.shape[0] == 2 * blk else bm[hd, pi, :, blk:]
        return lax.dot_general(q_at(), kk, (((1,), (1,)), ((), ())),
                               preferred_element_type=F32) + bias

    def finish(s, hd, pi, v_at, dst):
        mx = jnp.max(s, axis=-1, keepdims=True)
        p = jnp.exp2(s - mx).astype(BF16)
        vv = v_at()
        acc = jnp.dot(p, jnp.concatenate([vv, jnp.ones_like(vv)], axis=-1),
                      preferred_element_type=F32)
        oacc[hd, pi, dst, :] = acc[:, :LANES]
        lacc[hd, pi, dst, :] = acc[:, LANES:]
        macc[hd, pi, dst, :] = jnp.broadcast_to(mx, (blk, LANES))

    def p0_block(hd, i):
        rows = slice(i * blk, (i + 1) * blk)
        kr = slice(max(i - 1, 0) * blk, (i + 1) * blk)
        return (hd, 0, lambda: qkv1[0, hd, rows, :], lambda: qkv1[1, hd, kr, :],
                lambda: qkv1[2, hd, kr, :], rows)

    def p1_block(hd, r4, n):
        ln = slice(r4 * LANES, (r4 + 1) * LANES)
        qr = slice(n * blk, (n + 1) * blk)
        kr = slice(max(n - 1, 0) * blk, (n + 1) * blk)
        dst = slice((r4 * per4 + n) * blk, (r4 * per4 + n + 1) * blk)
        return (hd, 1, lambda: qkv4[0, hd, qr, ln], lambda: qkv4[1, hd, kr, ln],
                lambda: qkv4[2, hd, kr, ln], dst)

    def p2_block(hd, r):
        ln = slice(r * LANES, (r + 1) * LANES)
        dst = pl.ds((r % 4) * (seq // 4) + r // 4, blk, stride=4)
        return (hd, 2, lambda: qkv16[0, hd, :, ln], lambda: qkv16[1, hd, :, ln],
                lambda: qkv16[2, hd, :, ln], dst)

    def combine(hd, n):
        for r4 in range(4):
            rows = slice((r4 * per4 + n) * blk, (r4 * per4 + n + 1) * blk)
            nat = pl.ds(r4 + 4 * blk * n, blk, stride=4)
            at = (nat, rows, rows)
            ms = [macc[hd, pi, at[pi], :] for pi in range(3)]
            mx = jnp.maximum(jnp.maximum(ms[0], ms[1]), ms[2])
            es = [jnp.exp2(m - mx) for m in ms]
            den = (es[0] * lacc[hd, 0, nat, :] + es[1] * lacc[hd, 1, rows, :]
                   + es[2] * lacc[hd, 2, rows, :])
            num = (es[0] * oacc[hd, 0, nat, :] + es[1] * oacc[hd, 1, rows, :]
                   + es[2] * oacc[hd, 2, rows, :])
            onat[hd, nat, :] = num * (1.0 / den)
        done = slice(4 * blk * n, 4 * blk * (n + 1))
        o_ref[hd, done, :] = onat[hd, done, :].astype(o_ref.dtype)

    per_head = []
    for hd in range(heads):
        items = [(p2_block(hd, r), None) for r in range(nblk)]
        for n in range(per4):
            group = ([p0_block(hd, 4 * n + a) for a in range(4)]
                     + [p1_block(hd, r4, n) for r4 in range(4)])
            items += [(g, None) for g in group[:-1]] + [(group[-1], n)]
        per_head.append(items)
    work, after = [], {}
    for tup in zip(*per_head):
        for item, n in tup:
            if n is not None:
                after[len(work)] = (item[0], n)
            work.append(item)

    pending = {}
    for t in range(len(work) + ATTN_LOOKAHEAD):
        if t < len(work):
            hd, pi, q_at, k_at, _, _ = work[t]
            pending[t] = scores(hd, pi, q_at, k_at)
        b = t - ATTN_LOOKAHEAD
        if b >= 0:
            hd, pi, _, _, v_at, dst = work[b]
            finish(pending.pop(b), hd, pi, v_at, dst)
            if b in after:
                combine(*after[b])


def _attention(nat, mod4, mod16, slopes, *, batch, seq):
    h = N_ATTN_HEADS
    hps = ATTN_HEADS_PER_STEP
    views = [a.reshape((3, h) + a.shape[1:]) for a in (nat, mod4, mod16)]
    specs = [pl.BlockSpec((3, hps, seq, LANES), lambda b, hh: (0, hh, b, 0))]
    for d in DILATIONS[1:]:
        specs.append(pl.BlockSpec((3, hps, None, seq // d, d * LANES),
                                  lambda b, hh: (0, hh, b, 0, 0)))
    return pl.pallas_call(
        functools.partial(_attn_kernel, seq=seq),
        grid=(batch, h // hps),
        in_specs=[pl.BlockSpec(memory_space=pltpu.SMEM)] + specs,
        out_specs=pl.BlockSpec((hps, seq, LANES), lambda b, hh: (hh, b, 0)),
        out_shape=jax.ShapeDtypeStruct((h, batch * seq, LANES), BF16),
        scratch_shapes=[
            pltpu.VMEM((hps, 3, ATTN_BLK, 2 * ATTN_BLK), F32),
            pltpu.VMEM((hps, 3, seq, LANES), F32),
            pltpu.VMEM((hps, 3, seq, LANES), F32),
            pltpu.VMEM((hps, 3, seq, LANES), F32),
            pltpu.VMEM((hps, seq, LANES), F32),
        ],
        compiler_params=pltpu.CompilerParams(
            dimension_semantics=("parallel", "parallel"),
            vmem_limit_bytes=VMEM_LIMIT),
        name="dilated_attention",
    )(slopes, *views)


def _ret_kernel(lg_ref, qkvg_ref, o_ref, decay, zeta, xi, states, *, seq):
    q_ref, k_ref, v_ref, g_ref = (qkvg_ref.at[t] for t in range(4))
    c = RET_CHUNK
    dh = RET_HEAD_DIM
    nc = seq // c
    lg = lg_ref[pl.program_id(1)]
    k_scale = 1.0 / math.sqrt(dh)
    assert math.log2(k_scale).is_integer()

    ii = lax.broadcasted_iota(jnp.int32, (c, c), 0)
    jj = lax.broadcasted_iota(jnp.int32, (c, c), 1)
    dif = (ii - jj).astype(F32)
    decay[...] = jnp.where(dif >= 0, jnp.exp(lg * jnp.maximum(dif, 0.0)), 0.0) * k_scale
    idx = lax.broadcasted_iota(jnp.int32, (c, dh), 0).astype(F32)
    zeta[...] = jnp.exp(lg * (c - 1.0 - idx)) * k_scale
    xi[...] = jnp.exp(lg * (idx + 1.0))
    gamma_chunk = jnp.exp(jnp.full((dh, dh), lg * c, F32))

    def wide(ref, n):
        rows = slice(n * c, (n + 1) * c)
        return jnp.concatenate([ref[0, rows, :], ref[1, rows, :]], axis=-1)

    st = jnp.zeros((dh, dh), F32)
    for n in range(nc):
        states[n] = st.astype(BF16)
        if n + 1 < nc:
            kz = (wide(k_ref, n).astype(F32) * zeta[...]).astype(BF16)
            kv = lax.dot_general(kz, wide(v_ref, n), (((0,), (0,)), ((), ())),
                                 preferred_element_type=F32)
            st = st * gamma_chunk + kv

    def front(n):
        qn = wide(q_ref, n)
        sc = lax.dot_general(qn, wide(k_ref, n), (((1,), (1,)), ((), ())),
                             preferred_element_type=F32) * decay[...]
        cross = jnp.dot(qn, states[n], preferred_element_type=F32) * xi[...]
        return sc.astype(BF16), cross

    def back(n, sc, cross):
        ret = jnp.dot(sc, wide(v_ref, n), preferred_element_type=F32) + cross
        ret = ret * lax.rsqrt(jnp.mean(ret * ret, axis=-1, keepdims=True) + NORM_EPS)
        gate = wide(g_ref, n).astype(F32)
        out = (gate * jax.nn.sigmoid(gate) * ret).astype(o_ref.dtype)
        o_ref[0, n * c:(n + 1) * c, :] = out[:, :LANES]
        o_ref[1, n * c:(n + 1) * c, :] = out[:, LANES:]

    pending = {}
    for t in range(nc + RET_LOOKAHEAD):
        if t < nc:
            pending[t] = front(t)
        if t >= RET_LOOKAHEAD:
            back(t - RET_LOOKAHEAD, *pending.pop(t - RET_LOOKAHEAD))


def _retention(proj_hm, log_gamma, *, batch, seq):
    h = N_RET_HEADS
    blk = (2, seq, LANES)
    view = proj_hm.reshape((4, h, 2) + proj_hm.shape[1:])
    return pl.pallas_call(
        functools.partial(_ret_kernel, seq=seq),
        grid=(batch, h),
        in_specs=[
            pl.BlockSpec(memory_space=pltpu.SMEM),
            pl.BlockSpec((4, None, 2, seq, LANES), lambda b, hh: (0, hh, 0, b, 0)),
        ],
        out_specs=pl.BlockSpec(blk, lambda b, hh: (hh, b, 0)),
        out_shape=jax.ShapeDtypeStruct((2 * h, batch * seq, LANES), BF16),
        scratch_shapes=[
            pltpu.VMEM((RET_CHUNK, RET_CHUNK), F32),
            pltpu.VMEM((RET_CHUNK, RET_HEAD_DIM), F32),
            pltpu.VMEM((RET_CHUNK, RET_HEAD_DIM), F32),
            pltpu.VMEM((seq // RET_CHUNK, RET_HEAD_DIM, RET_HEAD_DIM), BF16),
        ],
        compiler_params=pltpu.CompilerParams(
            dimension_semantics=("parallel", "parallel"),
            vmem_limit_bytes=VMEM_LIMIT),
        name="retention",
    )(log_gamma, view)


def _outproj_kernel(x_ref, a_ref, r_ref, w_ref, o_ref):
    mixed = jnp.concatenate([a_ref[i] for i in range(a_ref.shape[0])]
                            + [r_ref[i] for i in range(r_ref.shape[0])], axis=-1)
    o_ref[...] = x_ref[...] + jnp.dot(mixed, w_ref[...], preferred_element_type=F32)


def _out_proj(x2d, attn_hm, ret_hm, w_out_bf16, *, tm=1024):
    m, d = x2d.shape
    na, nr = attn_hm.shape[0], ret_hm.shape[0]
    assert (na + nr) * LANES == w_out_bf16.shape[0]
    return pl.pallas_call(
        _outproj_kernel,
        grid=(m // tm,),
        in_specs=[
            pl.BlockSpec((tm, d), lambda i: (i, 0)),
            pl.BlockSpec((na, tm, LANES), lambda i: (0, i, 0)),
            pl.BlockSpec((nr, tm, LANES), lambda i: (0, i, 0)),
            pl.BlockSpec(w_out_bf16.shape, lambda i: (0, 0), pipeline_mode=pl.Buffered(1)),
        ],
        out_specs=pl.BlockSpec((tm, d), lambda i: (i, 0)),
        out_shape=jax.ShapeDtypeStruct((m, d), F32),
        compiler_params=pltpu.CompilerParams(
            dimension_semantics=("parallel",),
            vmem_limit_bytes=VMEM_LIMIT),
        name="out_proj",
    )(x2d, attn_hm, ret_hm, w_out_bf16)


def _ffn_kernel(x_ref, nw_ref, wg_ref, wu_ref, wd_ref, fw_ref, o_ref, h_ref, *, tm, final_norm):
    f = pl.program_id(1)

    @pl.when(f == 0)
    def _():
        def body(i, c):
            r = pl.multiple_of(i * NORM_ROWS, NORM_ROWS)
            x = x_ref[pl.ds(r, NORM_ROWS), :]
            h_ref[pl.ds(r, NORM_ROWS), :] = _rms_rows(x, nw_ref[...]).astype(h_ref.dtype)
            o_ref[pl.ds(r, NORM_ROWS), :] = x
            return c
        lax.fori_loop(0, tm // NORM_ROWS, body, 0)

    h = h_ref[...]
    g = jnp.dot(h, wg_ref[...], preferred_element_type=F32)
    u = jnp.dot(h, wu_ref[...], preferred_element_type=F32)
    a = (g * jax.nn.sigmoid(g) * u).astype(BF16)
    y = jnp.dot(a, wd_ref[...], preferred_element_type=F32)
    o_ref[...] += y

    if final_norm:
        @pl.when(f == pl.num_programs(1) - 1)
        def _():
            def body(i, c):
                r = pl.multiple_of(i * NORM_ROWS, NORM_ROWS)
                o_ref[pl.ds(r, NORM_ROWS), :] = _rms_rows(o_ref[pl.ds(r, NORM_ROWS), :],
                                                          fw_ref[...])
                return c
            lax.fori_loop(0, tm // NORM_ROWS, body, 0)


def _ffn(x2d, norm_w, wg, wu, wd, final_w, *, final_norm, tm=1024, tf=512):
    m, d = x2d.shape
    hid = wg.shape[1]
    return pl.pallas_call(
        functools.partial(_ffn_kernel, tm=tm, final_norm=final_norm),
        grid=(m // tm, hid // tf),
        in_specs=[
            pl.BlockSpec((tm, d), lambda i, f: (i, 0)),
            pl.BlockSpec((1, d), lambda i, f: (0, 0)),
            pl.BlockSpec((d, tf), lambda i, f: (0, f)),
            pl.BlockSpec((d, tf), lambda i, f: (0, f)),
            pl.BlockSpec((tf, d), lambda i, f: (f, 0)),
            pl.BlockSpec((1, d), lambda i, f: (0, 0)),
        ],
        out_specs=pl.BlockSpec((tm, d), lambda i, f: (i, 0)),
        out_shape=jax.ShapeDtypeStruct((m, d), F32),
        scratch_shapes=[pltpu.VMEM((tm, d), BF16)],
        compiler_params=pltpu.CompilerParams(
            dimension_semantics=("parallel", "arbitrary"),
            vmem_limit_bytes=VMEM_LIMIT),
        name="ffn",
    )(x2d, norm_w.reshape(1, d), wg, wu, wd, final_w.reshape(1, d))


def kernel(x, norm_mix_w, w_in, w_out, norm_ffn_w, w_gate, w_up, w_down, norm_final_w):
    batch, seq, d = x.shape
    depth = w_in.shape[0]
    assert seq == DILATIONS[-1] * ATTN_BLK and seq % RET_CHUNK == 0
    slopes = jnp.exp2(-8.0 * jnp.arange(1, N_ATTN_HEADS + 1, dtype=F32) / N_ATTN_HEADS)
    log_gamma = jnp.log(1.0 - jnp.exp2(-5.0 - jnp.arange(N_RET_HEADS, dtype=F32)))

    xs = x.reshape(batch * seq, d)
    for layer in range(depth):
        w_in_l = w_in[layer].astype(BF16)
        side = ((w_gate[layer], 0, FFN_CAST_BLOCKS), (w_up[layer], 0, FFN_CAST_BLOCKS),
                (w_down[layer], FFN_CAST_BLOCKS, FFN_CAST_BLOCKS // 2),
                (w_out[layer], FFN_CAST_BLOCKS, FFN_CAST_BLOCKS // 2))
        nat, mod4, mod16, proj_ret, wg, wu, wd, wo = _in_proj(
            xs, norm_mix_w[layer], w_in_l, side, batch=batch, seq=seq)
        attn = _attention(nat, mod4, mod16, slopes, batch=batch, seq=seq)
        ret = _retention(proj_ret, log_gamma, batch=batch, seq=seq)
        xs = _out_proj(xs, attn, ret, wo)
        xs = _ffn(xs, norm_ffn_w[layer], wg, wu, wd, norm_final_w,
                  final_norm=(layer == depth - 1))
    return xs.reshape(batch, seq, d)
```

```python
import functools
import math

import jax
import jax.numpy as jnp
from jax import lax
from jax.experimental import pallas as pl
from jax.experimental.pallas import tpu as pltpu

F32 = jnp.float32
BF16 = jnp.bfloat16

LANES = 128
ATTN_HEAD_DIM = 128
N_ATTN_HEADS = 8
RET_HEAD_DIM = 256
N_RET_HEADS = 4
RET_CHUNK = 128
ATTN_BLK = 128
DILATIONS = (1, 4, 16)
ATTN_HEADS_PER_STEP = 2
ATTN_LOOKAHEAD = 14
RET_HEADS_PER_STEP = 2
RET_LOOKAHEAD = 1
NORM_EPS = 1e-6
MASK_VALUE = -1e30
LOG2E = math.log2(math.e)
QK_SCALE = LOG2E / math.sqrt(ATTN_HEAD_DIM)
NORM_ROWS = 512
INPROJ_X_ROWS = 256
FFN_CAST_BLOCKS = 64
VMEM_LIMIT = 60 * 1024 * 1024


def _rms_rows(x, w):
    ms = jnp.mean(x * x, axis=-1, keepdims=True)
    return x * lax.rsqrt(ms + NORM_EPS) * w


def _inproj_kernel(*refs, tm, tn, attn_tiles, q_tiles, casts):
    nc = len(casts)
    x_ref, nw_ref, w_ref = refs[:3]
    cast_in = refs[3:3 + nc]
    o1_ref, o4_ref, o16_ref, ret_ref = refs[3 + nc:7 + nc]
    cast_out = refs[7 + nc:7 + 2 * nc]
    h_even, h_odd, nat, mod4 = refs[7 + 2 * nc:]
    i = pl.program_id(0)
    j = pl.program_id(1)
    xr = x_ref.shape[0]
    n4, n16 = tm // 4, tm // 16

    step_no = (i - 1) * pl.num_programs(1) + j
    for src, dst, (first, count) in zip(cast_in, cast_out, casts):
        @pl.when((step_no >= first) & (step_no < first + count))
        def _(src=src, dst=dst):
            dst[...] = src[...].astype(dst.dtype)

    def norm_chunk(h_ref):
        r = pl.multiple_of(jnp.minimum(j, tm // xr - 1) * xr, xr)
        h_ref[pl.ds(r, xr), :] = _rms_rows(x_ref[...], nw_ref[...]).astype(BF16)

    def step(h_wr, h_rd, store):
        norm_chunk(h_wr)
        acc = jnp.dot(h_rd[...], w_ref[...], preferred_element_type=F32)
        for c in range(tn // LANES):
            store(c, acc[:, c * LANES:(c + 1) * LANES])

    @pl.when(i == 0)
    def _():
        norm_chunk(h_even)

    def store_attn(c, slab):
        slab = slab * jnp.where(j < q_tiles, QK_SCALE, 1.0)
        o1_ref[c] = slab.astype(o1_ref.dtype)
        b = c % nat.shape[0]
        nat[b] = slab
        for r in range(4):
            rows = nat[b, pl.ds(r, n4, stride=4), :]
            mod4[b, r * n4:(r + 1) * n4, :] = rows
            o4_ref[c, :, r * LANES:(r + 1) * LANES] = rows.astype(o4_ref.dtype)
        for r in range(16):
            a, r4 = divmod(r, 4)
            rows = mod4[b, pl.ds(r4 * n4 + a, n16, stride=4), :]
            o16_ref[c, :, r * LANES:(r + 1) * LANES] = rows.astype(o16_ref.dtype)

    def store_ret(c, slab):
        ret_ref[c] = slab.astype(ret_ref.dtype)

    for parity, (h_wr, h_rd) in enumerate(((h_even, h_odd), (h_odd, h_even))):
        mine = (i > 0) & (i % 2 == parity)
        pl.when(mine & (j < attn_tiles))(functools.partial(step, h_wr, h_rd, store_attn))
        pl.when(mine & (j >= attn_tiles))(functools.partial(step, h_wr, h_rd, store_ret))


def _in_proj(x2d, norm_w, w_bf16, side_f32, *, batch, seq, tm=1024, tn=1024):
    m, d = x2d.shape
    n = w_bf16.shape[1]
    n_attn = 3 * N_ATTN_HEADS * ATTN_HEAD_DIM
    ns = tn // LANES
    tiles = m // tm
    nj = n // tn
    attn_tiles = n_attn // tn
    assert 0 < attn_tiles < nj
    xr = INPROJ_X_ROWS
    chunks = tm // xr
    assert chunks <= nj
    per_b = seq // tm
    q_tiles = N_ATTN_HEADS * ATTN_HEAD_DIM // tn

    def t_of(i):
        return jnp.maximum(i - 1, 0)

    def ja(i, j):
        return jnp.where(i == 0, 0, jnp.minimum(j, attn_tiles - 1))

    def jr(i, j):
        return jnp.where(i == 0, 0, jnp.maximum(j - attn_tiles, 0))

    casts, cast_specs, cast_shapes = [], [], []
    for arr, first, blocks in side_f32:
        rows, cols = arr.shape
        assert rows % (blocks * 16) == 0
        assert first + blocks <= tiles * nj

        def at(i, j, first=first, blocks=blocks):
            return (jnp.clip((i - 1) * nj + j - first, 0, blocks - 1), 0)

        casts.append((first, blocks))
        cast_specs.append(pl.BlockSpec((rows // blocks, cols), at))
        cast_shapes.append(jax.ShapeDtypeStruct(arr.shape, BF16))

    return pl.pallas_call(
        functools.partial(_inproj_kernel, tm=tm, tn=tn, attn_tiles=attn_tiles, q_tiles=q_tiles,
                          casts=tuple(casts)),
        grid=(tiles + 1, nj),
        in_specs=[
            pl.BlockSpec((xr, d), lambda i, j: (
                jnp.minimum(i, tiles - 1) * chunks + jnp.minimum(j, chunks - 1), 0)),
            pl.BlockSpec((1, d), lambda i, j: (0, 0)),
            pl.BlockSpec((d, tn), lambda i, j: (0, jnp.where(i == 0, 0, j))),
        ] + cast_specs,
        out_specs=[
            pl.BlockSpec((ns, tm, LANES), lambda i, j: (ja(i, j), t_of(i), 0)),
            pl.BlockSpec((ns, None, tm // 4, 4 * LANES),
                         lambda i, j: (ja(i, j), t_of(i) // per_b, t_of(i) % per_b, 0)),
            pl.BlockSpec((ns, None, tm // 16, 16 * LANES),
                         lambda i, j: (ja(i, j), t_of(i) // per_b, t_of(i) % per_b, 0)),
            pl.BlockSpec((ns, tm, LANES), lambda i, j: (jr(i, j), t_of(i), 0)),
        ] + cast_specs,
        out_shape=[
            jax.ShapeDtypeStruct((n_attn // LANES, m, LANES), BF16),
            jax.ShapeDtypeStruct((n_attn // LANES, batch, seq // 4, 4 * LANES), BF16),
            jax.ShapeDtypeStruct((n_attn // LANES, batch, seq // 16, 16 * LANES), BF16),
            jax.ShapeDtypeStruct(((n - n_attn) // LANES, m, LANES), BF16),
        ] + cast_shapes,
        scratch_shapes=[pltpu.VMEM((tm, d), BF16),
                        pltpu.VMEM((tm, d), BF16),
                        pltpu.VMEM((2, tm, LANES), F32),
                        pltpu.VMEM((2, tm, LANES), F32)],
        compiler_params=pltpu.CompilerParams(
            dimension_semantics=("arbitrary", "arbitrary"),
            vmem_limit_bytes=VMEM_LIMIT),
        name="in_proj",
    )(x2d, norm_w.reshape(1, d), w_bf16, *[arr for arr, _, _ in side_f32])


def _attn_kernel(slopes_ref, qkv1, qkv4, qkv16, o_ref, bm, macc, lacc, oacc, onat, *, seq):
    blk = ATTN_BLK
    nblk = seq // blk
    per4 = nblk // 4
    heads = o_ref.shape[0]

    qi = lax.broadcasted_iota(jnp.int32, (blk, 2 * blk), 0)
    kj = lax.broadcasted_iota(jnp.int32, (blk, 2 * blk), 1)
    diff = qi - kj + blk
    valid = (diff >= 0) & (diff <= blk)
    dist = diff.astype(F32)
    for hd in range(heads):
        slope = slopes_ref[pl.program_id(1) * heads + hd]
        for pi, dil in enumerate(DILATIONS):
            bm[hd, pi] = jnp.where(valid, (-LOG2E * slope) * (dist * float(dil)), MASK_VALUE)

    def scores(hd, pi, q_at, k_at):
        kk = k_at()
        bias = bm[hd, pi] if kk.shape[0] == 2 * blk else bm[hd, pi, :, blk:]
        return lax.dot_general(q_at(), kk, (((1,), (1,)), ((), ())),
                               preferred_element_type=F32) + bias

    def finish(s, hd, pi, v_at, dst):
        mx = jnp.max(s, axis=-1, keepdims=True)
        p = jnp.exp2(s - mx).astype(BF16)
        vv = v_at()
        acc = jnp.dot(p, jnp.concatenate([vv, jnp.ones_like(vv)], axis=-1),
                      preferred_element_type=F32)
        oacc[hd, pi, dst, :] = acc[:, :LANES]
        lacc[hd, pi, dst, :] = acc[:, LANES:]
        macc[hd, pi, dst, :] = jnp.broadcast_to(mx, (blk, LANES))

    def p0_block(hd, i):
        rows = slice(i * blk, (i + 1) * blk)
        kr = slice(max(i - 1, 0) * blk, (i + 1) * blk)
        return (hd, 0, lambda: qkv1[0, hd, rows, :], lambda: qkv1[1, hd, kr, :],
                lambda: qkv1[2, hd, kr, :], rows)

    def p1_block(hd, r4, n):
        ln = slice(r4 * LANES, (r4 + 1) * LANES)
        qr = slice(n * blk, (n + 1) * blk)
        kr = slice(max(n - 1, 0) * blk, (n + 1) * blk)
        dst = slice((r4 * per4 + n) * blk, (r4 * per4 + n + 1) * blk)
        return (hd, 1, lambda: qkv4[0, hd, qr, ln], lambda: qkv4[1, hd, kr, ln],
                lambda: qkv4[2, hd, kr, ln], dst)

    def p2_block(hd, r):
        ln = slice(r * LANES, (r + 1) * LANES)
        dst = pl.ds((r % 4) * (seq // 4) + r // 4, blk, stride=4)
        return (hd, 2, lambda: qkv16[0, hd, :, ln], lambda: qkv16[1, hd, :, ln],
                lambda: qkv16[2, hd, :, ln], dst)

    def combine(hd, n):
        for r4 in range(4):
            rows = slice((r4 * per4 + n) * blk, (r4 * per4 + n + 1) * blk)
            nat = pl.ds(r4 + 4 * blk * n, blk, stride=4)
            at = (nat, rows, rows)
            ms = [macc[hd, pi, at[pi], :] for pi in range(3)]
            mx = jnp.maximum(jnp.maximum(ms[0], ms[1]), ms[2])
            es = [jnp.exp2(m - mx) for m in ms]
            den = (es[0] * lacc[hd, 0, nat, :] + es[1] * lacc[hd, 1, rows, :]
                   + es[2] * lacc[hd, 2, rows, :])
            num = (es[0] * oacc[hd, 0, nat, :] + es[1] * oacc[hd, 1, rows, :]
                   + es[2] * oacc[hd, 2, rows, :])
            onat[hd, nat, :] = num * (1.0 / den)
        done = slice(4 * blk * n, 4 * blk * (n + 1))
        o_ref[hd, done, :] = onat[hd, done, :].astype(o_ref.dtype)

    per_head = []
    for hd in range(heads):
        items = [(p2_block(hd, r), None) for r in range(nblk)]
        for n in range(per4):
            group = ([p0_block(hd, 4 * n + a) for a in range(4)]
                     + [p1_block(hd, r4, n) for r4 in range(4)])
            items += [(g, None) for g in group[:-1]] + [(group[-1], n)]
        per_head.append(items)
    work, after = [], {}
    for tup in zip(*per_head):
        for item, n in tup:
            if n is not None:
                after[len(work)] = (item[0], n)
            work.append(item)

    pending = {}
    for t in range(len(work) + ATTN_LOOKAHEAD):
        if t < len(work):
            hd, pi, q_at, k_at, _, _ = work[t]
            pending[t] = scores(hd, pi, q_at, k_at)
        b = t - ATTN_LOOKAHEAD
        if b >= 0:
            hd, pi, _, _, v_at, dst = work[b]
            finish(pending.pop(b), hd, pi, v_at, dst)
            if b in after:
                combine(*after[b])


def _attention(nat, mod4, mod16, slopes, *, batch, seq):
    h = N_ATTN_HEADS
    hps = ATTN_HEADS_PER_STEP
    views = [a.reshape((3, h) + a.shape[1:]) for a in (nat, mod4, mod16)]
    specs = [pl.BlockSpec((3, hps, seq, LANES), lambda b, hh: (0, hh, b, 0))]
    for d in DILATIONS[1:]:
        specs.append(pl.BlockSpec((3, hps, None, seq // d, d * LANES),
                                  lambda b, hh: (0, hh, b, 0, 0)))
    return pl.pallas_call(
        functools.partial(_attn_kernel, seq=seq),
        grid=(batch, h // hps),
        in_specs=[pl.BlockSpec(memory_space=pltpu.SMEM)] + specs,
        out_specs=pl.BlockSpec((hps, seq, LANES), lambda b, hh: (hh, b, 0)),
        out_shape=jax.ShapeDtypeStruct((h, batch * seq, LANES), BF16),
        scratch_shapes=[
            pltpu.VMEM((hps, 3, ATTN_BLK, 2 * ATTN_BLK), F32),
            pltpu.VMEM((hps, 3, seq, LANES), F32),
            pltpu.VMEM((hps, 3, seq, LANES), F32),
            pltpu.VMEM((hps, 3, seq, LANES), F32),
            pltpu.VMEM((hps, seq, LANES), F32),
        ],
        compiler_params=pltpu.CompilerParams(
            dimension_semantics=("parallel", "parallel"),
            vmem_limit_bytes=VMEM_LIMIT),
        name="dilated_attention",
    )(slopes, *views)


def _ret_kernel(lg_ref, qkvg_ref, o_ref, decay, zeta, xi, states, *, seq):
    c = RET_CHUNK
    dh = RET_HEAD_DIM
    nc = seq // c
    heads = qkvg_ref.shape[1]
    k_scale = 1.0 / math.sqrt(dh)
    assert math.log2(k_scale).is_integer()

    ii = lax.broadcasted_iota(jnp.int32, (c, c), 0)
    jj = lax.broadcasted_iota(jnp.int32, (c, c), 1)
    dif = (ii - jj).astype(F32)
    idx = lax.broadcasted_iota(jnp.int32, (c, dh), 0).astype(F32)
    gamma_chunk = []
    for hd in range(heads):
        lg = lg_ref[pl.program_id(1) * heads + hd]
        decay[hd] = jnp.where(dif >= 0, jnp.exp(lg * jnp.maximum(dif, 0.0)), 0.0) * k_scale
        zeta[hd] = jnp.exp(lg * (c - 1.0 - idx)) * k_scale
        xi[hd] = jnp.exp(lg * (idx + 1.0))
        gamma_chunk.append(jnp.exp(jnp.full((dh, dh), lg * c, F32)))

    def wide(t, hd, n):
        rows = slice(n * c, (n + 1) * c)
        return jnp.concatenate([qkvg_ref[t, hd, 0, rows, :], qkvg_ref[t, hd, 1, rows, :]],
                               axis=-1)

    st = [jnp.zeros((dh, dh), F32) for _ in range(heads)]
    for n in range(nc):
        for hd in range(heads):
            states[hd, n] = st[hd].astype(BF16)
            if n + 1 < nc:
                kz = (wide(1, hd, n).astype(F32) * zeta[hd]).astype(BF16)
                kv = lax.dot_general(kz, wide(2, hd, n), (((0,), (0,)), ((), ())),
                                     preferred_element_type=F32)
                st[hd] = st[hd] * gamma_chunk[hd] + kv

    def front(hd, n):
        qn = wide(0, hd, n)
        sc = lax.dot_general(qn, wide(1, hd, n), (((1,), (1,)), ((), ())),
                             preferred_element_type=F32) * decay[hd]
        cross = jnp.dot(qn, states[hd, n], preferred_element_type=F32) * xi[hd]
        return sc.astype(BF16), cross

    def back(hd, n, sc, cross):
        ret = jnp.dot(sc, wide(2, hd, n), preferred_element_type=F32) + cross
        ret = ret * lax.rsqrt(jnp.mean(ret * ret, axis=-1, keepdims=True) + NORM_EPS)
        gate = wide(3, hd, n).astype(F32)
        out = (gate * jax.nn.sigmoid(gate) * ret).astype(o_ref.dtype)
        o_ref[2 * hd, n * c:(n + 1) * c, :] = out[:, :LANES]
        o_ref[2 * hd + 1, n * c:(n + 1) * c, :] = out[:, LANES:]

    pending = {}
    for t in range(nc + RET_LOOKAHEAD):
        for hd in range(heads):
            if t < nc:
                pending[hd, t] = front(hd, t)
            if t >= RET_LOOKAHEAD:
                back(hd, t - RET_LOOKAHEAD, *pending.pop((hd, t - RET_LOOKAHEAD)))


def _retention(proj_hm, log_gamma, *, batch, seq):
    h = N_RET_HEADS
    hps = RET_HEADS_PER_STEP
    view = proj_hm.reshape((4, h, 2) + proj_hm.shape[1:])
    return pl.pallas_call(
        functools.partial(_ret_kernel, seq=seq),
        grid=(batch, h // hps),
        in_specs=[
            pl.BlockSpec(memory_space=pltpu.SMEM),
            pl.BlockSpec((4, hps, 2, seq, LANES), lambda b, hh: (0, hh, 0, b, 0)),
        ],
        out_specs=pl.BlockSpec((2 * hps, seq, LANES), lambda b, hh: (hh, b, 0)),
        out_shape=jax.ShapeDtypeStruct((2 * h, batch * seq, LANES), BF16),
        scratch_shapes=[
            pltpu.VMEM((hps, RET_CHUNK, RET_CHUNK), F32),
            pltpu.VMEM((hps, RET_CHUNK, RET_HEAD_DIM), F32),
            pltpu.VMEM((hps, RET_CHUNK, RET_HEAD_DIM), F32),
            pltpu.VMEM((hps, seq // RET_CHUNK, RET_HEAD_DIM, RET_HEAD_DIM), BF16),
        ],
        compiler_params=pltpu.CompilerParams(
            dimension_semantics=("parallel", "parallel"),
            vmem_limit_bytes=VMEM_LIMIT),
        name="retention",
    )(log_gamma, view)


def _outproj_kernel(x_ref, a_ref, r_ref, w_ref, o_ref):
    mixed = jnp.concatenate([a_ref[i] for i in range(a_ref.shape[0])]
                            + [r_ref[i] for i in range(r_ref.shape[0])], axis=-1)
    o_ref[...] = x_ref[...] + jnp.dot(mixed, w_ref[...], preferred_element_type=F32)


def _out_proj(x2d, attn_hm, ret_hm, w_out_bf16, *, tm=1024):
    m, d = x2d.shape
    na, nr = attn_hm.shape[0], ret_hm.shape[0]
    assert (na + nr) * LANES == w_out_bf16.shape[0]
    return pl.pallas_call(
        _outproj_kernel,
        grid=(m // tm,),
        in_specs=[
            pl.BlockSpec((tm, d), lambda i: (i, 0)),
            pl.BlockSpec((na, tm, LANES), lambda i: (0, i, 0)),
            pl.BlockSpec((nr, tm, LANES), lambda i: (0, i, 0)),
            pl.BlockSpec(w_out_bf16.shape, lambda i: (0, 0), pipeline_mode=pl.Buffered(1)),
        ],
        out_specs=pl.BlockSpec((tm, d), lambda i: (i, 0)),
        out_shape=jax.ShapeDtypeStruct((m, d), F32),
        compiler_params=pltpu.CompilerParams(
            dimension_semantics=("parallel",),
            vmem_limit_bytes=VMEM_LIMIT),
        name="out_proj",
    )(x2d, attn_hm, ret_hm, w_out_bf16)


def _ffn_kernel(x_ref, nw_ref, wg_ref, wu_ref, wd_ref, fw_ref, o_ref, h_ref, *, tm, final_norm):
    f = pl.program_id(1)

    @pl.when(f == 0)
    def _():
        def body(i, c):
            r = pl.multiple_of(i * NORM_ROWS, NORM_ROWS)
            x = x_ref[pl.ds(r, NORM_ROWS), :]
            h_ref[pl.ds(r, NORM_ROWS), :] = _rms_rows(x, nw_ref[...]).astype(h_ref.dtype)
            o_ref[pl.ds(r, NORM_ROWS), :] = x
            return c
        lax.fori_loop(0, tm // NORM_ROWS, body, 0)

    h = h_ref[...]
    g = jnp.dot(h, wg_ref[...], preferred_element_type=F32)
    u = jnp.dot(h, wu_ref[...], preferred_element_type=F32)
    a = (g * jax.nn.sigmoid(g) * u).astype(BF16)
    y = jnp.dot(a, wd_ref[...], preferred_element_type=F32)
    o_ref[...] += y

    if final_norm:
        @pl.when(f == pl.num_programs(1) - 1)
        def _():
            def body(i, c):
                r = pl.multiple_of(i * NORM_ROWS, NORM_ROWS)
                o_ref[pl.ds(r, NORM_ROWS), :] = _rms_rows(o_ref[pl.ds(r, NORM_ROWS), :],
                                                          fw_ref[...])
                return c
            lax.fori_loop(0, tm // NORM_ROWS, body, 0)


def _ffn(x2d, norm_w, wg, wu, wd, final_w, *, final_norm, tm=1024, tf=512):
    m, d = x2d.shape
    hid = wg.shape[1]
    return pl.pallas_call(
        functools.partial(_ffn_kernel, tm=tm, final_norm=final_norm),
        grid=(m // tm, hid // tf),
        in_specs=[
            pl.BlockSpec((tm, d), lambda i, f: (i, 0)),
            pl.BlockSpec((1, d), lambda i, f: (0, 0)),
            pl.BlockSpec((d, tf), lambda i, f: (0, f)),
            pl.BlockSpec((d, tf), lambda i, f: (0, f)),
            pl.BlockSpec((tf, d), lambda i, f: (f, 0)),
            pl.BlockSpec((1, d), lambda i, f: (0, 0)),
        ],
        out_specs=pl.BlockSpec((tm, d), lambda i, f: (i, 0)),
        out_shape=jax.ShapeDtypeStruct((m, d), F32),
        scratch_shapes=[pltpu.VMEM((tm, d), BF16)],
        compiler_params=pltpu.CompilerParams(
            dimension_semantics=("parallel", "arbitrary"),
            vmem_limit_bytes=VMEM_LIMIT),
        name="ffn",
    )(x2d, norm_w.reshape(1, d), wg, wu, wd, final_w.reshape(1, d))


def kernel(x, norm_mix_w, w_in, w_out, norm_ffn_w, w_gate, w_up, w_down, norm_final_w):
    batch, seq, d = x.shape
    depth = w_in.shape[0]
    assert seq == DILATIONS[-1] * ATTN_BLK and seq % RET_CHUNK == 0
    slopes = jnp.exp2(-8.0 * jnp.arange(1, N_ATTN_HEADS + 1, dtype=F32) / N_ATTN_HEADS)
    log_gamma = jnp.log(1.0 - jnp.exp2(-5.0 - jnp.arange(N_RET_HEADS, dtype=F32)))

    xs = x.reshape(batch * seq, d)
    for layer in range(depth):
        w_in_l = w_in[layer].astype(BF16)
        side = ((w_gate[layer], 0, FFN_CAST_BLOCKS), (w_up[layer], 0, FFN_CAST_BLOCKS),
                (w_down[layer], FFN_CAST_BLOCKS, FFN_CAST_BLOCKS // 2),
                (w_out[layer], FFN_CAST_BLOCKS, FFN_CAST_BLOCKS // 2))
        nat, mod4, mod16, proj_ret, wg, wu, wd, wo = _in_proj(
            xs, norm_mix_w[layer], w_in_l, side, batch=batch, seq=seq)
        attn = _attention(nat, mod4, mod16, slopes, batch=batch, seq=seq)
        ret = _retention(proj_ret, log_gamma, batch=batch, seq=seq)
        xs = _out_proj(xs, attn, ret, wo)
        xs = _ffn(xs, norm_ffn_w[layer], wg, wu, wd, norm_final_w,
                  final_norm=(layer == depth - 1))
    return xs.reshape(batch, seq, d)
```

```python
import functools
import math

import jax
import jax.numpy as jnp
from jax import lax
from jax.experimental import pallas as pl
from jax.experimental.pallas import tpu as pltpu

F32 = jnp.float32
BF16 = jnp.bfloat16

LANES = 128
ATTN_HEAD_DIM = 128
N_ATTN_HEADS = 8
RET_HEAD_DIM = 256
N_RET_HEADS = 4
RET_CHUNK = 128
ATTN_BLK = 128
DILATIONS = (1, 4, 16)
ATTN_HEADS_PER_STEP = 2
ATTN_LOOKAHEAD = 14
RET_HEADS_PER_STEP = 2
RET_LOOKAHEAD = 1
NORM_EPS = 1e-6
MASK_VALUE = -1e30
LOG2E = math.log2(math.e)
QK_SCALE = LOG2E / math.sqrt(ATTN_HEAD_DIM)
NORM_ROWS = 512
INPROJ_X_ROWS = 256
FFN_CAST_BLOCKS = 64
VMEM_LIMIT = 60 * 1024 * 1024


def _rms_rows(x, w):
    ms = jnp.mean(x * x, axis=-1, keepdims=True)
    return x * lax.rsqrt(ms + NORM_EPS) * w


def _inproj_kernel(*refs, tm, tn, attn_tiles, q_tiles, casts):
    nc = len(casts)
    x_ref, nw_ref, w_ref = refs[:3]
    cast_in = refs[3:3 + nc]
    o1_ref, o4_ref, o16_ref, ret_ref = refs[3 + nc:7 + nc]
    cast_out = refs[7 + nc:7 + 2 * nc]
    h_even, h_odd, nat, mod4 = refs[7 + 2 * nc:]
    i = pl.program_id(0)
    j = pl.program_id(1)
    xr = x_ref.shape[0]
    n4, n16 = tm // 4, tm // 16

    step_no = (i - 1) * pl.num_programs(1) + j
    for src, dst, (first, count) in zip(cast_in, cast_out, casts):
        @pl.when((step_no >= first) & (step_no < first + count))
        def _(src=src, dst=dst):
            dst[...] = src[...].astype(dst.dtype)

    def norm_chunk(h_ref):
        r = pl.multiple_of(jnp.minimum(j, tm // xr - 1) * xr, xr)
        h_ref[pl.ds(r, xr), :] = _rms_rows(x_ref[...], nw_ref[...]).astype(BF16)

    def step(h_wr, h_rd, store):
        norm_chunk(h_wr)
        acc = jnp.dot(h_rd[...], w_ref[...], preferred_element_type=F32)
        for c in range(tn // LANES):
            store(c, acc[:, c * LANES:(c + 1) * LANES])

    @pl.when(i == 0)
    def _():
        norm_chunk(h_even)

    def store_attn(c, slab):
        slab = slab * jnp.where(j < q_tiles, QK_SCALE, 1.0)
        o1_ref[c] = slab.astype(o1_ref.dtype)
        b = c % nat.shape[0]
        nat[b] = slab
        for r in range(4):
            rows = nat[b, pl.ds(r, n4, stride=4), :]
            mod4[b, r * n4:(r + 1) * n4, :] = rows
            o4_ref[c, :, r * LANES:(r + 1) * LANES] = rows.astype(o4_ref.dtype)
        for r in range(16):
            a, r4 = divmod(r, 4)
            rows = mod4[b, pl.ds(r4 * n4 + a, n16, stride=4), :]
            o16_ref[c, :, r * LANES:(r + 1) * LANES] = rows.astype(o16_ref.dtype)

    def store_ret(c, slab):
        ret_ref[c] = slab.astype(ret_ref.dtype)

    for parity, (h_wr, h_rd) in enumerate(((h_even, h_odd), (h_odd, h_even))):
        mine = (i > 0) & (i % 2 == parity)
        pl.when(mine & (j < attn_tiles))(functools.partial(step, h_wr, h_rd, store_attn))
        pl.when(mine & (j >= attn_tiles))(functools.partial(step, h_wr, h_rd, store_ret))


def _in_proj(x2d, norm_w, w_bf16, side_f32, *, batch, seq, tm=1024, tn=1024):
    m, d = x2d.shape
    n = w_bf16.shape[1]
    n_attn = 3 * N_ATTN_HEADS * ATTN_HEAD_DIM
    ns = tn // LANES
    tiles = m // tm
    nj = n // tn
    attn_tiles = n_attn // tn
    assert 0 < attn_tiles < nj
    xr = INPROJ_X_ROWS
    chunks = tm // xr
    assert chunks <= nj
    per_b = seq // tm
    q_tiles = N_ATTN_HEADS * ATTN_HEAD_DIM // tn

    def t_of(i):
        return jnp.maximum(i - 1, 0)

    def ja(i, j):
        return jnp.where(i == 0, 0, jnp.minimum(j, attn_tiles - 1))

    def jr(i, j):
        return jnp.where(i == 0, 0, jnp.maximum(j - attn_tiles, 0))

    casts, cast_specs, cast_shapes = [], [], []
    for arr, first, blocks in side_f32:
        rows, cols = arr.shape
        assert rows % (blocks * 16) == 0
        assert first + blocks <= tiles * nj

        def at(i, j, first=first, blocks=blocks):
            return (jnp.clip((i - 1) * nj + j - first, 0, blocks - 1), 0)

        casts.append((first, blocks))
        cast_specs.append(pl.BlockSpec((rows // blocks, cols), at))
        cast_shapes.append(jax.ShapeDtypeStruct(arr.shape, BF16))

    return pl.pallas_call(
        functools.partial(_inproj_kernel, tm=tm, tn=tn, attn_tiles=attn_tiles, q_tiles=q_tiles,
                          casts=tuple(casts)),
        grid=(tiles + 1, nj),
        in_specs=[
            pl.BlockSpec((xr, d), lambda i, j: (
                jnp.minimum(i, tiles - 1) * chunks + jnp.minimum(j, chunks - 1), 0)),
            pl.BlockSpec((1, d), lambda i, j: (0, 0)),
            pl.BlockSpec((d, tn), lambda i, j: (0, jnp.where(i == 0, 0, j))),
        ] + cast_specs,
        out_specs=[
            pl.BlockSpec((ns, tm, LANES), lambda i, j: (ja(i, j), t_of(i), 0)),
            pl.BlockSpec((ns, None, tm // 4, 4 * LANES),
                         lambda i, j: (ja(i, j), t_of(i) // per_b, t_of(i) % per_b, 0)),
            pl.BlockSpec((ns, None, tm // 16, 16 * LANES),
                         lambda i, j: (ja(i, j), t_of(i) // per_b, t_of(i) % per_b, 0)),
            pl.BlockSpec((ns, tm, LANES), lambda i, j: (jr(i, j), t_of(i), 0)),
        ] + cast_specs,
        out_shape=[
            jax.ShapeDtypeStruct((n_attn // LANES, m, LANES), BF16),
            jax.ShapeDtypeStruct((n_attn // LANES, batch, seq // 4, 4 * LANES), BF16),
            jax.ShapeDtypeStruct((n_attn // LANES, batch, seq // 16, 16 * LANES), BF16),
            jax.ShapeDtypeStruct(((n - n_attn) // LANES, m, LANES), BF16),
        ] + cast_shapes,
        scratch_shapes=[pltpu.VMEM((tm, d), BF16),
                        pltpu.VMEM((tm, d), BF16),
                        pltpu.VMEM((2, tm, LANES), F32),
                        pltpu.VMEM((2, tm, LANES), F32)],
        compiler_params=pltpu.CompilerParams(
            dimension_semantics=("arbitrary", "arbitrary"),
            vmem_limit_bytes=VMEM_LIMIT),
        name="in_proj",
    )(x2d, norm_w.reshape(1, d), w_bf16, *[arr for arr, _, _ in side_f32])


def _attn_kernel(slopes_ref, qkv1, qkv4, qkv16, o_ref, bm, macc, lacc, oacc, onat, *, seq):
    blk = ATTN_BLK
    nblk = seq // blk
    per4 = nblk // 4
    heads = o_ref.shape[0]

    qi = lax.broadcasted_iota(jnp.int32, (blk, 2 * blk), 0)
    kj = lax.broadcasted_iota(jnp.int32, (blk, 2 * blk), 1)
    diff = qi - kj + blk
    valid = (diff >= 0) & (diff <= blk)
    dist = diff.astype(F32)
    for hd in range(heads):
        slope = slopes_ref[pl.program_id(1) * heads + hd]
        for pi, dil in enumerate(DILATIONS):
            bm[hd, pi] = jnp.where(valid, (-LOG2E * slope) * (dist * float(dil)), MASK_VALUE)

    def scores(hd, pi, q_at, k_at):
        kk = k_at()
        bias = bm[hd, pi] if kk.shape[0] == 2 * blk else bm[hd, pi, :, blk:]
        return lax.dot_general(q_at(), kk, (((1,), (1,)), ((), ())),
                               preferred_element_type=F32) + bias

    def finish(s, hd, pi, v_at, dst):
        mx = jnp.max(s, axis=-1, keepdims=True)
        p = jnp.exp2(s - mx).astype(BF16)
        vv = v_at()
        acc = jnp.dot(p, jnp.concatenate([vv, jnp.ones_like(vv)], axis=-1),
                      preferred_element_type=F32)
        oacc[hd, pi, dst, :] = acc[:, :LANES]
        lacc[hd, pi, dst, :] = acc[:, LANES:]
        macc[hd, pi, dst, :] = jnp.broadcast_to(mx, (blk, LANES))

    def p0_block(hd, i):
        rows = slice(i * blk, (i + 1) * blk)
        kr = slice(max(i - 1, 0) * blk, (i + 1) * blk)
        return (hd, 0, lambda: qkv1[0, hd, rows, :], lambda: qkv1[1, hd, kr, :],
                lambda: qkv1[2, hd, kr, :], rows)

    def p1_block(hd, r4, n):
        ln = slice(r4 * LANES, (r4 + 1) * LANES)
        qr = slice(n * blk, (n + 1) * blk)
        kr = slice(max(n - 1, 0) * blk, (n + 1) * blk)
        dst = slice((r4 * per4 + n) * blk, (r4 * per4 + n + 1) * blk)
        return (hd, 1, lambda: qkv4[0, hd, qr, ln], lambda: qkv4[1, hd, kr, ln],
                lambda: qkv4[2, hd, kr, ln], dst)

    def p2_block(hd, r):
        ln = slice(r * LANES, (r + 1) * LANES)
        dst = pl.ds((r % 4) * (seq // 4) + r // 4, blk, stride=4)
        return (hd, 2, lambda: qkv16[0, hd, :, ln], lambda: qkv16[1, hd, :, ln],
                lambda: qkv16[2, hd, :, ln], dst)

    def combine(hd, n):
        for r4 in range(4):
            rows = slice((r4 * per4 + n) * blk, (r4 * per4 + n + 1) * blk)
            nat = pl.ds(r4 + 4 * blk * n, blk, stride=4)
            at = (nat, rows, rows)
            ms = [macc[hd, pi, at[pi], :] for pi in range(3)]
            mx = jnp.maximum(jnp.maximum(ms[0], ms[1]), ms[2])
            es = [jnp.exp2(m - mx) for m in ms]
            den = (es[0] * lacc[hd, 0, nat, :] + es[1] * lacc[hd, 1, rows, :]
                   + es[2] * lacc[hd, 2, rows, :])
            num = (es[0] * oacc[hd, 0, nat, :] + es[1] * oacc[hd, 1, rows, :]
                   + es[2] * oacc[hd, 2, rows, :])
            onat[hd, nat, :] = num * (1.0 / den)
        done = slice(4 * blk * n, 4 * blk * (n + 1))
        o_ref[hd, done, :] = onat[hd, done, :].astype(o_ref.dtype)

    per_head = []
    for hd in range(heads):
        items = [(p2_block(hd, r), None) for r in range(nblk)]
        for n in range(per4):
            group = ([p0_block(hd, 4 * n + a) for a in range(4)]
                     + [p1_block(hd, r4, n) for r4 in range(4)])
            items += [(g, None) for g in group[:-1]] + [(group[-1], n)]
        per_head.append(items)
    work, after = [], {}
    for tup in zip(*per_head):
        for item, n in tup:
            if n is not None:
                after[len(work)] = (item[0], n)
            work.append(item)

    pending = {}
    for t in range(len(work) + ATTN_LOOKAHEAD):
        if t < len(work):
            hd, pi, q_at, k_at, _, _ = work[t]
            pending[t] = scores(hd, pi, q_at, k_at)
        b = t - ATTN_LOOKAHEAD
        if b >= 0:
            hd, pi, _, _, v_at, dst = work[b]
            finish(pending.pop(b), hd, pi, v_at, dst)
            if b in after:
                combine(*after[b])


def _attention(nat, mod4, mod16, slopes, *, batch, seq):
    h = N_ATTN_HEADS
    hps = ATTN_HEADS_PER_STEP
    views = [a.reshape((3, h) + a.shape[1:]) for a in (nat, mod4, mod16)]
    specs = [pl.BlockSpec((3, hps, seq, LANES), lambda b, hh: (0, hh, b, 0))]
    for d in DILATIONS[1:]:
        specs.append(pl.BlockSpec((3, hps, None, seq // d, d * LANES),
                                  lambda b, hh: (0, hh, b, 0, 0)))
    return pl.pallas_call(
        functools.partial(_attn_kernel, seq=seq),
        grid=(batch, h // hps),
        in_specs=[pl.BlockSpec(memory_space=pltpu.SMEM)] + specs,
        out_specs=pl.BlockSpec((hps, seq, LANES), lambda b, hh: (hh, b, 0)),
        out_shape=jax.ShapeDtypeStruct((h, batch * seq, LANES), BF16),
        scratch_shapes=[
            pltpu.VMEM((hps, 3, ATTN_BLK, 2 * ATTN_BLK), F32),
            pltpu.VMEM((hps, 3, seq, LANES), F32),
            pltpu.VMEM((hps, 3, seq, LANES), F32),
            pltpu.VMEM((hps, 3, seq, LANES), F32),
            pltpu.VMEM((hps, seq, LANES), F32),
        ],
        compiler_params=pltpu.CompilerParams(
            dimension_semantics=("parallel", "parallel"),
            vmem_limit_bytes=VMEM_LIMIT),
        name="dilated_attention",
    )(slopes, *views)


def _ret_kernel(lg_ref, qkvg_ref, o_ref, decay, zeta, xi, states, *, seq):
    c = RET_CHUNK
    dh = RET_HEAD_DIM
    nc = seq // c
    heads = qkvg_ref.shape[1]
    k_scale = 1.0 / math.sqrt(dh)
    assert math.log2(k_scale).is_integer()

    ii = lax.broadcasted_iota(jnp.int32, (c, c), 0)
    jj = lax.broadcasted_iota(jnp.int32, (c, c), 1)
    dif = (ii - jj).astype(F32)
    idx = lax.broadcasted_iota(jnp.int32, (c, dh), 0).astype(F32)
    gamma_chunk = []
    for hd in range(heads):
        lg = lg_ref[pl.program_id(1) * heads + hd]
        decay[hd] = jnp.where(dif >= 0, jnp.exp(lg * jnp.maximum(dif, 0.0)), 0.0) * k_scale
        zeta[hd] = jnp.exp(lg * (c - 1.0 - idx)) * k_scale
        xi[hd] = jnp.exp(lg * (idx + 1.0))
        gamma_chunk.append(jnp.exp(jnp.full((dh, dh), lg * c, F32)))

    def wide(t, hd, n):
        rows = slice(n * c, (n + 1) * c)
        return jnp.concatenate([qkvg_ref[t, hd, 0, rows, :], qkvg_ref[t, hd, 1, rows, :]],
                               axis=-1)

    st = [jnp.zeros((dh, dh), F32) for _ in range(heads)]
    for n in range(nc):
        for hd in range(heads):
            states[hd, n] = st[hd].astype(BF16)
            if n + 1 < nc:
                kz = (wide(1, hd, n).astype(F32) * zeta[hd]).astype(BF16)
                kv = lax.dot_general(kz, wide(2, hd, n), (((0,), (0,)), ((), ())),
                                     preferred_element_type=F32)
                st[hd] = st[hd] * gamma_chunk[hd] + kv

    def front(hd, n):
        qn = wide(0, hd, n)
        sc = lax.dot_general(qn, wide(1, hd, n), (((1,), (1,)), ((), ())),
                             preferred_element_type=F32) * decay[hd]
        cross = jnp.dot(qn, states[hd, n], preferred_element_type=F32) * xi[hd]
        return sc.astype(BF16), cross

    def back(hd, n, sc, cross):
        ret = jnp.dot(sc, wide(2, hd, n), preferred_element_type=F32) + cross
        ret = ret * lax.rsqrt(jnp.mean(ret * ret, axis=-1, keepdims=True) + NORM_EPS)
        gate = wide(3, hd, n).astype(F32)
        out = (gate * jax.nn.sigmoid(gate) * ret).astype(o_ref.dtype)
        o_ref[2 * hd, n * c:(n + 1) * c, :] = out[:, :LANES]
        o_ref[2 * hd + 1, n * c:(n + 1) * c, :] = out[:, LANES:]

    pending = {}
    for t in range(nc + RET_LOOKAHEAD):
        for hd in range(heads):
            if t < nc:
                pending[hd, t] = front(hd, t)
            if t >= RET_LOOKAHEAD:
                back(hd, t - RET_LOOKAHEAD, *pending.pop((hd, t - RET_LOOKAHEAD)))


def _retention(proj_hm, log_gamma, *, batch, seq):
    h = N_RET_HEADS
    hps = RET_HEADS_PER_STEP
    view = proj_hm.reshape((4, h, 2) + proj_hm.shape[1:])
    return pl.pallas_call(
        functools.partial(_ret_kernel, seq=seq),
        grid=(batch, h // hps),
        in_specs=[
            pl.BlockSpec(memory_space=pltpu.SMEM),
            pl.BlockSpec((4, hps, 2, seq, LANES), lambda b, hh: (0, hh, 0, b, 0)),
        ],
        out_specs=pl.BlockSpec((2 * hps, seq, LANES), lambda b, hh: (hh, b, 0)),
        out_shape=jax.ShapeDtypeStruct((2 * h, batch * seq, LANES), BF16),
        scratch_shapes=[
            pltpu.VMEM((hps, RET_CHUNK, RET_CHUNK), F32),
            pltpu.VMEM((hps, RET_CHUNK, RET_HEAD_DIM), F32),
            pltpu.VMEM((hps, RET_CHUNK, RET_HEAD_DIM), F32),
            pltpu.VMEM((hps, seq // RET_CHUNK, RET_HEAD_DIM, RET_HEAD_DIM), BF16),
        ],
        compiler_params=pltpu.CompilerParams(
            dimension_semantics=("parallel", "parallel"),
            vmem_limit_bytes=VMEM_LIMIT),
        name="retention",
    )(log_gamma, view)


def _mixer_kernel(slopes_ref, lg_ref, qkv1, qkv4, qkv16, qkvg_ref, ao_ref, ro_ref,
                  bm, macc, lacc, oacc, onat, decay, zeta, xi, states, *, seq):
    _ret_kernel(lg_ref, qkvg_ref, ro_ref, decay, zeta, xi, states, seq=seq)
    _attn_kernel(slopes_ref, qkv1, qkv4, qkv16, ao_ref, bm, macc, lacc, oacc, onat, seq=seq)


def _mixer(nat, mod4, mod16, proj_ret, slopes, log_gamma, *, batch, seq):
    h, hr = N_ATTN_HEADS, N_RET_HEADS
    hps = ATTN_HEADS_PER_STEP
    steps = h // hps
    assert steps == hr
    views = [a.reshape((3, h) + a.shape[1:]) for a in (nat, mod4, mod16)]
    specs = [pl.BlockSpec((3, hps, seq, LANES), lambda b, hh: (0, hh, b, 0))]
    for d in DILATIONS[1:]:
        specs.append(pl.BlockSpec((3, hps, None, seq // d, d * LANES),
                                  lambda b, hh: (0, hh, b, 0, 0)))
    rview = proj_ret.reshape((4, hr, 2) + proj_ret.shape[1:])
    specs.append(pl.BlockSpec((4, 1, 2, seq, LANES), lambda b, hh: (0, hh, 0, b, 0)))
    smem = pl.BlockSpec(memory_space=pltpu.SMEM)
    return pl.pallas_call(
        functools.partial(_mixer_kernel, seq=seq),
        grid=(batch, steps),
        in_specs=[smem, smem] + specs,
        out_specs=[pl.BlockSpec((hps, seq, LANES), lambda b, hh: (hh, b, 0)),
                   pl.BlockSpec((2, seq, LANES), lambda b, hh: (hh, b, 0))],
        out_shape=[jax.ShapeDtypeStruct((h, batch * seq, LANES), BF16),
                   jax.ShapeDtypeStruct((2 * hr, batch * seq, LANES), BF16)],
        scratch_shapes=[
            pltpu.VMEM((hps, 3, ATTN_BLK, 2 * ATTN_BLK), F32),
            pltpu.VMEM((hps, 3, seq, LANES), F32),
            pltpu.VMEM((hps, 3, seq, LANES), F32),
            pltpu.VMEM((hps, 3, seq, LANES), F32),
            pltpu.VMEM((hps, seq, LANES), F32),
            pltpu.VMEM((1, RET_CHUNK, RET_CHUNK), F32),
            pltpu.VMEM((1, RET_CHUNK, RET_HEAD_DIM), F32),
            pltpu.VMEM((1, RET_CHUNK, RET_HEAD_DIM), F32),
            pltpu.VMEM((1, seq // RET_CHUNK, RET_HEAD_DIM, RET_HEAD_DIM), BF16),
        ],
        compiler_params=pltpu.CompilerParams(
            dimension_semantics=("parallel", "parallel"),
            vmem_limit_bytes=VMEM_LIMIT),
        name="mixer",
    )(slopes, log_gamma, *views, rview)


def _outproj_kernel(x_ref, a_ref, r_ref, w_ref, o_ref):
    mixed = jnp.concatenate([a_ref[i] for i in range(a_ref.shape[0])]
                            + [r_ref[i] for i in range(r_ref.shape[0])], axis=-1)
    o_ref[...] = x_ref[...] + jnp.dot(mixed, w_ref[...], preferred_element_type=F32)


def _out_proj(x2d, attn_hm, ret_hm, w_out_bf16, *, tm=1024):
    m, d = x2d.shape
    na, nr = attn_hm.shape[0], ret_hm.shape[0]
    assert (na + nr) * LANES == w_out_bf16.shape[0]
    return pl.pallas_call(
        _outproj_kernel,
        grid=(m // tm,),
        in_specs=[
            pl.BlockSpec((tm, d), lambda i: (i, 0)),
            pl.BlockSpec((na, tm, LANES), lambda i: (0, i, 0)),
            pl.BlockSpec((nr, tm, LANES), lambda i: (0, i, 0)),
            pl.BlockSpec(w_out_bf16.shape, lambda i: (0, 0), pipeline_mode=pl.Buffered(1)),
        ],
        out_specs=pl.BlockSpec((tm, d), lambda i: (i, 0)),
        out_shape=jax.ShapeDtypeStruct((m, d), F32),
        compiler_params=pltpu.CompilerParams(
            dimension_semantics=("parallel",),
            vmem_limit_bytes=VMEM_LIMIT),
        name="out_proj",
    )(x2d, attn_hm, ret_hm, w_out_bf16)


def _ffn_kernel(x_ref, nw_ref, wg_ref, wu_ref, wd_ref, fw_ref, o_ref, h_ref, *, tm, final_norm):
    f = pl.program_id(1)

    @pl.when(f == 0)
    def _():
        def body(i, c):
            r = pl.multiple_of(i * NORM_ROWS, NORM_ROWS)
            x = x_ref[pl.ds(r, NORM_ROWS), :]
            h_ref[pl.ds(r, NORM_ROWS), :] = _rms_rows(x, nw_ref[...]).astype(h_ref.dtype)
            o_ref[pl.ds(r, NORM_ROWS), :] = x
            return c
        lax.fori_loop(0, tm // NORM_ROWS, body, 0)

    h = h_ref[...]
    g = jnp.dot(h, wg_ref[...], preferred_element_type=F32)
    u = jnp.dot(h, wu_ref[...], preferred_element_type=F32)
    a = (g * jax.nn.sigmoid(g) * u).astype(BF16)
    y = jnp.dot(a, wd_ref[...], preferred_element_type=F32)
    o_ref[...] += y

    if final_norm:
        @pl.when(f == pl.num_programs(1) - 1)
        def _():
            def body(i, c):
                r = pl.multiple_of(i * NORM_ROWS, NORM_ROWS)
                o_ref[pl.ds(r, NORM_ROWS), :] = _rms_rows(o_ref[pl.ds(r, NORM_ROWS), :],
                                                          fw_ref[...])
                return c
            lax.fori_loop(0, tm // NORM_ROWS, body, 0)


def _ffn(x2d, norm_w, wg, wu, wd, final_w, *, final_norm, tm=1024, tf=512):
    m, d = x2d.shape
    hid = wg.shape[1]
    return pl.pallas_call(
        functools.partial(_ffn_kernel, tm=tm, final_norm=final_norm),
        grid=(m // tm, hid // tf),
        in_specs=[
            pl.BlockSpec((tm, d), lambda i, f: (i, 0)),
            pl.BlockSpec((1, d), lambda i, f: (0, 0)),
            pl.BlockSpec((d, tf), lambda i, f: (0, f)),
            pl.BlockSpec((d, tf), lambda i, f: (0, f)),
            pl.BlockSpec((tf, d), lambda i, f: (f, 0)),
            pl.BlockSpec((1, d), lambda i, f: (0, 0)),
        ],
        out_specs=pl.BlockSpec((tm, d), lambda i, f: (i, 0)),
        out_shape=jax.ShapeDtypeStruct((m, d), F32),
        scratch_shapes=[pltpu.VMEM((tm, d), BF16)],
        compiler_params=pltpu.CompilerParams(
            dimension_semantics=("parallel", "arbitrary"),
            vmem_limit_bytes=VMEM_LIMIT),
        name="ffn",
    )(x2d, norm_w.reshape(1, d), wg, wu, wd, final_w.reshape(1, d))


def kernel(x, norm_mix_w, w_in, w_out, norm_ffn_w, w_gate, w_up, w_down, norm_final_w):
    batch, seq, d = x.shape
    depth = w_in.shape[0]
    assert seq == DILATIONS[-1] * ATTN_BLK and seq % RET_CHUNK == 0
    slopes = jnp.exp2(-8.0 * jnp.arange(1, N_ATTN_HEADS + 1, dtype=F32) / N_ATTN_HEADS)
    log_gamma = jnp.log(1.0 - jnp.exp2(-5.0 - jnp.arange(N_RET_HEADS, dtype=F32)))

    xs = x.reshape(batch * seq, d)
    for layer in range(depth):
        w_in_l = w_in[layer].astype(BF16)
        side = ((w_gate[layer], 0, FFN_CAST_BLOCKS), (w_up[layer], 0, FFN_CAST_BLOCKS),
                (w_down[layer], FFN_CAST_BLOCKS, FFN_CAST_BLOCKS // 2),
                (w_out[layer], FFN_CAST_BLOCKS, FFN_CAST_BLOCKS // 2))
        nat, mod4, mod16, proj_ret, wg, wu, wd, wo = _in_proj(
            xs, norm_mix_w[layer], w_in_l, side, batch=batch, seq=seq)
        attn, ret = _mixer(nat, mod4, mod16, proj_ret, slopes, log_gamma, batch=batch, seq=seq)
        xs = _out_proj(xs, attn, ret, wo)
        xs = _ffn(xs, norm_ffn_w[layer], wg, wu, wd, norm_final_w,
                  final_norm=(layer == depth - 1))
    return xs.reshape(batch, seq, d)
```
